```python
import math
import jax, jax.numpy as jnp
from jax import lax
import numpy as np

D_MODEL = 1024
BATCH = 4
SEQ = 4096
DEPTH = 4
DEC_BATCH = 128
DEC_SEQ = 8
PAST_LEN = 2048
PAGE_SIZE = 128

POOL_DIM = D_MODEL // 4
POOL_WINDOWS = (2, 4, 8, 16)
POOL_GROUP = POOL_DIM // len(POOL_WINDOWS)
POOL_HIST = max(POOL_WINDOWS) - 1
HEAD_DIM = 64
ATT_GROUPS = ((128, 1), (512, 4), (2048, 16))
N_DIL = len(ATT_GROUPS)
ATT_DIM = D_MODEL - POOL_DIM
HEADS_PER_GROUP = ATT_DIM // (N_DIL * HEAD_DIM)
N_ATT_HEADS = N_DIL * HEADS_PER_GROUP
MERGED_DIM = HEADS_PER_GROUP * HEAD_DIM
IN_DIM = POOL_DIM + 3 * ATT_DIM
OUT_IN_DIM = POOL_DIM + MERGED_DIM
ATT_BLOCK = 128
REL_BUCKETS = 32
REL_MAX_DIST = 2048
N_EXPERTS = 32
TOP_K = 4
D_FF = D_MODEL
SWIGLU_LIMIT = 7.0
SWIGLU_ALPHA = 1.702
MOE_BLOCK = 128
EPS = 1e-6
NEG_INF = -1e30

kernel_name = 'hymba_pool_dilated_moe_adaln_step'


def _rmsnorm(x, g):
    xf = x.astype(jnp.float32)
    inv = lax.rsqrt(jnp.mean(xf * xf, axis=-1, keepdims=True) + EPS)
    return (xf * inv * g.astype(jnp.float32)).astype(x.dtype)


def _adaln(c, w, b):
    m = jax.nn.silu(c) @ w + b
    return jnp.split(m[:, None, :], 6, axis=-1)


def _rel_bucket(dist):
    dist = dist.astype(jnp.int32)
    max_exact = REL_BUCKETS // 2
    d_f = jnp.maximum(dist, 1).astype(jnp.float32)
    large = max_exact + (jnp.log(d_f / max_exact) / math.log(REL_MAX_DIST / max_exact)
                         * (REL_BUCKETS - max_exact)).astype(jnp.int32)
    large = jnp.minimum(large, REL_BUCKETS - 1)
    return jnp.where(dist < max_exact, dist, large)


def _pool_mixer(u, hist, pos0, w_l, scale_l):
    B, T, C = u.shape
    ext_raw = jnp.concatenate([hist.astype(u.dtype), u], axis=1)
    ext = ext_raw.astype(jnp.float32)
    cs = jnp.pad(jnp.cumsum(ext, axis=1), ((0, 0), (1, 0), (0, 0)))
    pos = pos0 + jnp.arange(T)
    outs = []
    for g, w in enumerate(POOL_WINDOWS):
        ch = slice(g * POOL_GROUP, (g + 1) * POOL_GROUP)
        s = cs[:, POOL_HIST + 1:POOL_HIST + 1 + T, ch] - cs[:, POOL_HIST + 1 - w:POOL_HIST + 1 - w + T, ch]
        cnt = jnp.minimum(w, pos + 1).astype(jnp.float32)
        zg = s / cnt[None, :, None] - ext[:, POOL_HIST:, ch]
        outs.append(jnp.einsum('btc,cd->btd', zg, w_l[g].astype(jnp.float32)))
    y = jnp.concatenate(outs, axis=-1) * scale_l.astype(jnp.float32)
    return y.astype(u.dtype), ext_raw[:, -POOL_HIST:]


def _head_rmsnorm(x, g):
    xf = x.astype(jnp.float32)
    inv = lax.rsqrt(jnp.mean(xf * xf, axis=-1, keepdims=True) + EPS)
    return (xf * inv * g.astype(jnp.float32)).astype(x.dtype)


def _dilated_attn_prompt(q, k, v, bias_g, window, dil):
    B, S, H, Dh = q.shape
    L = S // dil
    nb = -(-L // ATT_BLOCK)
    Lp = nb * ATT_BLOCK
    scale = HEAD_DIM ** -0.5

    def by_residue(a):
        a = a.reshape(B, L, dil, H, Dh).transpose(0, 2, 1, 3, 4)
        return jnp.pad(a, ((0, 0), (0, 0), (0, Lp - L), (0, 0), (0, 0)))

    def band(a):
        cur = a.reshape(B, dil, nb, ATT_BLOCK, H, Dh)
        prev = jnp.pad(cur, ((0, 0), (0, 0), (1, 0), (0, 0), (0, 0), (0, 0)))[:, :, :nb]
        return jnp.concatenate([prev, cur], axis=3)

    qb = by_residue(q).reshape(B, dil, nb, ATT_BLOCK, H, Dh)
    kb = band(by_residue(k))
    vb = band(by_residue(v))
    logits = jnp.einsum('brnqhd,brnkhd->brnhqk', qb, kb).astype(jnp.float32) * scale
    qi = jnp.arange(ATT_BLOCK)[:, None]
    kj = jnp.arange(2 * ATT_BLOCK)[None, :]
    delta = qi + ATT_BLOCK - kj
    band_ok = (delta >= 0) & (delta <= window // dil)
    start_ok = (jnp.arange(nb)[:, None, None] > 0) | (kj >= ATT_BLOCK)[None]
    mask = band_ok[None] & start_ok
    bias = bias_g[_rel_bucket(jnp.maximum(delta, 0) * dil)].astype(jnp.float32)
    logits = logits + bias.transpose(2, 0, 1)
    logits = jnp.where(mask[:, None], logits, NEG_INF)
    m = jnp.max(logits, axis=-1, keepdims=True)
    p = jnp.exp(logits - m)
    s = jnp.sum(p, axis=-1, keepdims=True)
    o = jnp.einsum('brnhqk,brnkhd->brnqhd', p, vb.astype(jnp.float32))
    o = o / jnp.swapaxes(s[..., 0], 3, 4)[..., None]
    lse = jnp.swapaxes((m + jnp.log(s))[..., 0], 3, 4)
    o = o.reshape(B, dil, Lp, H, Dh)[:, :, :L].transpose(0, 2, 1, 3, 4).reshape(B, S, H, Dh)
    lse = lse.reshape(B, dil, Lp, H)[:, :, :L].transpose(0, 2, 1, 3).reshape(B, S, H)
    return o, lse


def _dilated_attn_cached(q, k, v, k_buf, v_buf, bias_g, window, dil):
    Bd, T, H, Dh = q.shape
    Lb = k_buf.shape[1]
    kk = jnp.concatenate([k_buf.astype(k.dtype), k], axis=1)
    vv = jnp.concatenate([v_buf.astype(v.dtype), v], axis=1)
    steps = jnp.arange(window // dil + 1)
    idx = Lb + jnp.arange(T)[:, None] - steps[None, :] * dil
    valid = idx >= 0
    idx = jnp.maximum(idx, 0)
    kg = jnp.take(kk, idx, axis=1)
    vg = jnp.take(vv, idx, axis=1)
    logits = jnp.einsum('bthd,btkhd->bhtk', q, kg).astype(jnp.float32) * HEAD_DIM ** -0.5
    bias = bias_g[_rel_bucket(steps * dil)].astype(jnp.float32)
    logits = logits + bias.T[:, None, :]
    logits = jnp.where(valid[None, None], logits, NEG_INF)
    m = jnp.max(logits, axis=-1, keepdims=True)
    p = jnp.exp(logits - m)
    s = jnp.sum(p, axis=-1, keepdims=True)
    o = jnp.einsum('bhtk,btkhd->bthd', p, vg.astype(jnp.float32))
    o = o / jnp.swapaxes(s[..., 0], 1, 2)[..., None]
    lse = jnp.swapaxes((m + jnp.log(s))[..., 0], 1, 2)
    return o, lse


def _moe(h, w_r, b_r, w1, b1, w2, b2):
    B, T, D = h.shape
    x = h.reshape(-1, D)
    N = x.shape[0]
    logits = (x @ w_r + b_r).astype(jnp.float32)
    top_v, top_i = lax.top_k(logits, TOP_K)
    gates = jax.nn.softmax(top_v, axis=-1)
    M = N * TOP_K
    e_flat = top_i.reshape(-1)
    tok_flat = jnp.repeat(jnp.arange(N, dtype=jnp.int32), TOP_K)
    g_flat = gates.reshape(-1)
    order = jnp.argsort(e_flat)
    e_s, tok_s, g_s = e_flat[order], tok_flat[order], g_flat[order]
    counts = jnp.bincount(e_flat, length=N_EXPERTS)
    starts = jnp.cumsum(counts) - counts
    blocks_per = (counts + MOE_BLOCK - 1) // MOE_BLOCK
    block_end = jnp.cumsum(blocks_per)
    pstarts = (block_end - blocks_per) * MOE_BLOCK
    slot = pstarts[e_s] + (jnp.arange(M) - starts[e_s])
    n_blocks = -(-M // MOE_BLOCK) + N_EXPERTS
    n_slots = n_blocks * MOE_BLOCK
    slot_tok = jnp.full((n_slots,), N, jnp.int32).at[slot].set(tok_s)
    slot_gate = jnp.zeros((n_slots,), jnp.float32).at[slot].set(g_s)
    block_expert = jnp.minimum(jnp.searchsorted(block_end, jnp.arange(n_blocks), side='right'), N_EXPERTS - 1)
    x_pad = jnp.concatenate([x, jnp.zeros((1, D), x.dtype)], axis=0)

    def expert_block(args):
        tok_b, e = args
        xb = x_pad[tok_b]
        hc = xb @ w1[e] + b1[e]
        gate = jnp.minimum(hc[:, :D_FF], SWIGLU_LIMIT)
        up = jnp.clip(hc[:, D_FF:], -SWIGLU_LIMIT, SWIGLU_LIMIT)
        act = gate * jax.nn.sigmoid(SWIGLU_ALPHA * gate) * (up + 1)
        return act @ w2[e] + b2[e]

    y_slots = lax.map(expert_block, (slot_tok.reshape(n_blocks, MOE_BLOCK), block_expert))
    y_slots = y_slots.reshape(n_slots, D).astype(jnp.float32) * slot_gate[:, None]
    y = jax.ops.segment_sum(y_slots, slot_tok, num_segments=N + 1)[:N]
    return y.reshape(B, T, D).astype(h.dtype)


def _trunk(x, c, past_kv, past_pool, pos0, params):
    (rel_bias, norm_mix, norm_ffn, w_ada, b_ada, w_in, q_norm, k_norm, pool_w, pool_scale,
     w_out, w_router, b_router, w_expert_in, b_expert_in, w_expert_out, b_expert_out) = params
    prompt = past_kv is None
    B, T, _ = x.shape
    kv_new = [[] for _ in ATT_GROUPS]
    pool_new = []
    for l in range(DEPTH):
        sh1, sc1, g1, sh2, sc2, g2 = _adaln(c, w_ada[l], b_ada[l])
        h = _rmsnorm(x, norm_mix[l]) * (1 + sc1) + sh1
        z = h @ w_in[l]
        u = z[..., :POOL_DIM]
        qkv = z[..., POOL_DIM:].reshape(B, T, 3, N_DIL, HEADS_PER_GROUP, HEAD_DIM)
        q = _head_rmsnorm(qkv[:, :, 0], q_norm[l])
        k = _head_rmsnorm(qkv[:, :, 1], k_norm[l])
        v = qkv[:, :, 2]
        hist = jnp.zeros((B, POOL_HIST, POOL_DIM), x.dtype) if prompt else past_pool[l]
        pool_y, pool_state = _pool_mixer(u, hist, pos0, pool_w[l], pool_scale[l])
        pool_new.append(pool_state)
        outs, lses = [], []
        for g, (win, dil) in enumerate(ATT_GROUPS):
            bias_g = rel_bias[:, g * HEADS_PER_GROUP:(g + 1) * HEADS_PER_GROUP]
            qg, kg, vg = q[:, :, g], k[:, :, g], v[:, :, g]
            if prompt:
                o, lse = _dilated_attn_prompt(qg, kg, vg, bias_g, win, dil)
                keep = min(win, T)
                kv_new[g].append(jnp.stack([kg[:, -keep:], vg[:, -keep:]], axis=2))
            else:
                buf = past_kv[g][l]
                o, lse = _dilated_attn_cached(qg, kg, vg, buf[:, :, 0], buf[:, :, 1], bias_g, win, dil)
                kv_new[g].append(jnp.stack([kg, vg], axis=2))
            outs.append(o)
            lses.append(lse)
        wts = jax.nn.softmax(jnp.stack(lses, axis=2), axis=2)
        attn = jnp.einsum('btgh,btghd->bthd', wts, jnp.stack(outs, axis=2))
        attn = attn.reshape(B, T, MERGED_DIM).astype(x.dtype)
        x = x + g1 * (jnp.concatenate([pool_y, attn], axis=-1) @ w_out[l])
        h = _rmsnorm(x, norm_ffn[l]) * (1 + sc2) + sh2
        x = x + g2 * _moe(h, w_router[l], b_router[l], w_expert_in[l], b_expert_in[l],
                          w_expert_out[l], b_expert_out[l])
    kv_stacked = [jnp.stack(kv, axis=0) for kv in kv_new]
    return x, kv_stacked, jnp.stack(pool_new, axis=0)


def setup_inputs(seed: int = 0) -> dict:
    key = jax.random.key(seed)
    ks = jax.random.split(key, 32)
    f32 = jnp.float32

    def nrm(k, shape, s):
        return jax.random.normal(k, shape, f32) * s

    bufs = [min(w, PAST_LEN) for w, _ in ATT_GROUPS]
    return {
        'x_prompt': nrm(ks[0], (BATCH, SEQ, D_MODEL), 1.0),
        'x_sample': nrm(ks[1], (DEC_BATCH, DEC_SEQ, D_MODEL), 1.0),
        'cache_kv_w128_d1': nrm(ks[2], (DEPTH, DEC_BATCH, bufs[0], 2, HEADS_PER_GROUP, HEAD_DIM), 1.0),
        'cache_kv_w512_d4': nrm(ks[3], (DEPTH, DEC_BATCH, bufs[1], 2, HEADS_PER_GROUP, HEAD_DIM), 1.0),
        'cache_kv_w2048_d16': nrm(ks[4], (DEPTH, DEC_BATCH, bufs[2], 2, HEADS_PER_GROUP, HEAD_DIM), 1.0),
        'state_pool': nrm(ks[5], (DEPTH, DEC_BATCH, POOL_HIST, POOL_DIM), 1.0),
        'c_prompt': nrm(ks[6], (BATCH, D_MODEL), 1.0),
        'c_sample': nrm(ks[7], (DEC_BATCH, D_MODEL), 1.0),
        'rel_bias': nrm(ks[8], (REL_BUCKETS, N_ATT_HEADS), 0.5),
        'norm_mix': 1.0 + nrm(ks[9], (DEPTH, D_MODEL), 0.05),
        'norm_ffn': 1.0 + nrm(ks[10], (DEPTH, D_MODEL), 0.05),
        'w_ada': nrm(ks[11], (DEPTH, D_MODEL, 6 * D_MODEL), 0.5 * D_MODEL ** -0.5),
        'b_ada': nrm(ks[12], (DEPTH, 6 * D_MODEL), 0.02),
        'w_in': nrm(ks[13], (DEPTH, D_MODEL, IN_DIM), D_MODEL ** -0.5),
        'q_norm': 1.0 + nrm(ks[14], (DEPTH, HEAD_DIM), 0.05),
        'k_norm': 1.0 + nrm(ks[15], (DEPTH, HEAD_DIM), 0.05),
        'pool_w': nrm(ks[16], (DEPTH, len(POOL_WINDOWS), POOL_GROUP, POOL_GROUP), POOL_GROUP ** -0.5),
        'pool_scale': 1.0 + nrm(ks[17], (DEPTH, POOL_DIM), 0.1),
        'w_out': nrm(ks[18], (DEPTH, OUT_IN_DIM, D_MODEL), OUT_IN_DIM ** -0.5),
        'w_router': nrm(ks[19], (DEPTH, D_MODEL, N_EXPERTS), D_MODEL ** -0.5),
        'b_router': nrm(ks[20], (DEPTH, N_EXPERTS), 0.01),
        'w_expert_in': nrm(ks[21], (DEPTH, N_EXPERTS, D_MODEL, 2 * D_FF), D_MODEL ** -0.5),
        'b_expert_in': nrm(ks[22], (DEPTH, N_EXPERTS, 2 * D_FF), 0.01),
        'w_expert_out': nrm(ks[23], (DEPTH, N_EXPERTS, D_FF, D_MODEL), D_FF ** -0.5),
        'b_expert_out': nrm(ks[24], (DEPTH, N_EXPERTS, D_MODEL), 0.01),
    }


def reference(x_prompt, x_sample, cache_kv_w128_d1, cache_kv_w512_d4, cache_kv_w2048_d16, state_pool,
              c_prompt, c_sample, rel_bias, norm_mix, norm_ffn, w_ada, b_ada, w_in, q_norm, k_norm,
              pool_w, pool_scale, w_out, w_router, b_router, w_expert_in, b_expert_in,
              w_expert_out, b_expert_out):
    params = (rel_bias, norm_mix, norm_ffn, w_ada, b_ada, w_in, q_norm, k_norm, pool_w, pool_scale,
              w_out, w_router, b_router, w_expert_in, b_expert_in, w_expert_out, b_expert_out)
    y_prompt, kv_p, pool_p = _trunk(x_prompt, c_prompt, None, None, 0, params)
    y_sample, kv_s, pool_s = _trunk(x_sample, c_sample,
                                    (cache_kv_w128_d1, cache_kv_w512_d4, cache_kv_w2048_d16),
                                    state_pool, PAST_LEN, params)
    return (y_prompt, y_sample, kv_p[0], kv_p[1], kv_p[2], pool_p, kv_s[0], kv_s[1], kv_s[2], pool_s)
```

```python
import functools
import math

import numpy as np
import jax
import jax.numpy as jnp
from jax import lax
from jax.experimental import pallas as pl
from jax.experimental.pallas import tpu as pltpu

F32 = jnp.float32
BF16 = jnp.bfloat16

HEAD_DIM = 64
HEADS = 4
GROUP_DIM = HEADS * HEAD_DIM
ATT_GROUPS = ((128, 1), (512, 4), (2048, 16))
N_GROUPS = len(ATT_GROUPS)
BAND = 128
POOL_WINDOWS = (2, 4, 8, 16)
POOL_HIST = 15
POOL_DIM = 256
POOL_PAD = 16
REL_BUCKETS = 32
REL_MAX_DIST = 2048
N_EXPERTS = 32
TOP_K = 4
SWIGLU_LIMIT = 7.0
SWIGLU_ALPHA = 1.702
EPS = 1e-6
NEG_INF = -1e30
PAST_LEN = 2048

SUBLANES = 8
LANES = 128
TM = 512
GP = TM // SUBLANES
MOE_TM = 256
VMEM_LIMIT = 52 * 1024 * 1024


def _cparams(n_axes, vmem=None):
    return pltpu.CompilerParams(dimension_semantics=("arbitrary",) * n_axes,
                                vmem_limit_bytes=vmem)


def _lane_head(width=GROUP_DIM):
    return lax.broadcasted_iota(jnp.int32, (1, width), 1) // HEAD_DIM


def _ada_kernel(c_ref, w_ref, b_ref, o_ref):
    c = c_ref[...]
    a = (c * jax.nn.sigmoid(c)).astype(BF16)
    o_ref[...] = jnp.dot(a, w_ref[...].astype(BF16), preferred_element_type=F32) + b_ref[...]


def _ada_mods(c_all, w_ada, b_ada):
    depth, d, d6 = w_ada.shape
    bc = c_all.shape[0]
    tn = d6 // 4
    return pl.pallas_call(
        _ada_kernel,
        grid=(depth, d6 // tn),
        in_specs=[pl.BlockSpec((bc, d), lambda l, j: (0, 0)),
                  pl.BlockSpec((None, d, tn), lambda l, j: (l, 0, j)),
                  pl.BlockSpec((None, 1, tn), lambda l, j: (l, 0, j))],
        out_specs=pl.BlockSpec((None, bc, tn), lambda l, j: (l, 0, j)),
        out_shape=jax.ShapeDtypeStruct((depth, bc, d6), F32),
        compiler_params=_cparams(2, VMEM_LIMIT),
        name="ada_mods",
    )(c_all, w_ada, b_ada.reshape(depth, 1, d6))


def _mod_specs(cfg, col):
    d = cfg["d"]
    n_pt, tps, b = cfg["n_pt"], cfg["tps"], cfg["b"]
    return [pl.BlockSpec((1, 1, d), lambda i: (jnp.minimum(i // tps, b - 1), 0, col)),
            pl.BlockSpec((GP, 1, d), lambda i: (jnp.maximum(i - n_pt, 0), 0, col))]


def _rmsnorm_mod(x, g, sc, sh):
    inv = lax.rsqrt(jnp.mean(x * x, axis=-1, keepdims=True) + EPS)
    return x * inv * g * (1.0 + sc) + sh


def _inproj_kernel(cfg, x_ref, shp_ref, shs_ref, scp_ref, scs_ref, g_ref, w_ref, qg_ref, kg_ref,
                   up_ref, us_ref, qp_ref, kp_ref, vp_ref, qs_ref, kvs_ref, *kvp_refs):
    i = pl.program_id(0)
    n_pt, tps, d = cfg["n_pt"], cfg["tps"], cfg["d"]
    is_p = i < n_pt
    sh = jnp.where(is_p, shp_ref[...], shs_ref[...])
    sc = jnp.where(is_p, scp_ref[...], scs_ref[...])
    h = _rmsnorm_mod(x_ref[...], g_ref[...], sc, sh)
    z = jnp.dot(h.reshape(TM, d).astype(BF16), w_ref[...], preferred_element_type=F32)

    r = lax.broadcasted_iota(jnp.int32, (GROUP_DIM, GROUP_DIM), 0) // HEAD_DIM
    c = lax.broadcasted_iota(jnp.int32, (GROUP_DIM, GROUP_DIM), 1) // HEAD_DIM
    head_ones = (r == c).astype(BF16)

    def head_norm(t, gain):
        ss = jnp.dot((t * t).astype(BF16), head_ones, preferred_element_type=F32) * (1.0 / HEAD_DIM)
        return t * lax.rsqrt(ss + EPS) * gain

    u = z[:, :POOL_DIM]
    att = d - POOL_DIM
    q, k, v = [], [], []
    for g in range(N_GROUPS):
        lo = POOL_DIM + g * GROUP_DIM
        q.append(head_norm(z[:, lo:lo + GROUP_DIM], qg_ref[...]) * (HEAD_DIM ** -0.5))
        k.append(head_norm(z[:, lo + att:lo + att + GROUP_DIM], kg_ref[...]))
        v.append(z[:, lo + 2 * att:lo + 2 * att + GROUP_DIM])

    @pl.when(is_p)
    def _():
        up_ref[...] = u
        for g in range(N_GROUPS):
            qp_ref[g] = q[g].astype(BF16)
            kp_ref[g] = k[g].astype(BF16)
            vp_ref[g] = v[g].astype(BF16)

    @pl.when(jnp.logical_not(is_p))
    def _():
        us_ref[...] = u
        for g in range(N_GROUPS):
            qs_ref[g] = q[g]
            kvs_ref[g, :, :GROUP_DIM] = k[g]
            kvs_ref[g, :, GROUP_DIM:] = v[g]

    j = jnp.minimum(i, n_pt - 1) % tps
    for g, (keep, nk) in enumerate(cfg["kv_keep"]):
        rows = min(keep, TM)

        @pl.when(jnp.logical_and(is_p, j >= tps - nk))
        def _(g=g, rows=rows):
            kvp_refs[g][:, :GROUP_DIM] = k[g][TM - rows:]
            kvp_refs[g][:, GROUP_DIM:] = v[g][TM - rows:]


def _inproj(cfg, x3, mp, ms, gain, w_in_b, qg, kg):
    d, n_pt, n_t, tps, b = cfg["d"], cfg["n_pt"], cfg["n_t"], cfg["tps"], cfg["b"]
    np_, ns = cfg["np"], cfg["ns"]
    pidx = lambda i: jnp.minimum(i, n_pt - 1)
    sidx = lambda i: jnp.maximum(i - n_pt, 0)

    kv_specs, kv_shapes = [], []
    for keep, nk in cfg["kv_keep"]:
        rows = min(keep, TM)

        def kv_idx(i, nk=nk):
            ip = pidx(i)
            return ((ip // tps) * nk + jnp.maximum(ip % tps - (tps - nk), 0), 0)

        kv_specs.append(pl.BlockSpec((rows, 2 * GROUP_DIM), kv_idx))
        kv_shapes.append(jax.ShapeDtypeStruct((b * keep, 2 * GROUP_DIM), F32))

    return pl.pallas_call(
        functools.partial(_inproj_kernel, cfg),
        grid=(n_t,),
        in_specs=[pl.BlockSpec((GP, SUBLANES, d), lambda i: (i, 0, 0))]
        + _mod_specs(cfg, 0) + _mod_specs(cfg, 1)
        + [pl.BlockSpec((1, d), lambda i: (0, 0)),
           pl.BlockSpec(w_in_b.shape, lambda i: (0, 0)),
           pl.BlockSpec((1, GROUP_DIM), lambda i: (0, 0)),
           pl.BlockSpec((1, GROUP_DIM), lambda i: (0, 0))],
        out_specs=[pl.BlockSpec((TM, POOL_DIM), lambda i: (pidx(i), 0)),
                   pl.BlockSpec((TM, POOL_DIM), lambda i: (sidx(i), 0)),
                   pl.BlockSpec((N_GROUPS, TM, GROUP_DIM), lambda i: (0, pidx(i), 0)),
                   pl.BlockSpec((N_GROUPS, TM, GROUP_DIM), lambda i: (0, pidx(i), 0)),
                   pl.BlockSpec((N_GROUPS, TM, GROUP_DIM), lambda i: (0, pidx(i), 0)),
                   pl.BlockSpec((N_GROUPS, TM, GROUP_DIM), lambda i: (0, sidx(i), 0)),
                   pl.BlockSpec((N_GROUPS, TM, 2 * GROUP_DIM), lambda i: (0, sidx(i), 0))] + kv_specs,
        out_shape=[jax.ShapeDtypeStruct((np_, POOL_DIM), F32),
                   jax.ShapeDtypeStruct((ns, POOL_DIM), F32),
                   jax.ShapeDtypeStruct((N_GROUPS, np_, GROUP_DIM), BF16),
                   jax.ShapeDtypeStruct((N_GROUPS, np_, GROUP_DIM), BF16),
                   jax.ShapeDtypeStruct((N_GROUPS, np_, GROUP_DIM), BF16),
                   jax.ShapeDtypeStruct((N_GROUPS, ns, GROUP_DIM), F32),
                   jax.ShapeDtypeStruct((N_GROUPS, ns, 2 * GROUP_DIM), F32)] + kv_shapes,
        compiler_params=_cparams(1, VMEM_LIMIT),
        name="inproj",
    )(x3, mp, ms, mp, ms, gain, w_in_b, qg, kg)


def _rel_bucket_np(dist):
    dist = np.asarray(dist, np.int32)
    max_exact = REL_BUCKETS // 2
    d_f = np.maximum(dist, 1).astype(np.float32)
    ratio = np.log(d_f / np.float32(max_exact)) / np.float32(math.log(REL_MAX_DIST / max_exact))
    large = max_exact + (ratio * np.float32(REL_BUCKETS - max_exact)).astype(np.int32)
    large = np.minimum(large, REL_BUCKETS - 1)
    return np.where(dist < max_exact, dist, large)


def _prompt_bias_tables(rel_bias):
    qi = np.arange(BAND)[:, None]
    kj = np.arange(2 * BAND)[None, :]
    delta = qi + BAND - kj
    ok = (delta >= 0) & (delta <= BAND)
    ok_first = ok & (kj >= BAND)
    tables = []
    for g, (_, dil) in enumerate(ATT_GROUPS):
        bucket = _rel_bucket_np(np.maximum(delta, 0) * dil)
        vals = rel_bias[:, g * HEADS:(g + 1) * HEADS][bucket]
        vals = jnp.transpose(vals, (2, 0, 1)).astype(F32)
        both = jnp.stack([jnp.where(ok[None], vals, NEG_INF), jnp.where(ok_first[None], vals, NEG_INF)])
        tables.append(both.reshape(2, HEADS * BAND, 2 * BAND))
    return tables


def _sample_layout(g, t_new):
    win, dil = ATT_GROUPS[g]
    lb = min(win, PAST_LEN)
    rkeep = min(dil, SUBLANES)
    if dil > SUBLANES:
        m = np.arange(lb // dil)[:, None]
        r = np.arange(rkeep)[None, :]
        cache_idx = (m * dil + r).reshape(-1)
    else:
        cache_idx = np.arange(lb)
    cat_idx = np.concatenate([cache_idx, lb + np.arange(t_new)])
    return lb, cache_idx.shape[0], cat_idx


def _sample_bias_tables(rel_bias, t_new):
    tables = []
    for g, (win, dil) in enumerate(ATT_GROUPS):
        lb, n_cache, cat_idx = _sample_layout(g, t_new)
        rpad = n_cache + LANES
        t = np.arange(t_new)[:, None]
        delta = lb + t - cat_idx[None, :]
        ok = (delta >= 0) & (delta % dil == 0) & (delta // dil <= win // dil)
        bucket = _rel_bucket_np(np.maximum(delta, 0))
        vals = rel_bias[:, g * HEADS:(g + 1) * HEADS][bucket]
        vals = jnp.transpose(vals, (2, 0, 1)).astype(F32)
        vals = jnp.where(ok[None], vals, NEG_INF).reshape(HEADS * t_new, -1)
        tables.append(jnp.pad(vals, ((0, 0), (0, rpad - vals.shape[1])), constant_values=NEG_INF))
    return tables


def _stack_heads(q):
    lh = _lane_head()
    return jnp.concatenate([jnp.where(lh == h, q, jnp.zeros_like(q)) for h in range(HEADS)], axis=0)


def _softmax_pv(s, v):
    m = jnp.max(s, axis=1, keepdims=True)
    p = jnp.exp(s - m)
    l = jnp.sum(p, axis=1, keepdims=True)
    o = jnp.dot(p.astype(BF16), v, preferred_element_type=F32) / l
    return o, m + jnp.log(l)


def _unstack_heads(o4, lse4, rows):
    lh = _lane_head()
    o = o4[0:rows]
    lse = jnp.broadcast_to(lse4[0:rows], (rows, GROUP_DIM))
    for h in range(1, HEADS):
        o = jnp.where(lh == h, o4[h * rows:(h + 1) * rows], o)
        lse = jnp.where(lh == h, lse4[h * rows:(h + 1) * rows], lse)
    return o, lse


def _attn_prompt_kernel(nsub, q_ref, kc_ref, kp_ref, vc_ref, vp_ref, bias_ref, o_ref, lse_ref, kbuf, vbuf):
    i = pl.program_id(2)
    kbuf[0:BAND] = kp_ref[...]
    kbuf[BAND:] = kc_ref[...]
    vbuf[0:BAND] = vp_ref[...]
    vbuf[BAND:] = vc_ref[...]
    for s in range(nsub):
        q4 = _stack_heads(q_ref[s * BAND:(s + 1) * BAND])
        kc = kbuf[s * BAND:(s + 2) * BAND]
        vc = vbuf[s * BAND:(s + 2) * BAND]
        logits = lax.dot_general(q4, kc, (((1,), (1,)), ((), ())), preferred_element_type=F32)
        if s == 0:
            bias = bias_ref[jnp.where(i == 0, 1, 0)]
        else:
            bias = bias_ref[0]
        o4, lse4 = _softmax_pv(logits + bias, vc)
        o, lse = _unstack_heads(o4, lse4, BAND)
        o_ref[s * BAND:(s + 1) * BAND] = o
        lse_ref[s * BAND:(s + 1) * BAND] = lse


def _attn_prompt(cfg, g, qp, kp, vp, bias):
    b, s_len = cfg["b"], cfg["s"]
    _, dil = ATT_GROUPS[g]
    l_len = s_len // dil
    tq = min(TM, l_len)
    nsub = tq // BAND
    nq = l_len // tq
    shape4 = (N_GROUPS, b, l_len, dil * GROUP_DIM)
    q4, k4, v4 = qp.reshape(shape4), kp.reshape(shape4), vp.reshape(shape4)
    cur = pl.BlockSpec((None, None, tq, GROUP_DIM), lambda bi, r, i: (g, bi, i, r))
    prev = pl.BlockSpec((None, None, BAND, GROUP_DIM),
                        lambda bi, r, i: (g, bi, jnp.maximum(i * nsub - 1, 0), r))
    out = pl.BlockSpec((None, tq, GROUP_DIM), lambda bi, r, i: (bi, i, r))
    o, lse = pl.pallas_call(
        functools.partial(_attn_prompt_kernel, nsub),
        grid=(b, dil, nq),
        in_specs=[cur, cur, prev, cur, prev, pl.BlockSpec(bias.shape, lambda bi, r, i: (0, 0, 0))],
        out_specs=[out, out],
        out_shape=[jax.ShapeDtypeStruct((b, l_len, dil * GROUP_DIM), F32)] * 2,
        scratch_shapes=[pltpu.VMEM((tq + BAND, GROUP_DIM), BF16)] * 2,
        compiler_params=_cparams(3, VMEM_LIMIT),
        name=f"attn_prompt_g{g}",
    )(q4, k4, k4, v4, v4, bias)
    return o.reshape(b * s_len, GROUP_DIM), lse.reshape(b * s_len, GROUP_DIM)


def _attn_sample_kernel(bs, n_cache, t_new, q_ref, kvn_ref, cache_ref, bias_ref, o_ref, lse_ref, kvbuf):
    rpad = n_cache + LANES
    kvbuf[n_cache + t_new:rpad] = jnp.zeros((rpad - n_cache - t_new, 2 * GROUP_DIM), BF16)
    for j in range(bs):
        kvbuf[0:n_cache] = cache_ref[j].reshape(n_cache, 2 * GROUP_DIM).astype(BF16)
        kvbuf[n_cache:n_cache + t_new] = kvn_ref[j].astype(BF16)
        q4 = _stack_heads(q_ref[j].astype(BF16))
        logits = lax.dot_general(q4, kvbuf[:, :GROUP_DIM], (((1,), (1,)), ((), ())),
                                 preferred_element_type=F32)
        o4, lse4 = _softmax_pv(logits + bias_ref[...], kvbuf[:, GROUP_DIM:])
        o, lse = _unstack_heads(o4, lse4, t_new)
        o_ref[j] = o
        lse_ref[j] = lse


def _attn_sample(cfg, g, layer, qs, kvs, cache, bias):
    bd, t_new = cfg["bd"], cfg["t"]
    _, dil = ATT_GROUPS[g]
    lb, n_cache, _ = _sample_layout(g, t_new)
    depth = cache.shape[0]
    bs = 4
    if dil > SUBLANES:
        cache_v = cache.reshape(depth, bd, lb // dil, dil, 2 * GROUP_DIM)
        cache_spec = pl.BlockSpec((None, bs, lb // dil, SUBLANES, 2 * GROUP_DIM),
                                  lambda i: (layer, i, 0, 0, 0))
    else:
        cache_v = cache.reshape(depth, bd, lb, 2 * GROUP_DIM)
        cache_spec = pl.BlockSpec((None, bs, lb, 2 * GROUP_DIM), lambda i: (layer, i, 0, 0))
    out = pl.BlockSpec((bs, t_new, GROUP_DIM), lambda i: (i, 0, 0))
    o, lse = pl.pallas_call(
        functools.partial(_attn_sample_kernel, bs, n_cache, t_new),
        grid=(bd // bs,),
        in_specs=[pl.BlockSpec((None, bs, t_new, GROUP_DIM), lambda i: (g, i, 0, 0)),
                  pl.BlockSpec((None, bs, t_new, 2 * GROUP_DIM), lambda i: (g, i, 0, 0)),
                  cache_spec,
                  pl.BlockSpec(bias.shape, lambda i: (0, 0))],
        out_specs=[out, out],
        out_shape=[jax.ShapeDtypeStruct((bd, t_new, GROUP_DIM), F32)] * 2,
        scratch_shapes=[pltpu.VMEM((n_cache + LANES, 2 * GROUP_DIM), BF16)],
        compiler_params=_cparams(1, VMEM_LIMIT),
        name=f"attn_sample_g{g}",
    )(qs.reshape(N_GROUPS, bd, t_new, GROUP_DIM), kvs.reshape(N_GROUPS, bd, t_new, 2 * GROUP_DIM),
      cache_v, bias)
    return o.reshape(bd * t_new, GROUP_DIM), lse.reshape(bd * t_new, GROUP_DIM)


def _pool_mix(ext_ref, rows, pos, w_ref, scale_ref, o_ref):
    lo = POOL_PAD
    u = ext_ref[:, lo:lo + rows, :]
    acc = u
    sums = {}
    for j in range(1, max(POOL_WINDOWS)):
        acc = acc + ext_ref[:, lo - j:lo - j + rows, :]
        if j + 1 in POOL_WINDOWS:
            sums[j + 1] = acc
    lane_grp = lax.broadcasted_iota(jnp.int32, (1, 1, POOL_DIM), 2) // (POOL_DIM // len(POOL_WINDOWS))
    z = None
    for gi, w in enumerate(POOL_WINDOWS):
        cnt = jnp.minimum(w, pos + 1).astype(F32)
        zw = sums[w] / cnt - u
        z = zw if z is None else jnp.where(lane_grp == gi, zw, z)
    nb = u.shape[0]
    y = jnp.dot(z.reshape(nb * rows, POOL_DIM).astype(BF16), w_ref[...], preferred_element_type=F32)
    o_ref[...] = (y * scale_ref[...]).astype(BF16)


def _pool_prompt_kernel(u_ref, halo_ref, w_ref, scale_ref, o_ref, ext_ref):
    i = pl.program_id(1)
    halo = halo_ref[...]
    ext_ref[:, 0:POOL_PAD, :] = jnp.where(i == 0, jnp.zeros_like(halo), halo)
    ext_ref[:, POOL_PAD:, :] = u_ref[...]
    pos = i * TM + lax.broadcasted_iota(jnp.int32, (1, TM, 1), 1)
    _pool_mix(ext_ref, TM, pos, w_ref, scale_ref, o_ref)


def _pool_sample_kernel(t_new, u_ref, hist_ref, w_ref, scale_ref, o_ref, ext_ref):
    nb = u_ref.shape[0]
    ext_ref[:, 0:POOL_PAD - POOL_HIST, :] = jnp.zeros((nb, POOL_PAD - POOL_HIST, POOL_DIM), F32)
    ext_ref[:, POOL_PAD - POOL_HIST:POOL_PAD, :] = hist_ref[...]
    ext_ref[:, POOL_PAD:, :] = u_ref[...]
    pos = PAST_LEN + lax.broadcasted_iota(jnp.int32, (1, t_new, 1), 1)
    _pool_mix(ext_ref, t_new, pos, w_ref, scale_ref, o_ref)


def _pool_prompt(cfg, u_p, w_bd, scale):
    b, s_len = cfg["b"], cfg["s"]
    u3 = u_p.reshape(b, s_len, POOL_DIM)
    per = TM // POOL_PAD
    return pl.pallas_call(
        _pool_prompt_kernel,
        grid=(b, s_len // TM),
        in_specs=[pl.BlockSpec((1, TM, POOL_DIM), lambda bi, i: (bi, i, 0)),
                  pl.BlockSpec((1, POOL_PAD, POOL_DIM), lambda bi, i: (bi, jnp.maximum(i * per - 1, 0), 0)),
                  pl.BlockSpec((POOL_DIM, POOL_DIM), lambda bi, i: (0, 0)),
                  pl.BlockSpec((1, POOL_DIM), lambda bi, i: (0, 0))],
        out_specs=pl.BlockSpec((TM, POOL_DIM), lambda bi, i: (bi * (s_len // TM) + i, 0)),
        out_shape=jax.ShapeDtypeStruct((b * s_len, POOL_DIM), BF16),
        scratch_shapes=[pltpu.VMEM((1, TM + POOL_PAD, POOL_DIM), F32)],
        compiler_params=_cparams(2, VMEM_LIMIT),
        name="pool_prompt",
    )(u3, u3, w_bd, scale)


def _pool_sample(cfg, layer, u_s, state_pool, w_bd, scale):
    bd, t_new = cfg["bd"], cfg["t"]
    u3 = u_s.reshape(bd, t_new, POOL_DIM)
    nb = GP
    return pl.pallas_call(
        functools.partial(_pool_sample_kernel, t_new),
        grid=(bd // nb,),
        in_specs=[pl.BlockSpec((nb, t_new, POOL_DIM), lambda i: (i, 0, 0)),
                  pl.BlockSpec((None, nb, POOL_HIST, POOL_DIM), lambda i: (layer, i, 0, 0)),
                  pl.BlockSpec((POOL_DIM, POOL_DIM), lambda i: (0, 0)),
                  pl.BlockSpec((1, POOL_DIM), lambda i: (0, 0))],
        out_specs=pl.BlockSpec((nb * t_new, POOL_DIM), lambda i: (i, 0)),
        out_shape=jax.ShapeDtypeStruct((bd * t_new, POOL_DIM), BF16),
        scratch_shapes=[pltpu.VMEM((nb, t_new + POOL_PAD, POOL_DIM), F32)],
        compiler_params=_cparams(1, VMEM_LIMIT),
        name="pool_sample",
    )(u3, state_pool, w_bd, scale)


def _outproj_kernel(cfg, x_ref, g1p_ref, g1s_ref, shp_ref, shs_ref, scp_ref, scs_ref, pp_ref, ps_ref, *rest):
    att_refs = rest[:4 * N_GROUPS]
    gain_ref, wo_ref, wr_ref, br_ref, x1_ref, h2_ref, route_ref, cnt_ref, run_ref = rest[4 * N_GROUPS:]
    i = pl.program_id(0)
    n_pt, d = cfg["n_pt"], cfg["d"]
    is_p = i < n_pt

    o, lse = [], []
    for g in range(N_GROUPS):
        op_ref, lp_ref, os_ref, ls_ref = att_refs[4 * g:4 * g + 4]
        o.append(jnp.where(is_p, op_ref[...], os_ref[...]))
        lse.append(jnp.where(is_p, lp_ref[...], ls_ref[...]))
    mx = functools.reduce(jnp.maximum, lse)
    e = [jnp.exp(l - mx) for l in lse]
    attn = sum(eg * og for eg, og in zip(e, o)) / sum(e)
    pool = jnp.where(is_p, pp_ref[...], ps_ref[...])
    cat = jnp.concatenate([pool, attn.astype(BF16)], axis=1)
    y = jnp.dot(cat, wo_ref[...], preferred_element_type=F32)

    g1 = jnp.where(is_p, g1p_ref[...], g1s_ref[...])
    x1 = x_ref[...] + g1 * y.reshape(GP, SUBLANES, d)
    x1_ref[...] = x1
    sh = jnp.where(is_p, shp_ref[...], shs_ref[...])
    sc = jnp.where(is_p, scp_ref[...], scs_ref[...])
    h2 = _rmsnorm_mod(x1, gain_ref[...], sc, sh).reshape(TM, d)
    h2_ref[...] = h2

    h_hi = h2.astype(BF16)
    h_lo = (h2 - h_hi.astype(F32)).astype(BF16)
    wr = wr_ref[...]
    w_hi = wr.astype(BF16)
    w_lo = (wr - w_hi.astype(F32)).astype(BF16)
    logits = (jnp.dot(h_hi, w_hi, preferred_element_type=F32)
              + jnp.dot(h_lo, w_hi, preferred_element_type=F32)
              + jnp.dot(h_hi, w_lo, preferred_element_type=F32)) + br_ref[...]

    lane = lax.broadcasted_iota(jnp.int32, (TM, LANES), 1).astype(F32)
    vals = logits
    top_v, top_i = [], []
    onehot = jnp.zeros((TM, LANES), F32)
    for _ in range(TOP_K):
        m = jnp.max(vals, axis=1, keepdims=True)
        idx = jnp.min(jnp.where(vals == m, lane, float(LANES)), axis=1, keepdims=True)
        hit = lane == idx
        vals = jnp.where(hit, -jnp.inf, vals)
        onehot = jnp.where(hit, 1.0, onehot)
        top_v.append(m)
        top_i.append(idx)
    ev = [jnp.exp(v - top_v[0]) for v in top_v]
    den = sum(ev)

    @pl.when(i == 0)
    def _():
        run_ref[...] = jnp.zeros_like(run_ref)

    row = lax.broadcasted_iota(jnp.int32, (TM, TM), 0)
    col = lax.broadcasted_iota(jnp.int32, (TM, TM), 1)
    before = (col < row).astype(BF16)
    rank_all = jnp.dot(before, onehot.astype(BF16), preferred_element_type=F32) + run_ref[...]
    route = jnp.zeros((TM, LANES), F32)
    for k in range(TOP_K):
        rank_k = jnp.sum(jnp.where(lane == top_i[k], rank_all, 0.0), axis=1, keepdims=True)
        route = jnp.where(lane == float(k), top_i[k], route)
        route = jnp.where(lane == float(TOP_K + k), ev[k] / den, route)
        route = jnp.where(lane == float(2 * TOP_K + k), rank_k, route)
    route_ref[...] = route
    run_ref[...] = run_ref[...] + jnp.sum(onehot, axis=0, keepdims=True)
    cnt_ref[...] = run_ref[...]


def _outproj(cfg, x3, mp, ms, pool_p, pool_s, att, gain, w_out_b, w_r, b_r):
    d, n_pt, n_t = cfg["d"], cfg["n_pt"], cfg["n_t"]
    n = cfg["n"]
    pidx = lambda i: (jnp.minimum(i, n_pt - 1), 0)
    sidx = lambda i: (jnp.maximum(i - n_pt, 0), 0)
    pspec = pl.BlockSpec((TM, GROUP_DIM), pidx)
    sspec = pl.BlockSpec((TM, GROUP_DIM), sidx)
    att_specs, att_args = [], []
    for (o_p, lse_p, o_s, lse_s) in att:
        att_specs += [pspec, pspec, sspec, sspec]
        att_args += [o_p, lse_p, o_s, lse_s]
    full = lambda a: pl.BlockSpec(a.shape, lambda i: (0,) * a.ndim)
    return pl.pallas_call(
        functools.partial(_outproj_kernel, cfg),
        grid=(n_t,),
        in_specs=[pl.BlockSpec((GP, SUBLANES, d), lambda i: (i, 0, 0))]
        + _mod_specs(cfg, 2) + _mod_specs(cfg, 3) + _mod_specs(cfg, 4)
        + [pspec, sspec] + att_specs + [full(gain), full(w_out_b), full(w_r), full(b_r)],
        out_specs=[pl.BlockSpec((GP, SUBLANES, d), lambda i: (i, 0, 0)),
                   pl.BlockSpec((TM, d), lambda i: (i, 0)),
                   pl.BlockSpec((TM, LANES), lambda i: (i, 0)),
                   pl.BlockSpec((1, LANES), lambda i: (0, 0))],
        out_shape=[jax.ShapeDtypeStruct(x3.shape, F32),
                   jax.ShapeDtypeStruct((n, d), F32),
                   jax.ShapeDtypeStruct((n, LANES), F32),
                   jax.ShapeDtypeStruct((1, LANES), F32)],
        scratch_shapes=[pltpu.VMEM((1, LANES), F32)],
        compiler_params=_cparams(1, VMEM_LIMIT),
        name="outproj_router",
    )(x3, mp, ms, mp, ms, mp, ms, pool_p, pool_s, *att_args, gain, w_out_b, w_r, b_r)


def _moe_kernel(te_ref, nu_ref, src_ref, nxt_ref, dst_ref, h_hbm, w1_ref, b1_ref, w2_ref, b2_ref, y_hbm,
                xbuf, ybuf, w1b, w2b, gsem, ssem):
    i = pl.program_id(0)
    n_used = nu_ref[0]
    slot = i % 2
    ff = w2_ref.shape[0]

    def gather_copy(row, s, tok):
        return pltpu.make_async_copy(h_hbm.at[pl.ds(tok, 1)], xbuf.at[s, pl.ds(row, 1)], gsem.at[s])

    def scatter_copy(row, s, dst):
        return pltpu.make_async_copy(ybuf.at[s, pl.ds(row, 1)], y_hbm.at[pl.ds(dst, 1)], ssem.at[s])

    def start_gather(idx_ref, s):
        for r in range(MOE_TM):
            gather_copy(r, s, idx_ref[0, r]).start()

    def wait_gather(s):
        pltpu.make_async_copy(h_hbm.at[pl.ds(0, MOE_TM)], xbuf.at[s], gsem.at[s]).wait()

    def wait_scatter(s):
        pltpu.make_async_copy(ybuf.at[s], y_hbm.at[pl.ds(0, MOE_TM)], ssem.at[s]).wait()

    @pl.when(i < n_used)
    def _():
        @pl.when(i == 0)
        def _():
            start_gather(src_ref, 0)
            n_asg = y_hbm.shape[0] - 2 * MOE_TM
            ybuf[...] = jnp.zeros_like(ybuf)
            for s in range(2):
                fill = pltpu.make_async_copy(ybuf.at[s], y_hbm.at[pl.ds(n_asg + s * MOE_TM, MOE_TM)],
                                             ssem.at[s])
                fill.start()
                fill.wait()

        wait_gather(slot)

        @pl.when(i + 1 < n_used)
        def _():
            start_gather(nxt_ref, 1 - slot)

        @pl.when(jnp.logical_or(i == 0, te_ref[i] != te_ref[jnp.maximum(i - 1, 0)]))
        def _():
            w1b[...] = w1_ref[...].astype(BF16)
            w2b[...] = w2_ref[...].astype(BF16)

        x = xbuf[slot].astype(BF16)
        h1 = jnp.dot(x, w1b[...], preferred_element_type=F32) + b1_ref[...]
        gate = jnp.minimum(h1[:, :ff], SWIGLU_LIMIT)
        up = jnp.clip(h1[:, ff:], -SWIGLU_LIMIT, SWIGLU_LIMIT)
        act = gate * jax.nn.sigmoid(SWIGLU_ALPHA * gate) * (up + 1.0)
        ybuf[slot] = jnp.dot(act.astype(BF16), w2b[...], preferred_element_type=F32) + b2_ref[...]

        for r in range(MOE_TM):
            scatter_copy(r, slot, dst_ref[0, r]).start()

        @pl.when(i >= 1)
        def _():
            wait_scatter(1 - slot)

        @pl.when(i == n_used - 1)
        def _():
            wait_scatter(slot)


def _moe(layer, n_tiles, tile_expert, n_used, slot_src, slot_dst, h2, w1, b1, w2, b2, n_rows_out):
    depth, n_exp, d, ff2 = w1.shape
    smem_tile = lambda f: pl.BlockSpec((None, 1, MOE_TM), f, memory_space=pltpu.SMEM)
    grid_spec = pltpu.PrefetchScalarGridSpec(
        num_scalar_prefetch=2,
        grid=(n_tiles,),
        in_specs=[smem_tile(lambda i, te, nu: (i, 0, 0)),
                  smem_tile(lambda i, te, nu: (jnp.minimum(i + 1, n_tiles - 1), 0, 0)),
                  smem_tile(lambda i, te, nu: (i, 0, 0)),
                  pl.BlockSpec(memory_space=pl.ANY),
                  pl.BlockSpec((None, None, d, ff2), lambda i, te, nu: (layer, te[i], 0, 0)),
                  pl.BlockSpec((None, None, 1, ff2), lambda i, te, nu: (layer, te[i], 0, 0)),
                  pl.BlockSpec((None, None, w2.shape[2], d), lambda i, te, nu: (layer, te[i], 0, 0)),
                  pl.BlockSpec((None, None, 1, d), lambda i, te, nu: (layer, te[i], 0, 0))],
        out_specs=pl.BlockSpec(memory_space=pl.ANY),
        scratch_shapes=[pltpu.VMEM((2, MOE_TM, d), F32),
                        pltpu.VMEM((2, MOE_TM, d), F32),
                        pltpu.VMEM((d, ff2), BF16),
                        pltpu.VMEM((w2.shape[2], d), BF16),
                        pltpu.SemaphoreType.DMA((2,)),
                        pltpu.SemaphoreType.DMA((2,))],
    )
    src3 =slot_src.reshape(n_tiles, 1, MOE_TM)
    return pl.pallas_call(
        _moe_kernel,
        grid_spec=grid_spec,
        out_shape=jax.ShapeDtypeStruct((n_rows_out, d), F32),
        compiler_params=_cparams(1, VMEM_LIMIT),
        name="moe_experts",
    )(tile_expert, n_used, src3, src3, slot_dst.reshape(n_tiles, 1, MOE_TM), h2,
      w1, b1.reshape(depth, n_exp, 1, ff2), w2, b2.reshape(depth, n_exp, 1, d))


def _moe_plan(cfg, route, counts):
    n = cfg["n"]
    m = n * TOP_K
    n_tiles = m // MOE_TM + N_EXPERTS
    m_pad = n_tiles * MOE_TM
    top_i = route[:, 0:TOP_K].astype(jnp.int32)
    rank = route[:, 2 * TOP_K:3 * TOP_K].astype(jnp.int32)
    cnt = counts[0, :N_EXPERTS].astype(jnp.int32)
    tiles_e = (cnt + MOE_TM - 1) // MOE_TM
    tile_end = jnp.cumsum(tiles_e)
    pstart = (tile_end - tiles_e) * MOE_TM
    n_used = tile_end[-1]
    pos = pstart[top_i] + rank
    t = jnp.arange(n_tiles, dtype=jnp.int32)
    te = jnp.minimum(jnp.searchsorted(tile_end, t, side="right"), N_EXPERTS - 1).astype(jnp.int32)
    te = jnp.where(t < n_used, te, te[jnp.maximum(n_used - 1, 0)])
    s = jnp.arange(m_pad, dtype=jnp.int32)
    dump = m + ((s // MOE_TM) % 2) * MOE_TM + s % MOE_TM
    slot_dst = dump.at[pos.T.reshape(-1)].set(jnp.arange(m, dtype=jnp.int32))
    slot_src = jnp.where(slot_dst < m, slot_dst % n, 0)
    return n_tiles, te, n_used.reshape(1).astype(jnp.int32), slot_src, slot_dst, m + 2 * MOE_TM


def _combine_kernel(cfg, x_ref, g2p_ref, g2s_ref, route_ref, *rest):
    y_refs, o_ref = rest[:TOP_K], rest[TOP_K]
    i = pl.program_id(0)
    is_p = i < cfg["n_pt"]
    route = route_ref[...]
    acc = route[:, TOP_K:TOP_K + 1] * y_refs[0][...]
    for k in range(1, TOP_K):
        acc = acc + route[:, TOP_K + k:TOP_K + k + 1] * y_refs[k][...]
    g2 = jnp.where(is_p, g2p_ref[...], g2s_ref[...])
    o_ref[...] = x_ref[...] + g2 * acc.reshape(GP, SUBLANES, cfg["d"])


def _combine(cfg, x1, mp, ms, route, ybuf):
    d, n_t = cfg["d"], cfg["n_t"]
    y_specs = [pl.BlockSpec((TM, d), lambda i, k=k: (k * n_t + i, 0)) for k in range(TOP_K)]
    return pl.pallas_call(
        functools.partial(_combine_kernel, cfg),
        grid=(n_t,),
        in_specs=[pl.BlockSpec((GP, SUBLANES, d), lambda i: (i, 0, 0))] + _mod_specs(cfg, 5)
        + [pl.BlockSpec((TM, LANES), lambda i: (i, 0))] + y_specs,
        out_specs=pl.BlockSpec((GP, SUBLANES, d), lambda i: (i, 0, 0)),
        out_shape=jax.ShapeDtypeStruct(x1.shape, F32),
        compiler_params=_cparams(1, VMEM_LIMIT),
        name="moe_combine",
    )(x1, mp, ms, route, *([ybuf] * TOP_K))


def kernel(x_prompt, x_sample, cache_kv_w128_d1, cache_kv_w512_d4, cache_kv_w2048_d16, state_pool,
           c_prompt, c_sample, rel_bias, norm_mix, norm_ffn, w_ada, b_ada, w_in, q_norm, k_norm,
           pool_w, pool_scale, w_out, w_router, b_router, w_expert_in, b_expert_in,
           w_expert_out, b_expert_out):
    b, s_len, d = x_prompt.shape
    bd, t_new, _ = x_sample.shape
    depth = w_in.shape[0]
    np_, ns = b * s_len, bd * t_new
    assert t_new == SUBLANES and s_len % TM == 0 and ns % TM == 0
    assert all((s_len // dil) % BAND == 0 for _, dil in ATT_GROUPS)
    cfg = dict(b=b, s=s_len, d=d, bd=bd, t=t_new, np=np_, ns=ns, n=np_ + ns,
               n_pt=np_ // TM, n_t=(np_ + ns) // TM, tps=s_len // TM)
    kv_keep = []
    for win, _ in ATT_GROUPS:
        keep = min(win, s_len)
        assert keep % TM == 0 or TM % keep == 0
        kv_keep.append((keep, max(keep // TM, 1)))
    cfg["kv_keep"] = tuple(kv_keep)
    caches = (cache_kv_w128_d1, cache_kv_w512_d4, cache_kv_w2048_d16)

    mods = _ada_mods(jnp.concatenate([c_prompt, c_sample], axis=0), w_ada, b_ada)
    bias_p = _prompt_bias_tables(rel_bias)
    bias_s = _sample_bias_tables(rel_bias, t_new)
    w_in_b = w_in.astype(BF16)
    w_out_b = w_out.astype(BF16)
    eye = jnp.eye(len(POOL_WINDOWS), dtype=F32)
    w_pool_bd = (eye[None, :, None, :, None] * pool_w[:, :, :, None, :]).reshape(depth, POOL_DIM, POOL_DIM)
    w_pool_bd = w_pool_bd.astype(BF16)
    w_r_pad = jnp.pad(w_router, ((0, 0), (0, 0), (0, LANES - N_EXPERTS)))
    b_r_pad = jnp.pad(b_router, ((0, 0), (0, LANES - N_EXPERTS)), constant_values=NEG_INF)

    x3 = jnp.concatenate([x_prompt.reshape(np_, d), x_sample.reshape(ns, d)], axis=0)
    x3 = x3.reshape((np_ + ns) // SUBLANES, SUBLANES, d)

    kv_p = [[] for _ in ATT_GROUPS]
    kv_s = [[] for _ in ATT_GROUPS]
    pool_p_state, pool_s_state = [], []
    for l in range(depth):
        mp = mods[l, :b].reshape(b, 1, 6 * d)
        ms = mods[l, b:].reshape(bd, 1, 6 * d)
        qg = jnp.tile(q_norm[l], HEADS).reshape(1, GROUP_DIM)
        kg = jnp.tile(k_norm[l], HEADS).reshape(1, GROUP_DIM)
        (u_p, u_s, qp, kp, vp, qs, kvs, *kvp) = _inproj(
            cfg, x3, mp, ms, norm_mix[l].reshape(1, d), w_in_b[l], qg, kg)

        att = []
        for g in range(N_GROUPS):
            o_p, lse_p = _attn_prompt(cfg, g, qp, kp, vp, bias_p[g])
            o_s, lse_s = _attn_sample(cfg, g, l, qs, kvs, caches[g], bias_s[g])
            att.append((o_p, lse_p, o_s, lse_s))
            keep = cfg["kv_keep"][g][0]
            kv_p[g].append(kvp[g].reshape(b, keep, 2, HEADS, HEAD_DIM))
            kv_s[g].append(kvs[g].reshape(bd, t_new, 2, HEADS, HEAD_DIM))

        scale = pool_scale[l].reshape(1, POOL_DIM)
        pool_p = _pool_prompt(cfg, u_p, w_pool_bd[l], scale)
        pool_s = _pool_sample(cfg, l, u_s, state_pool, w_pool_bd[l], scale)
        pool_p_state.append(u_p.reshape(b, s_len, POOL_DIM)[:, s_len - POOL_HIST:])
        pool_s_state.append(jnp.concatenate(
            [state_pool[l], u_s.reshape(bd, t_new, POOL_DIM)], axis=1)[:, -POOL_HIST:])

        x1, h2, route, counts = _outproj(cfg, x3, mp, ms, pool_p, pool_s, att,
                                         norm_ffn[l].reshape(1, d), w_out_b[l], w_r_pad[l],
                                         b_r_pad[l].reshape(1, LANES))
        n_tiles, te, n_used, slot_src, slot_dst, n_rows_out = _moe_plan(cfg, route, counts)
        ybuf = _moe(l, n_tiles, te, n_used, slot_src, slot_dst, h2,
                    w_expert_in, b_expert_in, w_expert_out, b_expert_out, n_rows_out)
        x3 = _combine(cfg, x1, mp, ms, route, ybuf)

    x_all = x3.reshape(np_ + ns, d)
    y_prompt = x_all[:np_].reshape(b, s_len, d)
    y_sample = x_all[np_:].reshape(bd, t_new, d)
    return (y_prompt, y_sample,
            jnp.stack(kv_p[0]), jnp.stack(kv_p[1]), jnp.stack(kv_p[2]), jnp.stack(pool_p_state),
            jnp.stack(kv_s[0]), jnp.stack(kv_s[1]), jnp.stack(kv_s[2]), jnp.stack(pool_s_state))
```

```python
import functools
import math

import numpy as np
import jax
import jax.numpy as jnp
from jax import lax
from jax.experimental import pallas as pl
from jax.experimental.pallas import tpu as pltpu

F32 = jnp.float32
BF16 = jnp.bfloat16

HEAD_DIM = 64
HEADS = 4
GROUP_DIM = HEADS * HEAD_DIM
ATT_GROUPS = ((128, 1), (512, 4), (2048, 16))
N_GROUPS = len(ATT_GROUPS)
BAND = 128
POOL_WINDOWS = (2, 4, 8, 16)
POOL_HIST = 15
POOL_DIM = 256
POOL_PAD = 16
REL_BUCKETS = 32
REL_MAX_DIST = 2048
N_EXPERTS = 32
TOP_K = 4
SWIGLU_LIMIT = 7.0
SWIGLU_ALPHA = 1.702
EPS = 1e-6
NEG_INF = -1e30
PAST_LEN = 2048

SUBLANES = 8
LANES = 128
TM = 512
GP = TM // SUBLANES
MOE_TM = 256
VMEM_LIMIT = 52 * 1024 * 1024


def _cparams(n_axes, vmem=None):
    return pltpu.CompilerParams(dimension_semantics=("arbitrary",) * n_axes,
                                vmem_limit_bytes=vmem)


def _lane_head(width=GROUP_DIM):
    return lax.broadcasted_iota(jnp.int32, (1, width), 1) // HEAD_DIM


def _ada_kernel(c_ref, w_ref, b_ref, o_ref):
    c = c_ref[...]
    a = (c * jax.nn.sigmoid(c)).astype(BF16)
    o_ref[...] = jnp.dot(a, w_ref[...].astype(BF16), preferred_element_type=F32) + b_ref[...]


def _ada_mods(c_all, w_ada, b_ada):
    depth, d, d6 = w_ada.shape
    bc = c_all.shape[0]
    tn = d6 // 4
    return pl.pallas_call(
        _ada_kernel,
        grid=(depth, d6 // tn),
        in_specs=[pl.BlockSpec((bc, d), lambda l, j: (0, 0)),
                  pl.BlockSpec((None, d, tn), lambda l, j: (l, 0, j)),
                  pl.BlockSpec((None, 1, tn), lambda l, j: (l, 0, j))],
        out_specs=pl.BlockSpec((None, bc, tn), lambda l, j: (l, 0, j)),
        out_shape=jax.ShapeDtypeStruct((depth, bc, d6), F32),
        compiler_params=_cparams(2, VMEM_LIMIT),
        name="ada_mods",
    )(c_all, w_ada, b_ada.reshape(depth, 1, d6))


def _mod_specs(cfg, col):
    d = cfg["d"]
    n_pt, tps, b = cfg["n_pt"], cfg["tps"], cfg["b"]
    return [pl.BlockSpec((1, 1, d), lambda i: (jnp.minimum(i // tps, b - 1), 0, col)),
            pl.BlockSpec((GP, 1, d), lambda i: (jnp.maximum(i - n_pt, 0), 0, col))]


def _rmsnorm_mod(x, g, sc, sh):
    inv = lax.rsqrt(jnp.mean(x * x, axis=-1, keepdims=True) + EPS)
    return x * inv * g * (1.0 + sc) + sh


def _inproj_kernel(cfg, x_ref, shp_ref, shs_ref, scp_ref, scs_ref, g_ref, w_ref, qg_ref, kg_ref,
                   up_ref, us_ref, qkv0_ref, qkv1_ref, qkv2_ref, qs_ref, kvs_ref,
                   kvp0_ref, kvp1_ref, kvp2_ref, res_ref):
    qkv_refs = (qkv0_ref, qkv1_ref, qkv2_ref)
    kvp_refs = (kvp0_ref, kvp1_ref, kvp2_ref)
    i = pl.program_id(0)
    n_pt, tps, d = cfg["n_pt"], cfg["tps"], cfg["d"]
    is_p = i < n_pt
    sh = jnp.where(is_p, shp_ref[...], shs_ref[...])
    sc = jnp.where(is_p, scp_ref[...], scs_ref[...])
    h = _rmsnorm_mod(x_ref[...], g_ref[...], sc, sh)
    z = jnp.dot(h.reshape(TM, d).astype(BF16), w_ref[...], preferred_element_type=F32)

    r = lax.broadcasted_iota(jnp.int32, (GROUP_DIM, GROUP_DIM), 0) // HEAD_DIM
    c = lax.broadcasted_iota(jnp.int32, (GROUP_DIM, GROUP_DIM), 1) // HEAD_DIM
    head_ones = (r == c).astype(BF16)

    def head_norm(t, gain):
        ss = jnp.dot((t * t).astype(BF16), head_ones, preferred_element_type=F32) * (1.0 / HEAD_DIM)
        return t * lax.rsqrt(ss + EPS) * gain

    u = z[:, :POOL_DIM]
    att = d - POOL_DIM
    q, k, v = [], [], []
    for g in range(N_GROUPS):
        lo = POOL_DIM + g * GROUP_DIM
        q.append(head_norm(z[:, lo:lo + GROUP_DIM], qg_ref[...]) * (HEAD_DIM ** -0.5))
        k.append(head_norm(z[:, lo + att:lo + att + GROUP_DIM], kg_ref[...]))
        v.append(z[:, lo + 2 * att:lo + 2 * att + GROUP_DIM])

    @pl.when(is_p)
    def _():
        up_ref[...] = u
        for g, (_, dil) in enumerate(ATT_GROUPS):
            for which, val in enumerate((q[g], k[g], v[g])):
                if dil == 1:
                    qkv_refs[g][which, 0] = val.astype(BF16)
                else:
                    for half in range(GROUP_DIM // LANES):
                        res_ref[half] = val[:, half * LANES:(half + 1) * LANES]
                    for r in range(dil):
                        rows = [res_ref[half, pl.ds(r, TM // dil, stride=dil), :]
                                for half in range(GROUP_DIM // LANES)]
                        qkv_refs[g][which, r] = jnp.concatenate(rows, axis=1).astype(BF16)

    @pl.when(jnp.logical_not(is_p))
    def _():
        us_ref[...] = u
        for g in range(N_GROUPS):
            qs_ref[g] = q[g]
            kvs_ref[g, :, :GROUP_DIM] = k[g]
            kvs_ref[g, :, GROUP_DIM:] = v[g]

    j = jnp.minimum(i, n_pt - 1) % tps
    for g, (keep, nk) in enumerate(cfg["kv_keep"]):
        rows = min(keep, TM)

        @pl.when(jnp.logical_and(is_p, j >= tps - nk))
        def _(g=g, rows=rows):
            kvp_refs[g][:, :GROUP_DIM] = k[g][TM - rows:]
            kvp_refs[g][:, GROUP_DIM:] = v[g][TM - rows:]


def _inproj(cfg, x3, mp, ms, gain, w_in_b, qg, kg):
    d, n_pt, n_t, tps, b = cfg["d"], cfg["n_pt"], cfg["n_t"], cfg["tps"], cfg["b"]
    np_, ns = cfg["np"], cfg["ns"]
    pidx = lambda i: jnp.minimum(i, n_pt - 1)
    sidx = lambda i: jnp.maximum(i - n_pt, 0)

    kv_specs, kv_shapes = [], []
    for keep, nk in cfg["kv_keep"]:
        rows = min(keep, TM)

        def kv_idx(i, nk=nk):
            ip = pidx(i)
            return ((ip // tps) * nk + jnp.maximum(ip % tps - (tps - nk), 0), 0)

        kv_specs.append(pl.BlockSpec((rows, 2 * GROUP_DIM), kv_idx))
        kv_shapes.append(jax.ShapeDtypeStruct((b * keep, 2 * GROUP_DIM), F32))

    qkv_specs, qkv_shapes = [], []
    for _, dil in ATT_GROUPS:
        qkv_specs.append(pl.BlockSpec((3, None, dil, TM // dil, GROUP_DIM),
                                      lambda i: (0, pidx(i) // tps, 0, pidx(i) % tps, 0)))
        qkv_shapes.append(jax.ShapeDtypeStruct((3, b, dil, cfg["s"] // dil, GROUP_DIM), BF16))

    return pl.pallas_call(
        functools.partial(_inproj_kernel, cfg),
        grid=(n_t,),
        in_specs=[pl.BlockSpec((GP, SUBLANES, d), lambda i: (i, 0, 0))]
        + _mod_specs(cfg, 0) + _mod_specs(cfg, 1)
        + [pl.BlockSpec((1, d), lambda i: (0, 0)),
           pl.BlockSpec(w_in_b.shape, lambda i: (0, 0)),
           pl.BlockSpec((1, GROUP_DIM), lambda i: (0, 0)),
           pl.BlockSpec((1, GROUP_DIM), lambda i: (0, 0))],
        out_specs=[pl.BlockSpec((TM, POOL_DIM), lambda i: (pidx(i), 0)),
                   pl.BlockSpec((TM, POOL_DIM), lambda i: (sidx(i), 0))] + qkv_specs
        + [pl.BlockSpec((N_GROUPS, TM, GROUP_DIM), lambda i: (0, sidx(i), 0)),
           pl.BlockSpec((N_GROUPS, TM, 2 * GROUP_DIM), lambda i: (0, sidx(i), 0))] + kv_specs,
        out_shape=[jax.ShapeDtypeStruct((np_, POOL_DIM), F32),
                   jax.ShapeDtypeStruct((ns, POOL_DIM), F32)] + qkv_shapes
        + [jax.ShapeDtypeStruct((N_GROUPS, ns, GROUP_DIM), F32),
           jax.ShapeDtypeStruct((N_GROUPS, ns, 2 * GROUP_DIM), F32)] + kv_shapes,
        scratch_shapes=[pltpu.VMEM((GROUP_DIM // LANES, TM, LANES), F32)],
        compiler_params=_cparams(1, VMEM_LIMIT),
        name="inproj",
    )(x3, mp, ms, mp, ms, gain, w_in_b, qg, kg)


def _rel_bucket_np(dist):
    dist = np.asarray(dist, np.int32)
    max_exact = REL_BUCKETS // 2
    d_f = np.maximum(dist, 1).astype(np.float32)
    ratio = np.log(d_f / np.float32(max_exact)) / np.float32(math.log(REL_MAX_DIST / max_exact))
    large = max_exact + (ratio * np.float32(REL_BUCKETS - max_exact)).astype(np.int32)
    large = np.minimum(large, REL_BUCKETS - 1)
    return np.where(dist < max_exact, dist, large)


def _bias_lookup(rel_bias, g, bucket, ok):
    table = rel_bias[:, g * HEADS:(g + 1) * HEADS].astype(F32)
    vals = jnp.zeros((HEADS,) + bucket.shape, F32)
    for bkt in np.unique(bucket):
        vals = jnp.where((bucket == bkt)[None], table[bkt].reshape((HEADS,) + (1,) * bucket.ndim), vals)
    return jnp.where(ok[None], vals, NEG_INF)


def _prompt_bias_tables(rel_bias):
    qi = np.arange(BAND)[:, None]
    kj = np.arange(2 * BAND)[None, :]
    delta = qi + BAND - kj
    ok = (delta >= 0) & (delta <= BAND)
    ok_first = ok & (kj >= BAND)
    tables = []
    for g, (_, dil) in enumerate(ATT_GROUPS):
        bucket = _rel_bucket_np(np.maximum(delta, 0) * dil)
        both = jnp.stack([_bias_lookup(rel_bias, g, bucket, ok), _bias_lookup(rel_bias, g, bucket, ok_first)])
        tables.append(both.reshape(2, HEADS * BAND, 2 * BAND))
    return tables


def _sample_bias_tables(rel_bias, t_new):
    tables = []
    for g, (win, dil) in enumerate(ATT_GROUPS):
        lb = min(win, PAST_LEN)
        t = np.arange(t_new)[:, None]
        col = np.arange(lb + LANES)[None, :]
        delta = lb + t - col
        ok = (col < lb + t_new) & (delta >= 0) & (delta % dil == 0) & (delta // dil <= win // dil)
        bucket = _rel_bucket_np(np.maximum(delta, 0))
        tables.append(_bias_lookup(rel_bias, g, bucket, ok).reshape(HEADS * t_new, lb + LANES))
    return tables


def _stack_heads(q):
    lh = _lane_head()
    return jnp.concatenate([jnp.where(lh == h, q, jnp.zeros_like(q)) for h in range(HEADS)], axis=0)


def _softmax(s):
    m = jnp.max(s, axis=1, keepdims=True)
    p = jnp.exp(s - m)
    l = jnp.sum(p, axis=1, keepdims=True)
    return p.astype(BF16), l, m + jnp.log(l)


def _unstack_heads(o4, lse4, rows):
    lh = _lane_head()
    o = o4[0:rows]
    lse = jnp.broadcast_to(lse4[0:rows], (rows, GROUP_DIM))
    for h in range(1, HEADS):
        o = jnp.where(lh == h, o4[h * rows:(h + 1) * rows], o)
        lse = jnp.where(lh == h, lse4[h * rows:(h + 1) * rows], lse)
    return o, lse


_NT = (((1,), (1,)), ((), ()))


def _attn_prompt_kernel(nsub, q_ref, kc_ref, kp_ref, vc_ref, vp_ref, bias_ref, o_ref, lse_ref, kbuf, vbuf):
    i = pl.program_id(2)
    kbuf[0:BAND] = kp_ref[...]
    kbuf[BAND:] = kc_ref[...]
    vbuf[0:BAND] = vp_ref[...]
    vbuf[BAND:] = vc_ref[...]
    for s in range(nsub):
        q4 = _stack_heads(q_ref[s * BAND:(s + 1) * BAND])
        kc = kbuf[s * BAND:(s + 2) * BAND]
        vc = vbuf[s * BAND:(s + 2) * BAND]
        logits = lax.dot_general(q4, kc, _NT, preferred_element_type=F32)
        if s == 0:
            bias = bias_ref[jnp.where(i == 0, 1, 0)]
        else:
            bias = bias_ref[0]
        p, l, lse4 = _softmax(logits + bias)
        o4 = jnp.dot(p, vc, preferred_element_type=F32) / l
        o, lse = _unstack_heads(o4, lse4, BAND)
        o_ref[s * BAND:(s + 1) * BAND] = o
        lse_ref[s * BAND:(s + 1) * BAND] = lse


def _attn_prompt(cfg, g, qkv, bias):
    b, s_len = cfg["b"], cfg["s"]
    _, dil = ATT_GROUPS[g]
    l_len = s_len // dil
    tq = min(TM, l_len)
    nsub = tq // BAND
    nq = l_len // tq

    def cur(which):
        return pl.BlockSpec((None, None, None, tq, GROUP_DIM), lambda bi, r, i: (which, bi, r, i, 0))

    def prev(which):
        return pl.BlockSpec((None, None, None, BAND, GROUP_DIM),
                            lambda bi, r, i: (which, bi, r, jnp.maximum(i * nsub - 1, 0), 0))

    out = pl.BlockSpec((None, None, tq, GROUP_DIM), lambda bi, r, i: (bi, r, i, 0))
    return pl.pallas_call(
        functools.partial(_attn_prompt_kernel, nsub),
        grid=(b, dil, nq),
        in_specs=[cur(0), cur(1), prev(1), cur(2), prev(2),
                  pl.BlockSpec(bias.shape, lambda bi, r, i: (0, 0, 0))],
        out_specs=[out, out],
        out_shape=[jax.ShapeDtypeStruct((b, dil, l_len, GROUP_DIM), F32)] * 2,
        scratch_shapes=[pltpu.VMEM((tq + BAND, GROUP_DIM), BF16)] * 2,
        compiler_params=_cparams(3, VMEM_LIMIT),
        name=f"attn_prompt_g{g}",
    )(qkv, qkv, qkv, qkv, qkv, bias)


def _attn_sample_kernel(bs, lb, t_new, q_ref, kvn_ref, cache_ref, bias_ref, o_ref, lse_ref):
    pad = jnp.zeros((LANES - t_new, GROUP_DIM), BF16)
    for j in range(bs):
        kt = cache_ref[j, 0:GROUP_DIM, :].astype(BF16)
        vt = cache_ref[j, GROUP_DIM:2 * GROUP_DIM, :].astype(BF16)
        kvn = kvn_ref[j].astype(BF16)
        kn = jnp.concatenate([kvn[:, :GROUP_DIM], pad], axis=0)
        vn = jnp.concatenate([kvn[:, GROUP_DIM:], pad], axis=0)
        q4 = _stack_heads(q_ref[j].astype(BF16))
        logits = jnp.concatenate(
            [jnp.dot(q4, kt, preferred_element_type=F32),
             lax.dot_general(q4, kn, _NT, preferred_element_type=F32)], axis=1)
        p, l, lse4 = _softmax(logits + bias_ref[...])
        o4 = (lax.dot_general(p[:, :lb], vt, _NT, preferred_element_type=F32)
              + jnp.dot(p[:, lb:], vn, preferred_element_type=F32)) / l
        o, lse = _unstack_heads(o4, lse4, t_new)
        o_ref[j] = o
        lse_ref[j] = lse


def _attn_sample(cfg, g, layer, qs, kvs, cache_t, bias):
    bd, t_new = cfg["bd"], cfg["t"]
    lb = cache_t.shape[-1]
    bs = max(1, min(8, (4 * 1024 * 1024) // (2 * GROUP_DIM * lb * 4)))
    out = pl.BlockSpec((bs, t_new, GROUP_DIM), lambda i: (i, 0, 0))
    o, lse = pl.pallas_call(
        functools.partial(_attn_sample_kernel, bs, lb, t_new),
        grid=(bd // bs,),
        in_specs=[pl.BlockSpec((None, bs, t_new, GROUP_DIM), lambda i: (g, i, 0, 0)),
                  pl.BlockSpec((None, bs, t_new, 2 * GROUP_DIM), lambda i: (g, i, 0, 0)),
                  pl.BlockSpec((None, bs, 2 * GROUP_DIM, lb), lambda i: (layer, i, 0, 0)),
                  pl.BlockSpec(bias.shape, lambda i: (0, 0))],
        out_specs=[out, out],
        out_shape=[jax.ShapeDtypeStruct((bd, t_new, GROUP_DIM), F32)] * 2,
        compiler_params=_cparams(1, VMEM_LIMIT),
        name=f"attn_sample_g{g}",
    )(qs.reshape(N_GROUPS, bd, t_new, GROUP_DIM), kvs.reshape(N_GROUPS, bd, t_new, 2 * GROUP_DIM),
      cache_t, bias)
    return o.reshape(bd * t_new, GROUP_DIM), lse.reshape(bd * t_new, GROUP_DIM)


def _pool_mix(ext_ref, rows, pos, w_ref, scale_ref, o_ref):
    lo = POOL_PAD
    u = ext_ref[:, lo:lo + rows, :]
    acc = u
    sums = {}
    for j in range(1, max(POOL_WINDOWS)):
        acc = acc + ext_ref[:, lo - j:lo - j + rows, :]
        if j + 1 in POOL_WINDOWS:
            sums[j + 1] = acc
    lane_grp = lax.broadcasted_iota(jnp.int32, (1, 1, POOL_DIM), 2) // (POOL_DIM // len(POOL_WINDOWS))
    z = None
    for gi, w in enumerate(POOL_WINDOWS):
        cnt = jnp.minimum(w, pos + 1).astype(F32)
        zw = sums[w] / cnt - u
        z = zw if z is None else jnp.where(lane_grp == gi, zw, z)
    nb = u.shape[0]
    y = jnp.dot(z.reshape(nb * rows, POOL_DIM).astype(BF16), w_ref[...], preferred_element_type=F32)
    o_ref[...] = (y * scale_ref[...]).astype(BF16)


def _pool_prompt_kernel(u_ref, halo_ref, w_ref, scale_ref, o_ref, ext_ref):
    i = pl.program_id(1)
    halo = halo_ref[...]
    ext_ref[:, 0:POOL_PAD, :] = jnp.where(i == 0, jnp.zeros_like(halo), halo)
    ext_ref[:, POOL_PAD:, :] = u_ref[...]
    pos = i * TM + lax.broadcasted_iota(jnp.int32, (1, TM, 1), 1)
    _pool_mix(ext_ref, TM, pos, w_ref, scale_ref, o_ref)


def _pool_sample_kernel(t_new, u_ref, hist_ref, w_ref, scale_ref, o_ref, ext_ref):
    nb = u_ref.shape[0]
    ext_ref[:, 0:POOL_PAD - POOL_HIST, :] = jnp.zeros((nb, POOL_PAD - POOL_HIST, POOL_DIM), F32)
    ext_ref[:, POOL_PAD - POOL_HIST:POOL_PAD, :] = hist_ref[...]
    ext_ref[:, POOL_PAD:, :] = u_ref[...]
    pos = PAST_LEN + lax.broadcasted_iota(jnp.int32, (1, t_new, 1), 1)
    _pool_mix(ext_ref, t_new, pos, w_ref, scale_ref, o_ref)


def _pool_prompt(cfg, u_p, w_bd, scale):
    b, s_len = cfg["b"], cfg["s"]
    u3 = u_p.reshape(b, s_len, POOL_DIM)
    per = TM // POOL_PAD
    return pl.pallas_call(
        _pool_prompt_kernel,
        grid=(b, s_len // TM),
        in_specs=[pl.BlockSpec((1, TM, POOL_DIM), lambda bi, i: (bi, i, 0)),
                  pl.BlockSpec((1, POOL_PAD, POOL_DIM), lambda bi, i: (bi, jnp.maximum(i * per - 1, 0), 0)),
                  pl.BlockSpec((POOL_DIM, POOL_DIM), lambda bi, i: (0, 0)),
                  pl.BlockSpec((1, POOL_DIM), lambda bi, i: (0, 0))],
        out_specs=pl.BlockSpec((TM, POOL_DIM), lambda bi, i: (bi * (s_len // TM) + i, 0)),
        out_shape=jax.ShapeDtypeStruct((b * s_len, POOL_DIM), BF16),
        scratch_shapes=[pltpu.VMEM((1, TM + POOL_PAD, POOL_DIM), F32)],
        compiler_params=_cparams(2, VMEM_LIMIT),
        name="pool_prompt",
    )(u3, u3, w_bd, scale)


def _pool_sample(cfg, layer, u_s, state_pool, w_bd, scale):
    bd, t_new = cfg["bd"], cfg["t"]
    u3 = u_s.reshape(bd, t_new, POOL_DIM)
    nb = GP
    return pl.pallas_call(
        functools.partial(_pool_sample_kernel, t_new),
        grid=(bd // nb,),
        in_specs=[pl.BlockSpec((nb, t_new, POOL_DIM), lambda i: (i, 0, 0)),
                  pl.BlockSpec((None, nb, POOL_HIST, POOL_DIM), lambda i: (layer, i, 0, 0)),
                  pl.BlockSpec((POOL_DIM, POOL_DIM), lambda i: (0, 0)),
                  pl.BlockSpec((1, POOL_DIM), lambda i: (0, 0))],
        out_specs=pl.BlockSpec((nb * t_new, POOL_DIM), lambda i: (i, 0)),
        out_shape=jax.ShapeDtypeStruct((bd * t_new, POOL_DIM), BF16),
        scratch_shapes=[pltpu.VMEM((nb, t_new + POOL_PAD, POOL_DIM), F32)],
        compiler_params=_cparams(1, VMEM_LIMIT),
        name="pool_sample",
    )(u3, state_pool, w_bd, scale)


def _outproj_kernel(cfg, x_ref, g1p_ref, g1s_ref, shp_ref, shs_ref, scp_ref, scs_ref, pp_ref, ps_ref, *rest):
    att_refs = rest[:4 * N_GROUPS]
    (gain_ref, wo_ref, wr_ref, br_ref, x1_ref, h2_ref, route_ref, cnt_ref,
     run_ref, nat_ref) = rest[4 * N_GROUPS:]
    i = pl.program_id(0)
    n_pt, d = cfg["n_pt"], cfg["d"]
    is_p = i < n_pt

    halves = GROUP_DIM // LANES

    def put(slot, val, rows=None):
        for half in range(halves):
            piece = val[:, half * LANES:(half + 1) * LANES]
            if rows is None:
                nat_ref[slot, half] = piece
            else:
                nat_ref[slot, half, rows, :] = piece

    @pl.when(is_p)
    def _():
        for g, (_, dil) in enumerate(ATT_GROUPS):
            for which in range(2):
                src = att_refs[4 * g + which]
                if dil == 1:
                    put(2 * g + which, src[0])
                else:
                    for r in range(dil):
                        put(2 * g + which, src[r], pl.ds(r, TM // dil, stride=dil))

    @pl.when(jnp.logical_not(is_p))
    def _():
        for g in range(N_GROUPS):
            for which in range(2):
                put(2 * g + which, att_refs[4 * g + 2 + which][...])

    def get(slot):
        return jnp.concatenate([nat_ref[slot, half] for half in range(halves)], axis=1)

    o = [get(2 * g) for g in range(N_GROUPS)]
    lse = [get(2 * g + 1) for g in range(N_GROUPS)]
    mx = functools.reduce(jnp.maximum, lse)
    e = [jnp.exp(l - mx) for l in lse]
    attn = sum(eg * og for eg, og in zip(e, o)) / sum(e)
    pool = jnp.where(is_p, pp_ref[...], ps_ref[...])
    cat = jnp.concatenate([pool, attn.astype(BF16)], axis=1)
    y = jnp.dot(cat, wo_ref[...], preferred_element_type=F32)

    g1 = jnp.where(is_p, g1p_ref[...], g1s_ref[...])
    x1 = x_ref[...] + g1 * y.reshape(GP, SUBLANES, d)
    x1_ref[...] = x1
    sh = jnp.where(is_p, shp_ref[...], shs_ref[...])
    sc = jnp.where(is_p, scp_ref[...], scs_ref[...])
    h2 = _rmsnorm_mod(x1, gain_ref[...], sc, sh).reshape(TM, d)
    _rows_to_slabs(h2_ref, h2)

    h_hi = h2.astype(BF16)
    h_lo = (h2 - h_hi.astype(F32)).astype(BF16)
    wr = wr_ref[...]
    w_hi = wr.astype(BF16)
    w_lo = (wr - w_hi.astype(F32)).astype(BF16)
    logits = (jnp.dot(h_hi, w_hi, preferred_element_type=F32)
              + jnp.dot(h_lo, w_hi, preferred_element_type=F32)
              + jnp.dot(h_hi, w_lo, preferred_element_type=F32)) + br_ref[...]

    lane = lax.broadcasted_iota(jnp.int32, (TM, LANES), 1).astype(F32)
    vals = logits
    top_v, top_i = [], []
    onehot = jnp.zeros((TM, LANES), F32)
    for _ in range(TOP_K):
        m = jnp.max(vals, axis=1, keepdims=True)
        idx = jnp.min(jnp.where(vals == m, lane, float(LANES)), axis=1, keepdims=True)
        hit = lane == idx
        vals = jnp.where(hit, -jnp.inf, vals)
        onehot = jnp.where(hit, 1.0, onehot)
        top_v.append(m)
        top_i.append(idx)
    ev = [jnp.exp(v - top_v[0]) for v in top_v]
    den = sum(ev)

    @pl.when(i == 0)
    def _():
        run_ref[...] = jnp.zeros_like(run_ref)

    row = lax.broadcasted_iota(jnp.int32, (TM, TM), 0)
    col = lax.broadcasted_iota(jnp.int32, (TM, TM), 1)
    before = (col < row).astype(BF16)
    rank_all = jnp.dot(before, onehot.astype(BF16), preferred_element_type=F32) + run_ref[...]
    route = jnp.zeros((TM, LANES), F32)
    for k in range(TOP_K):
        rank_k = jnp.sum(jnp.where(lane == top_i[k], rank_all, 0.0), axis=1, keepdims=True)
        route = jnp.where(lane == float(k), top_i[k], route)
        route = jnp.where(lane == float(TOP_K + k), ev[k] / den, route)
        route = jnp.where(lane == float(2 * TOP_K + k), rank_k, route)
    route_ref[...] = route
    run_ref[...] = run_ref[...] + jnp.sum(onehot, axis=0, keepdims=True)
    cnt_ref[...] = run_ref[...]


def _outproj(cfg, x3, mp, ms, pool_p, pool_s, att, gain, w_out_b, w_r, b_r):
    d, n_pt, n_t = cfg["d"], cfg["n_pt"], cfg["n_t"]
    n = cfg["n"]
    tps = cfg["tps"]
    pidx = lambda i: (jnp.minimum(i, n_pt - 1), 0)
    sidx = lambda i: (jnp.maximum(i - n_pt, 0), 0)
    pspec = pl.BlockSpec((TM, GROUP_DIM), pidx)
    sspec = pl.BlockSpec((TM, GROUP_DIM), sidx)
    att_specs, att_args = [], []
    for (_, dil), (o_p, lse_p, o_s, lse_s) in zip(ATT_GROUPS, att):
        rspec = pl.BlockSpec((None, dil, TM // dil, GROUP_DIM),
                             lambda i: (pidx(i)[0] // tps, 0, pidx(i)[0] % tps, 0))
        att_specs += [rspec, rspec, sspec, sspec]
        att_args += [o_p, lse_p, o_s, lse_s]
    full = lambda a: pl.BlockSpec(a.shape, lambda i: (0,) * a.ndim)
    return pl.pallas_call(
        functools.partial(_outproj_kernel, cfg),
        grid=(n_t,),
        in_specs=[pl.BlockSpec((GP, SUBLANES, d), lambda i: (i, 0, 0))]
        + _mod_specs(cfg, 2) + _mod_specs(cfg, 3) + _mod_specs(cfg, 4)
        + [pspec, sspec] + att_specs + [full(gain), full(w_out_b), full(w_r), full(b_r)],
        out_specs=[pl.BlockSpec((GP, SUBLANES, d), lambda i: (i, 0, 0)),
                   pl.BlockSpec((TM * SUBLANES, LANES), lambda i: (i, 0)),
                   pl.BlockSpec((TM, LANES), lambda i: (i, 0)),
                   pl.BlockSpec((1, LANES), lambda i: (0, 0))],
        out_shape=[jax.ShapeDtypeStruct(x3.shape, F32),
                   jax.ShapeDtypeStruct((n * SUBLANES, LANES), F32),
                   jax.ShapeDtypeStruct((n, LANES), F32),
                   jax.ShapeDtypeStruct((1, LANES), F32)],
        scratch_shapes=[pltpu.VMEM((1, LANES), F32),
                        pltpu.VMEM((2 * N_GROUPS, GROUP_DIM // LANES, TM, LANES), F32)],
        compiler_params=_cparams(1, VMEM_LIMIT),
        name="outproj_router",
    )(x3, mp, ms, mp, ms, mp, ms, pool_p, pool_s, *att_args, gain, w_out_b, w_r, b_r)


def _rows_from_slabs(ref):
    rows = ref.shape[0] // SUBLANES
    return jnp.concatenate([ref[pl.ds(j, rows, stride=SUBLANES), :] for j in range(SUBLANES)], axis=1)


def _rows_to_slabs(ref, val):
    for j in range(SUBLANES):
        ref[pl.ds(j, val.shape[0], stride=SUBLANES), :] = val[:, j * LANES:(j + 1) * LANES]


def _moe_kernel(te_ref, src0_ref, nxt_ref, prv_ref, last_ref, h_hbm, w1_ref, b1_ref, w2_ref, b2_ref, y_hbm,
                x0, x1, y0, y1, w1b, w2b, gsem, ssem):
    i = pl.program_id(0)
    n = pl.num_programs(0)
    ff = w2_ref.shape[0]
    xs, ys = (x0, x1), (y0, y1)

    def start_gather(idx_ref, p):
        for r in range(MOE_TM):
            pltpu.make_async_copy(h_hbm.at[idx_ref[0, r]], xs[p].at[pl.ds(r * SUBLANES, SUBLANES)],
                                  gsem.at[p]).start()

    def start_scatter(idx_ref, p):
        for r in range(MOE_TM):
            pltpu.make_async_copy(ys[p].at[pl.ds(r * SUBLANES, SUBLANES)], y_hbm.at[idx_ref[0, r]],
                                  ssem.at[p]).start()

    def wait_gather(p):
        pltpu.make_async_copy(xs[p], xs[p], gsem.at[p]).wait()

    def wait_scatter(p):
        pltpu.make_async_copy(ys[p], ys[p], ssem.at[p]).wait()

    @pl.when(i == 0)
    def _():
        start_gather(src0_ref, 0)
        n_asg = y_hbm.shape[0] - 2 * MOE_TM
        for p in range(2):
            ys[p][...] = jnp.zeros_like(ys[p])
            for r in range(MOE_TM):
                pltpu.make_async_copy(ys[p].at[pl.ds(r * SUBLANES, SUBLANES)],
                                      y_hbm.at[n_asg + p * MOE_TM + r], ssem.at[p]).start()
            wait_scatter(p)

    @pl.when(jnp.logical_or(i == 0, te_ref[i] != te_ref[jnp.maximum(i - 1, 0)]))
    def _():
        w1b[...] = w1_ref[...].astype(BF16)
        w2b[...] = w2_ref[...].astype(BF16)

    def step(p):
        wait_gather(p)
        start_gather(nxt_ref, 1 - p)
        start_scatter(prv_ref, 1 - p)
        x = _rows_from_slabs(xs[p]).astype(BF16)
        h1 = jnp.dot(x, w1b[...], preferred_element_type=F32) + b1_ref[...]
        gate = jnp.minimum(h1[:, :ff], SWIGLU_LIMIT)
        up = jnp.clip(h1[:, ff:], -SWIGLU_LIMIT, SWIGLU_LIMIT)
        act = gate * jax.nn.sigmoid(SWIGLU_ALPHA * gate) * (up + 1.0)
        _rows_to_slabs(ys[p], jnp.dot(act.astype(BF16), w2b[...], preferred_element_type=F32) + b2_ref[...])
        wait_scatter(1 - p)

        @pl.when(i == n - 1)
        def _():
            start_scatter(last_ref, p)
            wait_scatter(p)
            wait_gather(1 - p)

    for p in range(2):
        pl.when(i % 2 == p)(functools.partial(step, p))


def _moe(layer, n_tiles, tile_expert, slot_src, slot_dst, h2, w1, b1, w2, b2, n_rows_out):
    depth, n_exp, d, ff2 = w1.shape
    ff = w2.shape[2]
    smem_tile = lambda f: pl.BlockSpec((None, 1, MOE_TM), f, memory_space=pltpu.SMEM)
    grid_spec = pltpu.PrefetchScalarGridSpec(
        num_scalar_prefetch=1,
        grid=(n_tiles,),
        in_specs=[smem_tile(lambda i, te: (0, 0, 0)),
                  smem_tile(lambda i, te: (jnp.minimum(i + 1, n_tiles - 1), 0, 0)),
                  smem_tile(lambda i, te: (jnp.maximum(i - 1, 0), 0, 0)),
                  smem_tile(lambda i, te: (n_tiles - 1, 0, 0)),
                  pl.BlockSpec(memory_space=pl.ANY),
                  pl.BlockSpec((None, None, d, ff2), lambda i, te: (layer, te[i], 0, 0)),
                  pl.BlockSpec((None, None, 1, ff2), lambda i, te: (layer, te[i], 0, 0)),
                  pl.BlockSpec((None, None, ff, d), lambda i, te: (layer, te[i], 0, 0)),
                  pl.BlockSpec((None, None, 1, d), lambda i, te: (layer, te[i], 0, 0))],
        out_specs=pl.BlockSpec(memory_space=pl.ANY),
        scratch_shapes=[pltpu.VMEM((MOE_TM * SUBLANES, LANES), F32)] * 4
        + [pltpu.VMEM((d, ff2), BF16),
           pltpu.VMEM((ff, d), BF16),
           pltpu.SemaphoreType.DMA((2,)),
           pltpu.SemaphoreType.DMA((2,))],
    )
    src3 = slot_src.reshape(n_tiles, 1, MOE_TM)
    dst3 = slot_dst.reshape(n_tiles, 1, MOE_TM)
    return pl.pallas_call(
        _moe_kernel,
        grid_spec=grid_spec,
        out_shape=jax.ShapeDtypeStruct((n_rows_out, SUBLANES, LANES), F32),
        compiler_params=_cparams(1, VMEM_LIMIT),
        name="moe_experts",
    )(tile_expert, src3, src3, dst3, dst3, h2,
      w1, b1.reshape(depth, n_exp, 1, ff2), w2, b2.reshape(depth, n_exp, 1, d))


def _moe_plan(cfg, route, counts):
    n = cfg["n"]
    m = n * TOP_K
    n_tiles = m // MOE_TM + N_EXPERTS
    m_pad = n_tiles * MOE_TM
    top_i = route[:, 0:TOP_K].astype(jnp.int32)
    rank = route[:, 2 * TOP_K:3 * TOP_K].astype(jnp.int32)
    cnt = counts[0, :N_EXPERTS].astype(jnp.int32)
    tiles_e = (cnt + MOE_TM - 1) // MOE_TM
    tile_end = jnp.cumsum(tiles_e)
    pstart = (tile_end - tiles_e) * MOE_TM
    expert = jnp.arange(N_EXPERTS, dtype=jnp.int32)
    pos = jnp.sum(jnp.where(top_i[..., None] == expert, pstart, 0), axis=-1) + rank
    t = jnp.arange(n_tiles, dtype=jnp.int32)
    te = jnp.sum((tile_end[None, :] <= t[:, None]).astype(jnp.int32), axis=1)
    last_used = jnp.sum((tile_end <= tile_end[-1] - 1).astype(jnp.int32))
    te = jnp.minimum(te, last_used).astype(jnp.int32)
    s = jnp.arange(m_pad, dtype=jnp.int32)
    dump = m + ((s // MOE_TM) % 2) * MOE_TM + s % MOE_TM
    slot_dst = dump.at[pos.T.reshape(-1)].set(jnp.arange(m, dtype=jnp.int32))
    slot_src = jnp.where(slot_dst < m, slot_dst % n, 0)
    return n_tiles, te, slot_src, slot_dst, m + 2 * MOE_TM


def _combine_kernel(cfg, x_ref, g2p_ref, g2s_ref, route_ref, *rest):
    y_refs, o_ref = rest[:TOP_K], rest[TOP_K]
    i = pl.program_id(0)
    is_p = i < cfg["n_pt"]
    route = route_ref[...]
    acc = route[:, TOP_K:TOP_K + 1] * _rows_from_slabs(y_refs[0])
    for k in range(1, TOP_K):
        acc = acc + route[:, TOP_K + k:TOP_K + k + 1] * _rows_from_slabs(y_refs[k])
    g2 = jnp.where(is_p, g2p_ref[...], g2s_ref[...])
    o_ref[...] = x_ref[...] + g2 * acc.reshape(GP, SUBLANES, cfg["d"])


def _combine(cfg, x1, mp, ms, route, ybuf):
    d, n_t = cfg["d"], cfg["n_t"]
    y_specs = [pl.BlockSpec((TM * SUBLANES, LANES), lambda i, k=k: (k * n_t + i, 0)) for k in range(TOP_K)]
    ybuf = ybuf.reshape(ybuf.shape[0] * SUBLANES, LANES)
    return pl.pallas_call(
        functools.partial(_combine_kernel, cfg),
        grid=(n_t,),
        in_specs=[pl.BlockSpec((GP, SUBLANES, d), lambda i: (i, 0, 0))] + _mod_specs(cfg, 5)
        + [pl.BlockSpec((TM, LANES), lambda i: (i, 0))] + y_specs,
        out_specs=pl.BlockSpec((GP, SUBLANES, d), lambda i: (i, 0, 0)),
        out_shape=jax.ShapeDtypeStruct(x1.shape, F32),
        compiler_params=_cparams(1, VMEM_LIMIT),
        name="moe_combine",
    )(x1, mp, ms, route, *([ybuf] * TOP_K))


def kernel(x_prompt, x_sample, cache_kv_w128_d1, cache_kv_w512_d4, cache_kv_w2048_d16, state_pool,
           c_prompt, c_sample, rel_bias, norm_mix, norm_ffn, w_ada, b_ada, w_in, q_norm, k_norm,
           pool_w, pool_scale, w_out, w_router, b_router, w_expert_in, b_expert_in,
           w_expert_out, b_expert_out):
    b, s_len, d = x_prompt.shape
    bd, t_new, _ = x_sample.shape
    depth = w_in.shape[0]
    np_, ns = b * s_len, bd * t_new
    assert t_new == SUBLANES and s_len % TM == 0 and ns % TM == 0
    assert all((s_len // dil) % BAND == 0 for _, dil in ATT_GROUPS)
    cfg = dict(b=b, s=s_len, d=d, bd=bd, t=t_new, np=np_, ns=ns, n=np_ + ns,
               n_pt=np_ // TM, n_t=(np_ + ns) // TM, tps=s_len // TM)
    kv_keep = []
    for win, _ in ATT_GROUPS:
        keep = min(win, s_len)
        assert keep % TM == 0 or TM % keep == 0
        kv_keep.append((keep, max(keep // TM, 1)))
    cfg["kv_keep"] = tuple(kv_keep)
    caches_t = [jnp.transpose(c, (0, 1, 3, 4, 5, 2)).reshape(depth, bd, 2 * GROUP_DIM, c.shape[2])
                for c in (cache_kv_w128_d1, cache_kv_w512_d4, cache_kv_w2048_d16)]

    mods = _ada_mods(jnp.concatenate([c_prompt, c_sample], axis=0), w_ada, b_ada)
    bias_p = _prompt_bias_tables(rel_bias)
    bias_s = _sample_bias_tables(rel_bias, t_new)
    w_in_b = w_in.astype(BF16)
    w_out_b = w_out.astype(BF16)
    eye = jnp.eye(len(POOL_WINDOWS), dtype=F32)
    w_pool_bd = (eye[None, :, None, :, None] * pool_w[:, :, :, None, :]).reshape(depth, POOL_DIM, POOL_DIM)
    w_pool_bd = w_pool_bd.astype(BF16)
    w_r_pad = jnp.pad(w_router, ((0, 0), (0, 0), (0, LANES - N_EXPERTS)))
    b_r_pad = jnp.pad(b_router, ((0, 0), (0, LANES - N_EXPERTS)), constant_values=NEG_INF)

    x3 = jnp.concatenate([x_prompt.reshape(np_, d), x_sample.reshape(ns, d)], axis=0)
    x3 = x3.reshape((np_ + ns) // SUBLANES, SUBLANES, d)

    kv_p = [[] for _ in ATT_GROUPS]
    kv_s = [[] for _ in ATT_GROUPS]
    pool_p_state, pool_s_state = [], []
    for l in range(depth):
        mp = mods[l, :b].reshape(b, 1, 6 * d)
        ms = mods[l, b:].reshape(bd, 1, 6 * d)
        qg = jnp.tile(q_norm[l], HEADS).reshape(1, GROUP_DIM)
        kg = jnp.tile(k_norm[l], HEADS).reshape(1, GROUP_DIM)
        (u_p, u_s, qkv0, qkv1, qkv2, qs, kvs, *kvp) = _inproj(
            cfg, x3, mp, ms, norm_mix[l].reshape(1, d), w_in_b[l], qg, kg)

        att = []
        for g, qkv in enumerate((qkv0, qkv1, qkv2)):
            o_p, lse_p = _attn_prompt(cfg, g, qkv, bias_p[g])
            o_s, lse_s = _attn_sample(cfg, g, l, qs, kvs, caches_t[g], bias_s[g])
            att.append((o_p, lse_p, o_s, lse_s))
            keep = cfg["kv_keep"][g][0]
            kv_p[g].append(kvp[g].reshape(b, keep, 2, HEADS, HEAD_DIM))
            kv_s[g].append(kvs[g].reshape(bd, t_new, 2, HEADS, HEAD_DIM))

        scale = pool_scale[l].reshape(1, POOL_DIM)
        pool_p = _pool_prompt(cfg, u_p, w_pool_bd[l], scale)
        pool_s = _pool_sample(cfg, l, u_s, state_pool, w_pool_bd[l], scale)
        pool_p_state.append(u_p.reshape(b, s_len, POOL_DIM)[:, s_len - POOL_HIST:])
        pool_s_state.append(jnp.concatenate(
            [state_pool[l], u_s.reshape(bd, t_new, POOL_DIM)], axis=1)[:, -POOL_HIST:])

        x1, h2, route, counts = _outproj(cfg, x3, mp, ms, pool_p, pool_s, att,
                                         norm_ffn[l].reshape(1, d), w_out_b[l], w_r_pad[l],
                                         b_r_pad[l].reshape(1, LANES))
        n_tiles, te, slot_src, slot_dst, n_rows_out = _moe_plan(cfg, route, counts)
        ybuf = _moe(l, n_tiles, te, slot_src, slot_dst, h2.reshape(cfg["n"], SUBLANES, LANES),
                    w_expert_in, b_expert_in, w_expert_out, b_expert_out, n_rows_out)
        x3 = _combine(cfg, x1, mp, ms, route, ybuf)

    x_all = x3.reshape(np_ + ns, d)
    y_prompt = x_all[:np_].reshape(b, s_len, d)
    y_sample = x_all[np_:].reshape(bd, t_new, d)
    return (y_prompt, y_sample,
            jnp.stack(kv_p[0]), jnp.stack(kv_p[1]), jnp.stack(kv_p[2]), jnp.stack(pool_p_state),
            jnp.stack(kv_s[0]), jnp.stack(kv_s[1]), jnp.stack(kv_s[2]), jnp.stack(pool_s_state))
```

```python
import functools
import math

import numpy as np
import jax
import jax.numpy as jnp
from jax import lax
from jax.experimental import pallas as pl
from jax.experimental.pallas import tpu as pltpu

F32 = jnp.float32
BF16 = jnp.bfloat16

HEAD_DIM = 64
HEADS = 4
GROUP_DIM = HEADS * HEAD_DIM
ATT_GROUPS = ((128, 1), (512, 4), (2048, 16))
N_GROUPS = len(ATT_GROUPS)
BAND = 128
POOL_WINDOWS = (2, 4, 8, 16)
POOL_HIST = 15
POOL_DIM = 256
POOL_PAD = 16
REL_BUCKETS = 32
REL_MAX_DIST = 2048
N_EXPERTS = 32
TOP_K = 4
SWIGLU_LIMIT = 7.0
SWIGLU_ALPHA = 1.702
EPS = 1e-6
NEG_INF = -1e30
PAST_LEN = 2048

SUBLANES = 8
LANES = 128
TM = 512
GP = TM // SUBLANES
MOE_TM = 256
VMEM_LIMIT = 52 * 1024 * 1024


def _cparams(n_axes, vmem=None):
    return pltpu.CompilerParams(dimension_semantics=("arbitrary",) * n_axes,
                                vmem_limit_bytes=vmem)


def _lane_head(width=GROUP_DIM):
    return lax.broadcasted_iota(jnp.int32, (1, width), 1) // HEAD_DIM


def _ada_kernel(c_ref, w_ref, b_ref, o_ref):
    c = c_ref[...]
    a = (c * jax.nn.sigmoid(c)).astype(BF16)
    o_ref[...] = jnp.dot(a, w_ref[...].astype(BF16), preferred_element_type=F32) + b_ref[...]


def _ada_mods(c_all, w_ada, b_ada):
    depth, d, d6 = w_ada.shape
    bc = c_all.shape[0]
    tn = d6 // 4
    return pl.pallas_call(
        _ada_kernel,
        grid=(depth, d6 // tn),
        in_specs=[pl.BlockSpec((bc, d), lambda l, j: (0, 0)),
                  pl.BlockSpec((None, d, tn), lambda l, j: (l, 0, j)),
                  pl.BlockSpec((None, 1, tn), lambda l, j: (l, 0, j))],
        out_specs=pl.BlockSpec((None, bc, tn), lambda l, j: (l, 0, j)),
        out_shape=jax.ShapeDtypeStruct((depth, bc, d6), F32),
        compiler_params=_cparams(2, VMEM_LIMIT),
        name="ada_mods",
    )(c_all, w_ada, b_ada.reshape(depth, 1, d6))


def _mod_specs(cfg, col):
    d = cfg["d"]
    n_pt, tps, b = cfg["n_pt"], cfg["tps"], cfg["b"]
    return [pl.BlockSpec((1, 1, d), lambda i: (jnp.minimum(i // tps, b - 1), 0, col)),
            pl.BlockSpec((GP, 1, d), lambda i: (jnp.maximum(i - n_pt, 0), 0, col))]


def _rmsnorm_mod(x, g, sc, sh):
    inv = lax.rsqrt(jnp.mean(x * x, axis=-1, keepdims=True) + EPS)
    return x * inv * g * (1.0 + sc) + sh


def _inproj_kernel(cfg, x_ref, shp_ref, shs_ref, scp_ref, scs_ref, g_ref, w_ref, qg_ref, kg_ref,
                   up_ref, us_ref, qkv0_ref, qkv1_ref, qkv2_ref, qs_ref, kvs_ref,
                   kvp0_ref, kvp1_ref, kvp2_ref, res_ref):
    qkv_refs = (qkv0_ref, qkv1_ref, qkv2_ref)
    kvp_refs = (kvp0_ref, kvp1_ref, kvp2_ref)
    i = pl.program_id(0)
    n_pt, tps, d = cfg["n_pt"], cfg["tps"], cfg["d"]
    is_p = i < n_pt
    sh = jnp.where(is_p, shp_ref[...], shs_ref[...])
    sc = jnp.where(is_p, scp_ref[...], scs_ref[...])
    h = _rmsnorm_mod(x_ref[...], g_ref[...], sc, sh)
    z = jnp.dot(h.reshape(TM, d).astype(BF16), w_ref[...], preferred_element_type=F32)

    r = lax.broadcasted_iota(jnp.int32, (GROUP_DIM, GROUP_DIM), 0) // HEAD_DIM
    c = lax.broadcasted_iota(jnp.int32, (GROUP_DIM, GROUP_DIM), 1) // HEAD_DIM
    head_ones = (r == c).astype(BF16)

    def head_norm(t, gain):
        ss = jnp.dot((t * t).astype(BF16), head_ones, preferred_element_type=F32) * (1.0 / HEAD_DIM)
        return t * lax.rsqrt(ss + EPS) * gain

    u = z[:, :POOL_DIM]
    att = d - POOL_DIM
    q, k, v = [], [], []
    for g in range(N_GROUPS):
        lo = POOL_DIM + g * GROUP_DIM
        q.append(head_norm(z[:, lo:lo + GROUP_DIM], qg_ref[...]) * (HEAD_DIM ** -0.5))
        k.append(head_norm(z[:, lo + att:lo + att + GROUP_DIM], kg_ref[...]))
        v.append(z[:, lo + 2 * att:lo + 2 * att + GROUP_DIM])

    @pl.when(is_p)
    def _():
        up_ref[...] = u
        for g, (_, dil) in enumerate(ATT_GROUPS):
            for which, val in enumerate((q[g], k[g], v[g])):
                if dil == 1:
                    qkv_refs[g][which, 0] = val.astype(BF16)
                else:
                    for half in range(GROUP_DIM // LANES):
                        res_ref[half] = val[:, half * LANES:(half + 1) * LANES]
                    for r in range(dil):
                        rows = [res_ref[half, pl.ds(r, TM // dil, stride=dil), :]
                                for half in range(GROUP_DIM // LANES)]
                        qkv_refs[g][which, r] = jnp.concatenate(rows, axis=1).astype(BF16)

    @pl.when(jnp.logical_not(is_p))
    def _():
        us_ref[...] = u
        for g in range(N_GROUPS):
            qs_ref[g] = q[g]
            kvs_ref[g, :, :GROUP_DIM] = k[g]
            kvs_ref[g, :, GROUP_DIM:] = v[g]

    j = jnp.minimum(i, n_pt - 1) % tps
    for g, (keep, nk) in enumerate(cfg["kv_keep"]):
        rows = min(keep, TM)

        @pl.when(jnp.logical_and(is_p, j >= tps - nk))
        def _(g=g, rows=rows):
            kvp_refs[g][:, :GROUP_DIM] = k[g][TM - rows:]
            kvp_refs[g][:, GROUP_DIM:] = v[g][TM - rows:]


def _inproj(cfg, x3, mp, ms, gain, w_in_b, qg, kg):
    d, n_pt, n_t, tps, b = cfg["d"], cfg["n_pt"], cfg["n_t"], cfg["tps"], cfg["b"]
    np_, ns = cfg["np"], cfg["ns"]
    pidx = lambda i: jnp.minimum(i, n_pt - 1)
    sidx = lambda i: jnp.maximum(i - n_pt, 0)

    kv_specs, kv_shapes = [], []
    for keep, nk in cfg["kv_keep"]:
        rows = min(keep, TM)

        def kv_idx(i, nk=nk):
            ip = pidx(i)
            return ((ip // tps) * nk + jnp.maximum(ip % tps - (tps - nk), 0), 0)

        kv_specs.append(pl.BlockSpec((rows, 2 * GROUP_DIM), kv_idx))
        kv_shapes.append(jax.ShapeDtypeStruct((b * keep, 2 * GROUP_DIM), F32))

    qkv_specs, qkv_shapes = [], []
    for _, dil in ATT_GROUPS:
        qkv_specs.append(pl.BlockSpec((3, None, dil, TM // dil, GROUP_DIM),
                                      lambda i: (0, pidx(i) // tps, 0, pidx(i) % tps, 0)))
        qkv_shapes.append(jax.ShapeDtypeStruct((3, b, dil, cfg["s"] // dil, GROUP_DIM), BF16))

    return pl.pallas_call(
        functools.partial(_inproj_kernel, cfg),
        grid=(n_t,),
        in_specs=[pl.BlockSpec((GP, SUBLANES, d), lambda i: (i, 0, 0))]
        + _mod_specs(cfg, 0) + _mod_specs(cfg, 1)
        + [pl.BlockSpec((1, d), lambda i: (0, 0)),
           pl.BlockSpec(w_in_b.shape, lambda i: (0, 0)),
           pl.BlockSpec((1, GROUP_DIM), lambda i: (0, 0)),
           pl.BlockSpec((1, GROUP_DIM), lambda i: (0, 0))],
        out_specs=[pl.BlockSpec((TM, POOL_DIM), lambda i: (pidx(i), 0)),
                   pl.BlockSpec((TM, POOL_DIM), lambda i: (sidx(i), 0))] + qkv_specs
        + [pl.BlockSpec((N_GROUPS, TM, GROUP_DIM), lambda i: (0, sidx(i), 0)),
           pl.BlockSpec((N_GROUPS, TM, 2 * GROUP_DIM), lambda i: (0, sidx(i), 0))] + kv_specs,
        out_shape=[jax.ShapeDtypeStruct((np_, POOL_DIM), F32),
                   jax.ShapeDtypeStruct((ns, POOL_DIM), F32)] + qkv_shapes
        + [jax.ShapeDtypeStruct((N_GROUPS, ns, GROUP_DIM), F32),
           jax.ShapeDtypeStruct((N_GROUPS, ns, 2 * GROUP_DIM), F32)] + kv_shapes,
        scratch_shapes=[pltpu.VMEM((GROUP_DIM // LANES, TM, LANES), F32)],
        compiler_params=_cparams(1, VMEM_LIMIT),
        name="inproj",
    )(x3, mp, ms, mp, ms, gain, w_in_b, qg, kg)


def _rel_bucket_np(dist):
    dist = np.asarray(dist, np.int32)
    max_exact = REL_BUCKETS // 2
    d_f = np.maximum(dist, 1).astype(np.float32)
    ratio = np.log(d_f / np.float32(max_exact)) / np.float32(math.log(REL_MAX_DIST / max_exact))
    large = max_exact + (ratio * np.float32(REL_BUCKETS - max_exact)).astype(np.int32)
    large = np.minimum(large, REL_BUCKETS - 1)
    return np.where(dist < max_exact, dist, large)


def _bias_lookup(rel_bias, g, bucket, ok):
    table = rel_bias[:, g * HEADS:(g + 1) * HEADS].astype(F32)
    vals = jnp.zeros((HEADS,) + bucket.shape, F32)
    for bkt in np.unique(bucket):
        vals = jnp.where((bucket == bkt)[None], table[bkt].reshape((HEADS,) + (1,) * bucket.ndim), vals)
    return jnp.where(ok[None], vals, NEG_INF)


def _prompt_bias_tables(rel_bias):
    qi = np.arange(BAND)[:, None]
    kj = np.arange(2 * BAND)[None, :]
    delta = qi + BAND - kj
    ok = (delta >= 0) & (delta <= BAND)
    ok_first = ok & (kj >= BAND)
    tables = []
    for g, (_, dil) in enumerate(ATT_GROUPS):
        bucket = _rel_bucket_np(np.maximum(delta, 0) * dil)
        both = jnp.stack([_bias_lookup(rel_bias, g, bucket, ok), _bias_lookup(rel_bias, g, bucket, ok_first)])
        tables.append(both.reshape(2, HEADS * BAND, 2 * BAND))
    return tables


def _sample_bias_tables(rel_bias, t_new):
    tables = []
    for g, (win, dil) in enumerate(ATT_GROUPS):
        lb = min(win, PAST_LEN)
        t = np.arange(t_new)[:, None]
        col = np.arange(lb + LANES)[None, :]
        delta = lb + t - col
        ok = (col < lb + t_new) & (delta >= 0) & (delta % dil == 0) & (delta // dil <= win // dil)
        bucket = _rel_bucket_np(np.maximum(delta, 0))
        tables.append(_bias_lookup(rel_bias, g, bucket, ok).reshape(HEADS * t_new, lb + LANES))
    return tables


def _stack_heads(q):
    lh = _lane_head()
    return jnp.concatenate([jnp.where(lh == h, q, jnp.zeros_like(q)) for h in range(HEADS)], axis=0)


def _softmax(s):
    m = jnp.max(s, axis=1, keepdims=True)
    p = jnp.exp(s - m)
    l = jnp.sum(p, axis=1, keepdims=True)
    return p.astype(BF16), l, m + jnp.log(l)


def _unstack_heads(o4, lse4, rows):
    lh = _lane_head()
    o = o4[0:rows]
    lse = jnp.broadcast_to(lse4[0:rows], (rows, GROUP_DIM))
    for h in range(1, HEADS):
        o = jnp.where(lh == h, o4[h * rows:(h + 1) * rows], o)
        lse = jnp.where(lh == h, lse4[h * rows:(h + 1) * rows], lse)
    return o, lse


_NT = (((1,), (1,)), ((), ()))


def _attn_prompt_kernel(nsub, q_ref, kc_ref, kp_ref, vc_ref, vp_ref, bias_ref, o_ref, lse_ref, kbuf, vbuf):
    i = pl.program_id(2)
    kbuf[0:BAND] = kp_ref[...]
    kbuf[BAND:] = kc_ref[...]
    vbuf[0:BAND] = vp_ref[...]
    vbuf[BAND:] = vc_ref[...]
    for s in range(nsub):
        q4 = _stack_heads(q_ref[s * BAND:(s + 1) * BAND])
        kc = kbuf[s * BAND:(s + 2) * BAND]
        vc = vbuf[s * BAND:(s + 2) * BAND]
        logits = lax.dot_general(q4, kc, _NT, preferred_element_type=F32)
        if s == 0:
            bias = bias_ref[jnp.where(i == 0, 1, 0)]
        else:
            bias = bias_ref[0]
        p, l, lse4 = _softmax(logits + bias)
        o4 = jnp.dot(p, vc, preferred_element_type=F32) / l
        o, lse = _unstack_heads(o4, lse4, BAND)
        o_ref[s * BAND:(s + 1) * BAND] = o
        lse_ref[s * BAND:(s + 1) * BAND] = lse


def _attn_prompt(cfg, g, qkv, bias):
    b, s_len = cfg["b"], cfg["s"]
    _, dil = ATT_GROUPS[g]
    l_len = s_len // dil
    tq = min(TM, l_len)
    nsub = tq // BAND
    nq = l_len // tq

    def cur(which):
        return pl.BlockSpec((None, None, None, tq, GROUP_DIM), lambda bi, r, i: (which, bi, r, i, 0))

    def prev(which):
        return pl.BlockSpec((None, None, None, BAND, GROUP_DIM),
                            lambda bi, r, i: (which, bi, r, jnp.maximum(i * nsub - 1, 0), 0))

    out = pl.BlockSpec((None, None, tq, GROUP_DIM), lambda bi, r, i: (bi, r, i, 0))
    return pl.pallas_call(
        functools.partial(_attn_prompt_kernel, nsub),
        grid=(b, dil, nq),
        in_specs=[cur(0), cur(1), prev(1), cur(2), prev(2),
                  pl.BlockSpec(bias.shape, lambda bi, r, i: (0, 0, 0))],
        out_specs=[out, out],
        out_shape=[jax.ShapeDtypeStruct((b, dil, l_len, GROUP_DIM), F32)] * 2,
        scratch_shapes=[pltpu.VMEM((tq + BAND, GROUP_DIM), BF16)] * 2,
        compiler_params=_cparams(3, VMEM_LIMIT),
        name=f"attn_prompt_g{g}",
    )(qkv, qkv, qkv, qkv, qkv, bias)


def _attn_sample_kernel(bs, lb, t_new, q_ref, kvn_ref, cache_ref, bias_ref, o_ref, lse_ref):
    pad = jnp.zeros((LANES - t_new, GROUP_DIM), BF16)
    for j in range(bs):
        kt = cache_ref[j, 0:GROUP_DIM, :].astype(BF16)
        vt = cache_ref[j, GROUP_DIM:2 * GROUP_DIM, :].astype(BF16)
        kvn = kvn_ref[j].astype(BF16)
        kn = jnp.concatenate([kvn[:, :GROUP_DIM], pad], axis=0)
        vn = jnp.concatenate([kvn[:, GROUP_DIM:], pad], axis=0)
        q4 = _stack_heads(q_ref[j].astype(BF16))
        logits = jnp.concatenate(
            [jnp.dot(q4, kt, preferred_element_type=F32),
             lax.dot_general(q4, kn, _NT, preferred_element_type=F32)], axis=1)
        p, l, lse4 = _softmax(logits + bias_ref[...])
        o4 = (lax.dot_general(p[:, :lb], vt, _NT, preferred_element_type=F32)
              + jnp.dot(p[:, lb:], vn, preferred_element_type=F32)) / l
        o, lse = _unstack_heads(o4, lse4, t_new)
        o_ref[j] = o
        lse_ref[j] = lse


def _attn_sample(cfg, g, layer, qs, kvs, cache_t, bias):
    bd, t_new = cfg["bd"], cfg["t"]
    lb = cache_t.shape[-1]
    bs = max(1, min(8, (4 * 1024 * 1024) // (2 * GROUP_DIM * lb * 4)))
    out = pl.BlockSpec((bs, t_new, GROUP_DIM), lambda i: (i, 0, 0))
    o, lse = pl.pallas_call(
        functools.partial(_attn_sample_kernel, bs, lb, t_new),
        grid=(bd // bs,),
        in_specs=[pl.BlockSpec((None, bs, t_new, GROUP_DIM), lambda i: (g, i, 0, 0)),
                  pl.BlockSpec((None, bs, t_new, 2 * GROUP_DIM), lambda i: (g, i, 0, 0)),
                  pl.BlockSpec((None, bs, 2 * GROUP_DIM, lb), lambda i: (layer, i, 0, 0)),
                  pl.BlockSpec(bias.shape, lambda i: (0, 0))],
        out_specs=[out, out],
        out_shape=[jax.ShapeDtypeStruct((bd, t_new, GROUP_DIM), F32)] * 2,
        compiler_params=_cparams(1, VMEM_LIMIT),
        name=f"attn_sample_g{g}",
    )(qs.reshape(N_GROUPS, bd, t_new, GROUP_DIM), kvs.reshape(N_GROUPS, bd, t_new, 2 * GROUP_DIM),
      cache_t, bias)
    return o.reshape(bd * t_new, GROUP_DIM), lse.reshape(bd * t_new, GROUP_DIM)


def _pool_mix(ext_ref, rows, pos, w_ref, scale_ref, o_ref):
    lo = POOL_PAD
    u = ext_ref[:, lo:lo + rows, :]
    acc = u
    sums = {}
    for j in range(1, max(POOL_WINDOWS)):
        acc = acc + ext_ref[:, lo - j:lo - j + rows, :]
        if j + 1 in POOL_WINDOWS:
            sums[j + 1] = acc
    lane_grp = lax.broadcasted_iota(jnp.int32, (1, 1, POOL_DIM), 2) // (POOL_DIM // len(POOL_WINDOWS))
    z = None
    for gi, w in enumerate(POOL_WINDOWS):
        cnt = jnp.minimum(w, pos + 1).astype(F32)
        zw = sums[w] / cnt - u
        z = zw if z is None else jnp.where(lane_grp == gi, zw, z)
    nb = u.shape[0]
    y = jnp.dot(z.reshape(nb * rows, POOL_DIM).astype(BF16), w_ref[...], preferred_element_type=F32)
    o_ref[...] = (y * scale_ref[...]).astype(BF16)


def _pool_prompt_kernel(u_ref, halo_ref, w_ref, scale_ref, o_ref, ext_ref):
    i = pl.program_id(1)
    halo = halo_ref[...]
    ext_ref[:, 0:POOL_PAD, :] = jnp.where(i == 0, jnp.zeros_like(halo), halo)
    ext_ref[:, POOL_PAD:, :] = u_ref[...]
    pos = i * TM + lax.broadcasted_iota(jnp.int32, (1, TM, 1), 1)
    _pool_mix(ext_ref, TM, pos, w_ref, scale_ref, o_ref)


def _pool_sample_kernel(t_new, u_ref, hist_ref, w_ref, scale_ref, o_ref, ext_ref):
    nb = u_ref.shape[0]
    ext_ref[:, 0:POOL_PAD - POOL_HIST, :] = jnp.zeros((nb, POOL_PAD - POOL_HIST, POOL_DIM), F32)
    ext_ref[:, POOL_PAD - POOL_HIST:POOL_PAD, :] = hist_ref[...]
    ext_ref[:, POOL_PAD:, :] = u_ref[...]
    pos = PAST_LEN + lax.broadcasted_iota(jnp.int32, (1, t_new, 1), 1)
    _pool_mix(ext_ref, t_new, pos, w_ref, scale_ref, o_ref)


def _pool_prompt(cfg, u_p, w_bd, scale):
    b, s_len = cfg["b"], cfg["s"]
    u3 = u_p.reshape(b, s_len, POOL_DIM)
    per = TM // POOL_PAD
    return pl.pallas_call(
        _pool_prompt_kernel,
        grid=(b, s_len // TM),
        in_specs=[pl.BlockSpec((1, TM, POOL_DIM), lambda bi, i: (bi, i, 0)),
                  pl.BlockSpec((1, POOL_PAD, POOL_DIM), lambda bi, i: (bi, jnp.maximum(i * per - 1, 0), 0)),
                  pl.BlockSpec((POOL_DIM, POOL_DIM), lambda bi, i: (0, 0)),
                  pl.BlockSpec((1, POOL_DIM), lambda bi, i: (0, 0))],
        out_specs=pl.BlockSpec((TM, POOL_DIM), lambda bi, i: (bi * (s_len // TM) + i, 0)),
        out_shape=jax.ShapeDtypeStruct((b * s_len, POOL_DIM), BF16),
        scratch_shapes=[pltpu.VMEM((1, TM + POOL_PAD, POOL_DIM), F32)],
        compiler_params=_cparams(2, VMEM_LIMIT),
        name="pool_prompt",
    )(u3, u3, w_bd, scale)


def _pool_sample(cfg, layer, u_s, state_pool, w_bd, scale):
    bd, t_new = cfg["bd"], cfg["t"]
    u3 = u_s.reshape(bd, t_new, POOL_DIM)
    nb = GP
    return pl.pallas_call(
        functools.partial(_pool_sample_kernel, t_new),
        grid=(bd // nb,),
        in_specs=[pl.BlockSpec((nb, t_new, POOL_DIM), lambda i: (i, 0, 0)),
                  pl.BlockSpec((None, nb, POOL_HIST, POOL_DIM), lambda i: (layer, i, 0, 0)),
                  pl.BlockSpec((POOL_DIM, POOL_DIM), lambda i: (0, 0)),
                  pl.BlockSpec((1, POOL_DIM), lambda i: (0, 0))],
        out_specs=pl.BlockSpec((nb * t_new, POOL_DIM), lambda i: (i, 0)),
        out_shape=jax.ShapeDtypeStruct((bd * t_new, POOL_DIM), BF16),
        scratch_shapes=[pltpu.VMEM((nb, t_new + POOL_PAD, POOL_DIM), F32)],
        compiler_params=_cparams(1, VMEM_LIMIT),
        name="pool_sample",
    )(u3, state_pool, w_bd, scale)


def _outproj_kernel(cfg, x_ref, g1p_ref, g1s_ref, shp_ref, shs_ref, scp_ref, scs_ref, pp_ref, ps_ref, *rest):
    att_refs = rest[:4 * N_GROUPS]
    (gain_ref, wo_ref, wr_ref, br_ref, x1_ref, h2_ref, route_ref, cnt_ref,
     run_ref, nat_ref) = rest[4 * N_GROUPS:]
    i = pl.program_id(0)
    n_pt, d = cfg["n_pt"], cfg["d"]
    is_p = i < n_pt

    halves = GROUP_DIM // LANES

    def put(slot, val, rows=None):
        for half in range(halves):
            piece = val[:, half * LANES:(half + 1) * LANES]
            if rows is None:
                nat_ref[slot, half] = piece
            else:
                nat_ref[slot, half, rows, :] = piece

    @pl.when(is_p)
    def _():
        for g, (_, dil) in enumerate(ATT_GROUPS):
            for which in range(2):
                src = att_refs[4 * g + which]
                if dil == 1:
                    put(2 * g + which, src[0])
                else:
                    for r in range(dil):
                        put(2 * g + which, src[r], pl.ds(r, TM // dil, stride=dil))

    @pl.when(jnp.logical_not(is_p))
    def _():
        for g in range(N_GROUPS):
            for which in range(2):
                put(2 * g + which, att_refs[4 * g + 2 + which][...])

    def get(slot):
        return jnp.concatenate([nat_ref[slot, half] for half in range(halves)], axis=1)

    o = [get(2 * g) for g in range(N_GROUPS)]
    lse = [get(2 * g + 1) for g in range(N_GROUPS)]
    mx = functools.reduce(jnp.maximum, lse)
    e = [jnp.exp(l - mx) for l in lse]
    attn = sum(eg * og for eg, og in zip(e, o)) / sum(e)
    pool = jnp.where(is_p, pp_ref[...], ps_ref[...])
    cat = jnp.concatenate([pool, attn.astype(BF16)], axis=1)
    y = jnp.dot(cat, wo_ref[...], preferred_element_type=F32)

    g1 = jnp.where(is_p, g1p_ref[...], g1s_ref[...])
    x1 = x_ref[...] + g1 * y.reshape(GP, SUBLANES, d)
    x1_ref[...] = x1
    sh = jnp.where(is_p, shp_ref[...], shs_ref[...])
    sc = jnp.where(is_p, scp_ref[...], scs_ref[...])
    h2 = _rmsnorm_mod(x1, gain_ref[...], sc, sh).reshape(TM, d)
    _rows_to_slabs(h2_ref, h2)

    h_hi = h2.astype(BF16)
    h_lo = (h2 - h_hi.astype(F32)).astype(BF16)
    wr = wr_ref[...]
    w_hi = wr.astype(BF16)
    w_lo = (wr - w_hi.astype(F32)).astype(BF16)
    logits = (jnp.dot(h_hi, w_hi, preferred_element_type=F32)
              + jnp.dot(h_lo, w_hi, preferred_element_type=F32)
              + jnp.dot(h_hi, w_lo, preferred_element_type=F32)) + br_ref[...]

    lane = lax.broadcasted_iota(jnp.int32, (TM, LANES), 1).astype(F32)
    vals = logits
    top_v, top_i = [], []
    onehot = jnp.zeros((TM, LANES), F32)
    for _ in range(TOP_K):
        m = jnp.max(vals, axis=1, keepdims=True)
        idx = jnp.min(jnp.where(vals == m, lane, float(LANES)), axis=1, keepdims=True)
        hit = lane == idx
        vals = jnp.where(hit, -jnp.inf, vals)
        onehot = jnp.where(hit, 1.0, onehot)
        top_v.append(m)
        top_i.append(idx)
    ev = [jnp.exp(v - top_v[0]) for v in top_v]
    den = sum(ev)

    @pl.when(i == 0)
    def _():
        run_ref[...] = jnp.zeros_like(run_ref)

    row = lax.broadcasted_iota(jnp.int32, (TM, TM), 0)
    col = lax.broadcasted_iota(jnp.int32, (TM, TM), 1)
    before = (col < row).astype(BF16)
    rank_all = jnp.dot(before, onehot.astype(BF16), preferred_element_type=F32) + run_ref[...]
    route = jnp.zeros((TM, LANES), F32)
    for k in range(TOP_K):
        rank_k = jnp.sum(jnp.where(lane == top_i[k], rank_all, 0.0), axis=1, keepdims=True)
        route = jnp.where(lane == float(k), top_i[k], route)
        route = jnp.where(lane == float(TOP_K + k), ev[k] / den, route)
        route = jnp.where(lane == float(2 * TOP_K + k), rank_k, route)
    route_ref[...] = route
    run_ref[...] = run_ref[...] + jnp.sum(onehot, axis=0, keepdims=True)
    cnt_ref[...] = run_ref[...]


def _outproj(cfg, x3, mp, ms, pool_p, pool_s, att, gain, w_out_b, w_r, b_r):
    d, n_pt, n_t = cfg["d"], cfg["n_pt"], cfg["n_t"]
    n = cfg["n"]
    tps = cfg["tps"]
    pidx = lambda i: (jnp.minimum(i, n_pt - 1), 0)
    sidx = lambda i: (jnp.maximum(i - n_pt, 0), 0)
    pspec = pl.BlockSpec((TM, GROUP_DIM), pidx)
    sspec = pl.BlockSpec((TM, GROUP_DIM), sidx)
    att_specs, att_args = [], []
    for (_, dil), (o_p, lse_p, o_s, lse_s) in zip(ATT_GROUPS, att):
        rspec = pl.BlockSpec((None, dil, TM // dil, GROUP_DIM),
                             lambda i: (pidx(i)[0] // tps, 0, pidx(i)[0] % tps, 0))
        att_specs += [rspec, rspec, sspec, sspec]
        att_args += [o_p, lse_p, o_s, lse_s]
    full = lambda a: pl.BlockSpec(a.shape, lambda i: (0,) * a.ndim)
    return pl.pallas_call(
        functools.partial(_outproj_kernel, cfg),
        grid=(n_t,),
        in_specs=[pl.BlockSpec((GP, SUBLANES, d), lambda i: (i, 0, 0))]
        + _mod_specs(cfg, 2) + _mod_specs(cfg, 3) + _mod_specs(cfg, 4)
        + [pspec, sspec] + att_specs + [full(gain), full(w_out_b), full(w_r), full(b_r)],
        out_specs=[pl.BlockSpec((GP, SUBLANES, d), lambda i: (i, 0, 0)),
                   pl.BlockSpec((TM * SUBLANES, LANES), lambda i: (i, 0)),
                   pl.BlockSpec((TM, LANES), lambda i: (i, 0)),
                   pl.BlockSpec((1, LANES), lambda i: (0, 0))],
        out_shape=[jax.ShapeDtypeStruct(x3.shape, F32),
                   jax.ShapeDtypeStruct((n * SUBLANES, LANES), F32),
                   jax.ShapeDtypeStruct((n, LANES), F32),
                   jax.ShapeDtypeStruct((1, LANES), F32)],
        scratch_shapes=[pltpu.VMEM((1, LANES), F32),
                        pltpu.VMEM((2 * N_GROUPS, GROUP_DIM // LANES, TM, LANES), F32)],
        compiler_params=_cparams(1, VMEM_LIMIT),
        name="outproj_router",
    )(x3, mp, ms, mp, ms, mp, ms, pool_p, pool_s, *att_args, gain, w_out_b, w_r, b_r)


def _rows_from_slabs(ref):
    rows = ref.shape[0] // SUBLANES
    return jnp.concatenate([ref[pl.ds(j, rows, stride=SUBLANES), :] for j in range(SUBLANES)], axis=1)


def _rows_to_slabs(ref, val):
    for j in range(SUBLANES):
        ref[pl.ds(j, val.shape[0], stride=SUBLANES), :] = val[:, j * LANES:(j + 1) * LANES]


def _moe_kernel(te_ref, src0_ref, nxt_ref, prv_ref, last_ref, h_hbm, w1_ref, b1_ref, w2_ref, b2_ref, y_hbm,
                x0, x1, y0, y1, w1b, w2b, gsem, ssem):
    i = pl.program_id(0)
    n = pl.num_programs(0)
    ff = w2_ref.shape[0]
    xs, ys = (x0, x1), (y0, y1)

    def start_gather(idx_ref, p):
        for r in range(MOE_TM):
            pltpu.make_async_copy(h_hbm.at[idx_ref[0, r]], xs[p].at[pl.ds(r * SUBLANES, SUBLANES)],
                                  gsem.at[p]).start(priority=r % 2)

    def start_scatter(idx_ref, p):
        for r in range(MOE_TM):
            pltpu.make_async_copy(ys[p].at[pl.ds(r * SUBLANES, SUBLANES)], y_hbm.at[idx_ref[0, r]],
                                  ssem.at[p]).start(priority=r % 2)

    def wait_gather(p):
        pltpu.make_async_copy(xs[p], xs[p], gsem.at[p]).wait()

    def wait_scatter(p):
        pltpu.make_async_copy(ys[p], ys[p], ssem.at[p]).wait()

    @pl.when(i == 0)
    def _():
        start_gather(src0_ref, 0)
        n_asg = y_hbm.shape[0] - 2 * MOE_TM
        for p in range(2):
            ys[p][...] = jnp.zeros_like(ys[p])
            for r in range(MOE_TM):
                pltpu.make_async_copy(ys[p].at[pl.ds(r * SUBLANES, SUBLANES)],
                                      y_hbm.at[n_asg + p * MOE_TM + r], ssem.at[p]).start()
            wait_scatter(p)

    @pl.when(jnp.logical_or(i == 0, te_ref[i] != te_ref[jnp.maximum(i - 1, 0)]))
    def _():
        w1b[...] = w1_ref[...].astype(BF16)
        w2b[...] = w2_ref[...].astype(BF16)

    def step(p):
        wait_gather(p)
        start_gather(nxt_ref, 1 - p)
        start_scatter(prv_ref, 1 - p)
        x = _rows_from_slabs(xs[p]).astype(BF16)
        h1 = jnp.dot(x, w1b[...], preferred_element_type=F32) + b1_ref[...]
        gate = jnp.minimum(h1[:, :ff], SWIGLU_LIMIT)
        up = jnp.clip(h1[:, ff:], -SWIGLU_LIMIT, SWIGLU_LIMIT)
        act = gate * jax.nn.sigmoid(SWIGLU_ALPHA * gate) * (up + 1.0)
        _rows_to_slabs(ys[p], jnp.dot(act.astype(BF16), w2b[...], preferred_element_type=F32) + b2_ref[...])
        wait_scatter(1 - p)

        @pl.when(i == n - 1)
        def _():
            start_scatter(last_ref, p)
            wait_scatter(p)
            wait_gather(1 - p)

    for p in range(2):
        pl.when(i % 2 == p)(functools.partial(step, p))


def _moe(layer, n_tiles, tile_expert, slot_src, slot_dst, h2, w1, b1, w2, b2, n_rows_out):
    depth, n_exp, d, ff2 = w1.shape
    ff = w2.shape[2]
    smem_tile = lambda f: pl.BlockSpec((None, 1, MOE_TM), f, memory_space=pltpu.SMEM)
    grid_spec = pltpu.PrefetchScalarGridSpec(
        num_scalar_prefetch=1,
        grid=(n_tiles,),
        in_specs=[smem_tile(lambda i, te: (0, 0, 0)),
                  smem_tile(lambda i, te: (jnp.minimum(i + 1, n_tiles - 1), 0, 0)),
                  smem_tile(lambda i, te: (jnp.maximum(i - 1, 0), 0, 0)),
                  smem_tile(lambda i, te: (n_tiles - 1, 0, 0)),
                  pl.BlockSpec(memory_space=pl.ANY),
                  pl.BlockSpec((None, None, d, ff2), lambda i, te: (layer, te[i], 0, 0)),
                  pl.BlockSpec((None, None, 1, ff2), lambda i, te: (layer, te[i], 0, 0)),
                  pl.BlockSpec((None, None, ff, d), lambda i, te: (layer, te[i], 0, 0)),
                  pl.BlockSpec((None, None, 1, d), lambda i, te: (layer, te[i], 0, 0))],
        out_specs=pl.BlockSpec(memory_space=pl.ANY),
        scratch_shapes=[pltpu.VMEM((MOE_TM * SUBLANES, LANES), F32)] * 4
        + [pltpu.VMEM((d, ff2), BF16),
           pltpu.VMEM((ff, d), BF16),
           pltpu.SemaphoreType.DMA((2,)),
           pltpu.SemaphoreType.DMA((2,))],
    )
    src3 = slot_src.reshape(n_tiles, 1, MOE_TM)
    dst3 = slot_dst.reshape(n_tiles, 1, MOE_TM)
    return pl.pallas_call(
        _moe_kernel,
        grid_spec=grid_spec,
        out_shape=jax.ShapeDtypeStruct((n_rows_out, SUBLANES, LANES), F32),
        compiler_params=_cparams(1, VMEM_LIMIT),
        name="moe_experts",
    )(tile_expert, src3, src3, dst3, dst3, h2,
      w1, b1.reshape(depth, n_exp, 1, ff2), w2, b2.reshape(depth, n_exp, 1, d))


def _moe_plan(cfg, route, counts):
    n = cfg["n"]
    m = n * TOP_K
    n_tiles = m // MOE_TM + N_EXPERTS
    m_pad = n_tiles * MOE_TM
    top_i = route[:, 0:TOP_K].astype(jnp.int32)
    rank = route[:, 2 * TOP_K:3 * TOP_K].astype(jnp.int32)
    cnt = counts[0, :N_EXPERTS].astype(jnp.int32)
    tiles_e = (cnt + MOE_TM - 1) // MOE_TM
    tile_end = jnp.cumsum(tiles_e)
    pstart = (tile_end - tiles_e) * MOE_TM
    expert = jnp.arange(N_EXPERTS, dtype=jnp.int32)
    pos = jnp.sum(jnp.where(top_i[..., None] == expert, pstart, 0), axis=-1) + rank
    t = jnp.arange(n_tiles, dtype=jnp.int32)
    te = jnp.sum((tile_end[None, :] <= t[:, None]).astype(jnp.int32), axis=1)
    last_used = jnp.sum((tile_end <= tile_end[-1] - 1).astype(jnp.int32))
    te = jnp.minimum(te, last_used).astype(jnp.int32)
    s = jnp.arange(m_pad, dtype=jnp.int32)
    dump = m + ((s // MOE_TM) % 2) * MOE_TM + s % MOE_TM
    slot_dst = _invert_slots(pos.T.reshape(-1), dump)
    slot_src = jnp.where(slot_dst < m, slot_dst % n, 0)
    return n_tiles, te, slot_src, slot_dst, m + 2 * MOE_TM


SLOT_CHUNK = 2048


def _invert_kernel(pos_ref, init_ref, out_ref):
    base = pl.program_id(0) * SLOT_CHUNK

    @pl.when(base == 0)
    def _():
        def fill(s, carry):
            out_ref[s] = init_ref[s]
            return carry

        lax.fori_loop(0, out_ref.shape[0], fill, 0, unroll=8)

    def body(j, carry):
        out_ref[pos_ref[0, j]] = base + j
        return carry

    lax.fori_loop(0, SLOT_CHUNK, body, 0, unroll=8)


def _invert_slots(pos_flat, init):
    m = pos_flat.shape[0]
    assert m % SLOT_CHUNK == 0
    return pl.pallas_call(
        _invert_kernel,
        grid=(m // SLOT_CHUNK,),
        in_specs=[pl.BlockSpec((None, 1, SLOT_CHUNK), lambda c: (c, 0, 0), memory_space=pltpu.SMEM),
                  pl.BlockSpec(memory_space=pltpu.SMEM)],
        out_specs=pl.BlockSpec(memory_space=pltpu.SMEM),
        out_shape=jax.ShapeDtypeStruct(init.shape, jnp.int32),
        compiler_params=_cparams(1),
        name="invert_slots",
    )(pos_flat.reshape(m // SLOT_CHUNK, 1, SLOT_CHUNK), init)


def _combine_kernel(cfg, x_ref, g2p_ref, g2s_ref, route_ref, *rest):
    y_refs, o_ref = rest[:TOP_K], rest[TOP_K]
    i = pl.program_id(0)
    is_p = i < cfg["n_pt"]
    route = route_ref[...]
    acc = route[:, TOP_K:TOP_K + 1] * _rows_from_slabs(y_refs[0])
    for k in range(1, TOP_K):
        acc = acc + route[:, TOP_K + k:TOP_K + k + 1] * _rows_from_slabs(y_refs[k])
    g2 = jnp.where(is_p, g2p_ref[...], g2s_ref[...])
    o_ref[...] = x_ref[...] + g2 * acc.reshape(GP, SUBLANES, cfg["d"])


def _combine(cfg, x1, mp, ms, route, ybuf):
    d, n_t = cfg["d"], cfg["n_t"]
    y_specs = [pl.BlockSpec((TM * SUBLANES, LANES), lambda i, k=k: (k * n_t + i, 0)) for k in range(TOP_K)]
    ybuf = ybuf.reshape(ybuf.shape[0] * SUBLANES, LANES)
    return pl.pallas_call(
        functools.partial(_combine_kernel, cfg),
        grid=(n_t,),
        in_specs=[pl.BlockSpec((GP, SUBLANES, d), lambda i: (i, 0, 0))] + _mod_specs(cfg, 5)
        + [pl.BlockSpec((TM, LANES), lambda i: (i, 0))] + y_specs,
        out_specs=pl.BlockSpec((GP, SUBLANES, d), lambda i: (i, 0, 0)),
        out_shape=jax.ShapeDtypeStruct(x1.shape, F32),
        compiler_params=_cparams(1, VMEM_LIMIT),
        name="moe_combine",
    )(x1, mp, ms, route, *([ybuf] * TOP_K))


def kernel(x_prompt, x_sample, cache_kv_w128_d1, cache_kv_w512_d4, cache_kv_w2048_d16, state_pool,
           c_prompt, c_sample, rel_bias, norm_mix, norm_ffn, w_ada, b_ada, w_in, q_norm, k_norm,
           pool_w, pool_scale, w_out, w_router, b_router, w_expert_in, b_expert_in,
           w_expert_out, b_expert_out):
    b, s_len, d = x_prompt.shape
    bd, t_new, _ = x_sample.shape
    depth = w_in.shape[0]
    np_, ns = b * s_len, bd * t_new
    assert t_new == SUBLANES and s_len % TM == 0 and ns % TM == 0
    assert all((s_len // dil) % BAND == 0 for _, dil in ATT_GROUPS)
    cfg = dict(b=b, s=s_len, d=d, bd=bd, t=t_new, np=np_, ns=ns, n=np_ + ns,
               n_pt=np_ // TM, n_t=(np_ + ns) // TM, tps=s_len // TM)
    kv_keep = []
    for win, _ in ATT_GROUPS:
        keep = min(win, s_len)
        assert keep % TM == 0 or TM % keep == 0
        kv_keep.append((keep, max(keep // TM, 1)))
    cfg["kv_keep"] = tuple(kv_keep)
    caches_t = [jnp.transpose(c, (0, 1, 3, 4, 5, 2)).reshape(depth, bd, 2 * GROUP_DIM, c.shape[2])
                for c in (cache_kv_w128_d1, cache_kv_w512_d4, cache_kv_w2048_d16)]

    mods = _ada_mods(jnp.concatenate([c_prompt, c_sample], axis=0), w_ada, b_ada)
    bias_p = _prompt_bias_tables(rel_bias)
    bias_s = _sample_bias_tables(rel_bias, t_new)
    w_in_b = w_in.astype(BF16)
    w_out_b = w_out.astype(BF16)
    eye = jnp.eye(len(POOL_WINDOWS), dtype=F32)
    w_pool_bd = (eye[None, :, None, :, None] * pool_w[:, :, :, None, :]).reshape(depth, POOL_DIM, POOL_DIM)
    w_pool_bd = w_pool_bd.astype(BF16)
    w_r_pad = jnp.pad(w_router, ((0, 0), (0, 0), (0, LANES - N_EXPERTS)))
    b_r_pad = jnp.pad(b_router, ((0, 0), (0, LANES - N_EXPERTS)), constant_values=NEG_INF)

    x3 = jnp.concatenate([x_prompt.reshape(np_, d), x_sample.reshape(ns, d)], axis=0)
    x3 = x3.reshape((np_ + ns) // SUBLANES, SUBLANES, d)

    kv_p = [[] for _ in ATT_GROUPS]
    kv_s = [[] for _ in ATT_GROUPS]
    pool_p_state, pool_s_state = [], []
    for l in range(depth):
        mp = mods[l, :b].reshape(b, 1, 6 * d)
        ms = mods[l, b:].reshape(bd, 1, 6 * d)
        qg = jnp.tile(q_norm[l], HEADS).reshape(1, GROUP_DIM)
        kg = jnp.tile(k_norm[l], HEADS).reshape(1, GROUP_DIM)
        (u_p, u_s, qkv0, qkv1, qkv2, qs, kvs, *kvp) = _inproj(
            cfg, x3, mp, ms, norm_mix[l].reshape(1, d), w_in_b[l], qg, kg)

        att = []
        for g, qkv in enumerate((qkv0, qkv1, qkv2)):
            o_p, lse_p = _attn_prompt(cfg, g, qkv, bias_p[g])
            o_s, lse_s = _attn_sample(cfg, g, l, qs, kvs, caches_t[g], bias_s[g])
            att.append((o_p, lse_p, o_s, lse_s))
            keep = cfg["kv_keep"][g][0]
            kv_p[g].append(kvp[g].reshape(b, keep, 2, HEADS, HEAD_DIM))
            kv_s[g].append(kvs[g].reshape(bd, t_new, 2, HEADS, HEAD_DIM))

        scale = pool_scale[l].reshape(1, POOL_DIM)
        pool_p = _pool_prompt(cfg, u_p, w_pool_bd[l], scale)
        pool_s = _pool_sample(cfg, l, u_s, state_pool, w_pool_bd[l], scale)
        pool_p_state.append(u_p.reshape(b, s_len, POOL_DIM)[:, s_len - POOL_HIST:])
        pool_s_state.append(jnp.concatenate(
            [state_pool[l], u_s.reshape(bd, t_new, POOL_DIM)], axis=1)[:, -POOL_HIST:])

        x1, h2, route, counts = _outproj(cfg, x3, mp, ms, pool_p, pool_s, att,
                                         norm_ffn[l].reshape(1, d), w_out_b[l], w_r_pad[l],
                                         b_r_pad[l].reshape(1, LANES))
        n_tiles, te, slot_src, slot_dst, n_rows_out = _moe_plan(cfg, route, counts)
        ybuf = _moe(l, n_tiles, te, slot_src, slot_dst, h2.reshape(cfg["n"], SUBLANES, LANES),
                    w_expert_in, b_expert_in, w_expert_out, b_expert_out, n_rows_out)
        x3 = _combine(cfg, x1, mp, ms, route, ybuf)

    x_all = x3.reshape(np_ + ns, d)
    y_prompt = x_all[:np_].reshape(b, s_len, d)
    y_sample = x_all[np_:].reshape(bd, t_new, d)
    return (y_prompt, y_sample,
            jnp.stack(kv_p[0]), jnp.stack(kv_p[1]), jnp.stack(kv_p[2]), jnp.stack(pool_p_state),
            jnp.stack(kv_s[0]), jnp.stack(kv_s[1]), jnp.stack(kv_s[2]), jnp.stack(pool_s_state))
```

```python
import functools
import math

import numpy as np
import jax
import jax.numpy as jnp
from jax import lax
from jax.experimental import pallas as pl
from jax.experimental.pallas import tpu as pltpu

F32 = jnp.float32
BF16 = jnp.bfloat16

HEAD_DIM = 64
HEADS = 4
GROUP_DIM = HEADS * HEAD_DIM
ATT_GROUPS = ((128, 1), (512, 4), (2048, 16))
N_GROUPS = len(ATT_GROUPS)
BAND = 128
POOL_WINDOWS = (2, 4, 8, 16)
POOL_HIST = 15
POOL_DIM = 256
POOL_PAD = 16
REL_BUCKETS = 32
REL_MAX_DIST = 2048
N_EXPERTS = 32
TOP_K = 4
SWIGLU_LIMIT = 7.0
SWIGLU_ALPHA = 1.702
EPS = 1e-6
NEG_INF = -1e30
PAST_LEN = 2048

SUBLANES = 8
LANES = 128
TM = 512
GP = TM // SUBLANES
MOE_TM = 256
VMEM_LIMIT = 52 * 1024 * 1024


def _cparams(n_axes, vmem=None):
    return pltpu.CompilerParams(dimension_semantics=("arbitrary",) * n_axes,
                                vmem_limit_bytes=vmem)


def _lane_head(width=GROUP_DIM):
    return lax.broadcasted_iota(jnp.int32, (1, width), 1) // HEAD_DIM


def _ada_kernel(c_ref, w_ref, b_ref, o_ref):
    c = c_ref[...]
    a = (c * jax.nn.sigmoid(c)).astype(BF16)
    o_ref[...] = jnp.dot(a, w_ref[...].astype(BF16), preferred_element_type=F32) + b_ref[...]


def _ada_mods(c_all, w_ada, b_ada):
    depth, d, d6 = w_ada.shape
    bc = c_all.shape[0]
    tn = d6 // 4
    return pl.pallas_call(
        _ada_kernel,
        grid=(depth, d6 // tn),
        in_specs=[pl.BlockSpec((bc, d), lambda l, j: (0, 0)),
                  pl.BlockSpec((None, d, tn), lambda l, j: (l, 0, j)),
                  pl.BlockSpec((None, 1, tn), lambda l, j: (l, 0, j))],
        out_specs=pl.BlockSpec((None, bc, tn), lambda l, j: (l, 0, j)),
        out_shape=jax.ShapeDtypeStruct((depth, bc, d6), F32),
        compiler_params=_cparams(2, VMEM_LIMIT),
        name="ada_mods",
    )(c_all, w_ada, b_ada.reshape(depth, 1, d6))


def _mod_specs(cfg, col):
    d = cfg["d"]
    n_pt, tps, b = cfg["n_pt"], cfg["tps"], cfg["b"]
    return [pl.BlockSpec((1, 1, d), lambda i: (jnp.minimum(i // tps, b - 1), 0, col)),
            pl.BlockSpec((GP, 1, d), lambda i: (jnp.maximum(i - n_pt, 0), 0, col))]


def _rmsnorm_mod(x, g, sc, sh):
    inv = lax.rsqrt(jnp.mean(x * x, axis=-1, keepdims=True) + EPS)
    return x * inv * g * (1.0 + sc) + sh


def _inproj_kernel(cfg, x_ref, shp_ref, shs_ref, scp_ref, scs_ref, g_ref, w_ref, qg_ref, kg_ref,
                   up_ref, us_ref, qkv0_ref, qkv1_ref, qkv2_ref, qs_ref, kvs_ref,
                   kvp0_ref, kvp1_ref, kvp2_ref, res_ref):
    qkv_refs = (qkv0_ref, qkv1_ref, qkv2_ref)
    kvp_refs = (kvp0_ref, kvp1_ref, kvp2_ref)
    i = pl.program_id(0)
    n_pt, tps, d = cfg["n_pt"], cfg["tps"], cfg["d"]
    is_p = i < n_pt
    sh = jnp.where(is_p, shp_ref[...], shs_ref[...])
    sc = jnp.where(is_p, scp_ref[...], scs_ref[...])
    h = _rmsnorm_mod(x_ref[...], g_ref[...], sc, sh)
    z = jnp.dot(h.reshape(TM, d).astype(BF16), w_ref[...], preferred_element_type=F32)

    r = lax.broadcasted_iota(jnp.int32, (GROUP_DIM, GROUP_DIM), 0) // HEAD_DIM
    c = lax.broadcasted_iota(jnp.int32, (GROUP_DIM, GROUP_DIM), 1) // HEAD_DIM
    head_ones = (r == c).astype(BF16)

    def head_norm(t, gain):
        ss = jnp.dot((t * t).astype(BF16), head_ones, preferred_element_type=F32) * (1.0 / HEAD_DIM)
        return t * lax.rsqrt(ss + EPS) * gain

    u = z[:, :POOL_DIM]
    att = d - POOL_DIM
    q, k, v = [], [], []
    for g in range(N_GROUPS):
        lo = POOL_DIM + g * GROUP_DIM
        q.append(head_norm(z[:, lo:lo + GROUP_DIM], qg_ref[...]) * (HEAD_DIM ** -0.5))
        k.append(head_norm(z[:, lo + att:lo + att + GROUP_DIM], kg_ref[...]))
        v.append(z[:, lo + 2 * att:lo + 2 * att + GROUP_DIM])

    @pl.when(is_p)
    def _():
        up_ref[...] = u
        for g, (_, dil) in enumerate(ATT_GROUPS):
            for which, val in enumerate((q[g], k[g], v[g])):
                if dil == 1:
                    qkv_refs[g][which, 0] = val.astype(BF16)
                else:
                    for half in range(GROUP_DIM // LANES):
                        res_ref[half] = val[:, half * LANES:(half + 1) * LANES]
                    for r in range(dil):
                        rows = [res_ref[half, pl.ds(r, TM // dil, stride=dil), :]
                                for half in range(GROUP_DIM // LANES)]
                        qkv_refs[g][which, r] = jnp.concatenate(rows, axis=1).astype(BF16)

    @pl.when(jnp.logical_not(is_p))
    def _():
        us_ref[...] = u
        for g in range(N_GROUPS):
            qs_ref[g] = q[g]
            kvs_ref[g, :, :GROUP_DIM] = k[g]
            kvs_ref[g, :, GROUP_DIM:] = v[g]

    j = jnp.minimum(i, n_pt - 1) % tps
    for g, (keep, nk) in enumerate(cfg["kv_keep"]):
        rows = min(keep, TM)

        @pl.when(jnp.logical_and(is_p, j >= tps - nk))
        def _(g=g, rows=rows):
            kvp_refs[g][:, :GROUP_DIM] = k[g][TM - rows:]
            kvp_refs[g][:, GROUP_DIM:] = v[g][TM - rows:]


def _inproj(cfg, x3, mp, ms, gain, w_in_b, qg, kg):
    d, n_pt, n_t, tps, b = cfg["d"], cfg["n_pt"], cfg["n_t"], cfg["tps"], cfg["b"]
    np_, ns = cfg["np"], cfg["ns"]
    pidx = lambda i: jnp.minimum(i, n_pt - 1)
    sidx = lambda i: jnp.maximum(i - n_pt, 0)

    kv_specs, kv_shapes = [], []
    for keep, nk in cfg["kv_keep"]:
        rows = min(keep, TM)

        def kv_idx(i, nk=nk):
            ip = pidx(i)
            return ((ip // tps) * nk + jnp.maximum(ip % tps - (tps - nk), 0), 0)

        kv_specs.append(pl.BlockSpec((rows, 2 * GROUP_DIM), kv_idx))
        kv_shapes.append(jax.ShapeDtypeStruct((b * keep, 2 * GROUP_DIM), F32))

    qkv_specs, qkv_shapes = [], []
    for _, dil in ATT_GROUPS:
        qkv_specs.append(pl.BlockSpec((3, None, dil, TM // dil, GROUP_DIM),
                                      lambda i: (0, pidx(i) // tps, 0, pidx(i) % tps, 0)))
        qkv_shapes.append(jax.ShapeDtypeStruct((3, b, dil, cfg["s"] // dil, GROUP_DIM), BF16))

    return pl.pallas_call(
        functools.partial(_inproj_kernel, cfg),
        grid=(n_t,),
        in_specs=[pl.BlockSpec((GP, SUBLANES, d), lambda i: (i, 0, 0))]
        + _mod_specs(cfg, 0) + _mod_specs(cfg, 1)
        + [pl.BlockSpec((1, d), lambda i: (0, 0)),
           pl.BlockSpec(w_in_b.shape, lambda i: (0, 0)),
           pl.BlockSpec((1, GROUP_DIM), lambda i: (0, 0)),
           pl.BlockSpec((1, GROUP_DIM), lambda i: (0, 0))],
        out_specs=[pl.BlockSpec((TM, POOL_DIM), lambda i: (pidx(i), 0)),
                   pl.BlockSpec((TM, POOL_DIM), lambda i: (sidx(i), 0))] + qkv_specs
        + [pl.BlockSpec((N_GROUPS, TM, GROUP_DIM), lambda i: (0, sidx(i), 0)),
           pl.BlockSpec((N_GROUPS, TM, 2 * GROUP_DIM), lambda i: (0, sidx(i), 0))] + kv_specs,
        out_shape=[jax.ShapeDtypeStruct((np_, POOL_DIM), F32),
                   jax.ShapeDtypeStruct((ns, POOL_DIM), F32)] + qkv_shapes
        + [jax.ShapeDtypeStruct((N_GROUPS, ns, GROUP_DIM), F32),
           jax.ShapeDtypeStruct((N_GROUPS, ns, 2 * GROUP_DIM), F32)] + kv_shapes,
        scratch_shapes=[pltpu.VMEM((GROUP_DIM // LANES, TM, LANES), F32)],
        compiler_params=_cparams(1, VMEM_LIMIT),
        name="inproj",
    )(x3, mp, ms, mp, ms, gain, w_in_b, qg, kg)


def _rel_bucket_np(dist):
    dist = np.asarray(dist, np.int32)
    max_exact = REL_BUCKETS // 2
    d_f = np.maximum(dist, 1).astype(np.float32)
    ratio = np.log(d_f / np.float32(max_exact)) / np.float32(math.log(REL_MAX_DIST / max_exact))
    large = max_exact + (ratio * np.float32(REL_BUCKETS - max_exact)).astype(np.int32)
    large = np.minimum(large, REL_BUCKETS - 1)
    return np.where(dist < max_exact, dist, large)


def _bias_lookup(rel_bias, g, bucket, ok):
    table = rel_bias[:, g * HEADS:(g + 1) * HEADS].astype(F32)
    vals = jnp.zeros((HEADS,) + bucket.shape, F32)
    for bkt in np.unique(bucket):
        vals = jnp.where((bucket == bkt)[None], table[bkt].reshape((HEADS,) + (1,) * bucket.ndim), vals)
    return jnp.where(ok[None], vals, NEG_INF)


def _prompt_bias_tables(rel_bias):
    qi = np.arange(BAND)[:, None]
    kj = np.arange(2 * BAND)[None, :]
    delta = qi + BAND - kj
    ok = (delta >= 0) & (delta <= BAND)
    ok_first = ok & (kj >= BAND)
    tables = []
    for g, (_, dil) in enumerate(ATT_GROUPS):
        bucket = _rel_bucket_np(np.maximum(delta, 0) * dil)
        both = jnp.stack([_bias_lookup(rel_bias, g, bucket, ok), _bias_lookup(rel_bias, g, bucket, ok_first)])
        tables.append(both.reshape(2, HEADS * BAND, 2 * BAND))
    return tables


def _sample_bias_tables(rel_bias, t_new):
    tables = []
    for g, (win, dil) in enumerate(ATT_GROUPS):
        lb = min(win, PAST_LEN)
        t = np.arange(t_new)[:, None]
        col = np.arange(lb + LANES)[None, :]
        delta = lb + t - col
        ok = (col < lb + t_new) & (delta >= 0) & (delta % dil == 0) & (delta // dil <= win // dil)
        bucket = _rel_bucket_np(np.maximum(delta, 0))
        tables.append(_bias_lookup(rel_bias, g, bucket, ok).reshape(HEADS * t_new, lb + LANES))
    return tables


def _stack_heads(q):
    lh = _lane_head()
    return jnp.concatenate([jnp.where(lh == h, q, jnp.zeros_like(q)) for h in range(HEADS)], axis=0)


def _softmax(s):
    m = jnp.max(s, axis=1, keepdims=True)
    p = jnp.exp(s - m)
    l = jnp.sum(p, axis=1, keepdims=True)
    return p.astype(BF16), l, m + jnp.log(l)


def _unstack_heads(o4, lse4, rows):
    lh = _lane_head()
    o = o4[0:rows]
    lse = jnp.broadcast_to(lse4[0:rows], (rows, GROUP_DIM))
    for h in range(1, HEADS):
        o = jnp.where(lh == h, o4[h * rows:(h + 1) * rows], o)
        lse = jnp.where(lh == h, lse4[h * rows:(h + 1) * rows], lse)
    return o, lse


_NT = (((1,), (1,)), ((), ()))


def _attn_prompt_kernel(nsub, q_ref, kc_ref, kp_ref, vc_ref, vp_ref, bias_ref, o_ref, lse_ref, kbuf, vbuf):
    i = pl.program_id(2)
    kbuf[0:BAND] = kp_ref[...]
    kbuf[BAND:] = kc_ref[...]
    vbuf[0:BAND] = vp_ref[...]
    vbuf[BAND:] = vc_ref[...]
    for s in range(nsub):
        q4 = _stack_heads(q_ref[s * BAND:(s + 1) * BAND])
        kc = kbuf[s * BAND:(s + 2) * BAND]
        vc = vbuf[s * BAND:(s + 2) * BAND]
        logits = lax.dot_general(q4, kc, _NT, preferred_element_type=F32)
        if s == 0:
            bias = bias_ref[jnp.where(i == 0, 1, 0)]
        else:
            bias = bias_ref[0]
        p, l, lse4 = _softmax(logits + bias)
        o4 = jnp.dot(p, vc, preferred_element_type=F32) / l
        o, lse = _unstack_heads(o4, lse4, BAND)
        o_ref[s * BAND:(s + 1) * BAND] = o
        lse_ref[s * BAND:(s + 1) * BAND] = lse


def _attn_prompt(cfg, g, qkv, bias):
    b, s_len = cfg["b"], cfg["s"]
    _, dil = ATT_GROUPS[g]
    l_len = s_len // dil
    tq = min(TM, l_len)
    nsub = tq // BAND
    nq = l_len // tq

    def cur(which):
        return pl.BlockSpec((None, None, None, tq, GROUP_DIM), lambda bi, r, i: (which, bi, r, i, 0))

    def prev(which):
        return pl.BlockSpec((None, None, None, BAND, GROUP_DIM),
                            lambda bi, r, i: (which, bi, r, jnp.maximum(i * nsub - 1, 0), 0))

    out = pl.BlockSpec((None, None, tq, GROUP_DIM), lambda bi, r, i: (bi, r, i, 0))
    return pl.pallas_call(
        functools.partial(_attn_prompt_kernel, nsub),
        grid=(b, dil, nq),
        in_specs=[cur(0), cur(1), prev(1), cur(2), prev(2),
                  pl.BlockSpec(bias.shape, lambda bi, r, i: (0, 0, 0))],
        out_specs=[out, out],
        out_shape=[jax.ShapeDtypeStruct((b, dil, l_len, GROUP_DIM), F32)] * 2,
        scratch_shapes=[pltpu.VMEM((tq + BAND, GROUP_DIM), BF16)] * 2,
        compiler_params=_cparams(3, VMEM_LIMIT),
        name=f"attn_prompt_g{g}",
    )(qkv, qkv, qkv, qkv, qkv, bias)


def _attn_sample_kernel(bs, lb, t_new, q_ref, kvn_ref, cache_ref, bias_ref, o_ref, lse_ref):
    pad = jnp.zeros((LANES - t_new, GROUP_DIM), BF16)
    for j in range(bs):
        kt = cache_ref[j, 0:GROUP_DIM, :].astype(BF16)
        vt = cache_ref[j, GROUP_DIM:2 * GROUP_DIM, :].astype(BF16)
        kvn = kvn_ref[j].astype(BF16)
        kn = jnp.concatenate([kvn[:, :GROUP_DIM], pad], axis=0)
        vn = jnp.concatenate([kvn[:, GROUP_DIM:], pad], axis=0)
        q4 = _stack_heads(q_ref[j].astype(BF16))
        logits = jnp.concatenate(
            [jnp.dot(q4, kt, preferred_element_type=F32),
             lax.dot_general(q4, kn, _NT, preferred_element_type=F32)], axis=1)
        p, l, lse4 = _softmax(logits + bias_ref[...])
        o4 = (lax.dot_general(p[:, :lb], vt, _NT, preferred_element_type=F32)
              + jnp.dot(p[:, lb:], vn, preferred_element_type=F32)) / l
        o, lse = _unstack_heads(o4, lse4, t_new)
        o_ref[j] = o
        lse_ref[j] = lse


def _attn_sample(cfg, g, layer, qs, kvs, cache_t, bias):
    bd, t_new = cfg["bd"], cfg["t"]
    lb = cache_t.shape[-1]
    bs = max(1, min(8, (4 * 1024 * 1024) // (2 * GROUP_DIM * lb * 4)))
    out = pl.BlockSpec((bs, t_new, GROUP_DIM), lambda i: (i, 0, 0))
    o, lse = pl.pallas_call(
        functools.partial(_attn_sample_kernel, bs, lb, t_new),
        grid=(bd // bs,),
        in_specs=[pl.BlockSpec((None, bs, t_new, GROUP_DIM), lambda i: (g, i, 0, 0)),
                  pl.BlockSpec((None, bs, t_new, 2 * GROUP_DIM), lambda i: (g, i, 0, 0)),
                  pl.BlockSpec((None, bs, 2 * GROUP_DIM, lb), lambda i: (layer, i, 0, 0)),
                  pl.BlockSpec(bias.shape, lambda i: (0, 0))],
        out_specs=[out, out],
        out_shape=[jax.ShapeDtypeStruct((bd, t_new, GROUP_DIM), F32)] * 2,
        compiler_params=_cparams(1, VMEM_LIMIT),
        name=f"attn_sample_g{g}",
    )(qs.reshape(N_GROUPS, bd, t_new, GROUP_DIM), kvs.reshape(N_GROUPS, bd, t_new, 2 * GROUP_DIM),
      cache_t, bias)
    return o.reshape(bd * t_new, GROUP_DIM), lse.reshape(bd * t_new, GROUP_DIM)


def _pool_mix(ext_ref, rows, pos, w_ref, scale_ref, o_ref):
    lo = POOL_PAD
    u = ext_ref[:, lo:lo + rows, :]
    acc = u
    sums = {}
    for j in range(1, max(POOL_WINDOWS)):
        acc = acc + ext_ref[:, lo - j:lo - j + rows, :]
        if j + 1 in POOL_WINDOWS:
            sums[j + 1] = acc
    lane_grp = lax.broadcasted_iota(jnp.int32, (1, 1, POOL_DIM), 2) // (POOL_DIM // len(POOL_WINDOWS))
    z = None
    for gi, w in enumerate(POOL_WINDOWS):
        cnt = jnp.minimum(w, pos + 1).astype(F32)
        zw = sums[w] / cnt - u
        z = zw if z is None else jnp.where(lane_grp == gi, zw, z)
    nb = u.shape[0]
    y = jnp.dot(z.reshape(nb * rows, POOL_DIM).astype(BF16), w_ref[...], preferred_element_type=F32)
    o_ref[...] = (y * scale_ref[...]).astype(BF16)


def _pool_prompt_kernel(u_ref, halo_ref, w_ref, scale_ref, o_ref, ext_ref):
    i = pl.program_id(1)
    halo = halo_ref[...]
    ext_ref[:, 0:POOL_PAD, :] = jnp.where(i == 0, jnp.zeros_like(halo), halo)
    ext_ref[:, POOL_PAD:, :] = u_ref[...]
    pos = i * TM + lax.broadcasted_iota(jnp.int32, (1, TM, 1), 1)
    _pool_mix(ext_ref, TM, pos, w_ref, scale_ref, o_ref)


def _pool_sample_kernel(t_new, u_ref, hist_ref, w_ref, scale_ref, o_ref, ext_ref):
    nb = u_ref.shape[0]
    ext_ref[:, 0:POOL_PAD - POOL_HIST, :] = jnp.zeros((nb, POOL_PAD - POOL_HIST, POOL_DIM), F32)
    ext_ref[:, POOL_PAD - POOL_HIST:POOL_PAD, :] = hist_ref[...]
    ext_ref[:, POOL_PAD:, :] = u_ref[...]
    pos = PAST_LEN + lax.broadcasted_iota(jnp.int32, (1, t_new, 1), 1)
    _pool_mix(ext_ref, t_new, pos, w_ref, scale_ref, o_ref)


def _pool_prompt(cfg, u_p, w_bd, scale):
    b, s_len = cfg["b"], cfg["s"]
    u3 = u_p.reshape(b, s_len, POOL_DIM)
    per = TM // POOL_PAD
    return pl.pallas_call(
        _pool_prompt_kernel,
        grid=(b, s_len // TM),
        in_specs=[pl.BlockSpec((1, TM, POOL_DIM), lambda bi, i: (bi, i, 0)),
                  pl.BlockSpec((1, POOL_PAD, POOL_DIM), lambda bi, i: (bi, jnp.maximum(i * per - 1, 0), 0)),
                  pl.BlockSpec((POOL_DIM, POOL_DIM), lambda bi, i: (0, 0)),
                  pl.BlockSpec((1, POOL_DIM), lambda bi, i: (0, 0))],
        out_specs=pl.BlockSpec((TM, POOL_DIM), lambda bi, i: (bi * (s_len // TM) + i, 0)),
        out_shape=jax.ShapeDtypeStruct((b * s_len, POOL_DIM), BF16),
        scratch_shapes=[pltpu.VMEM((1, TM + POOL_PAD, POOL_DIM), F32)],
        compiler_params=_cparams(2, VMEM_LIMIT),
        name="pool_prompt",
    )(u3, u3, w_bd, scale)


def _pool_sample(cfg, layer, u_s, state_pool, w_bd, scale):
    bd, t_new = cfg["bd"], cfg["t"]
    u3 = u_s.reshape(bd, t_new, POOL_DIM)
    nb = GP
    return pl.pallas_call(
        functools.partial(_pool_sample_kernel, t_new),
        grid=(bd // nb,),
        in_specs=[pl.BlockSpec((nb, t_new, POOL_DIM), lambda i: (i, 0, 0)),
                  pl.BlockSpec((None, nb, POOL_HIST, POOL_DIM), lambda i: (layer, i, 0, 0)),
                  pl.BlockSpec((POOL_DIM, POOL_DIM), lambda i: (0, 0)),
                  pl.BlockSpec((1, POOL_DIM), lambda i: (0, 0))],
        out_specs=pl.BlockSpec((nb * t_new, POOL_DIM), lambda i: (i, 0)),
        out_shape=jax.ShapeDtypeStruct((bd * t_new, POOL_DIM), BF16),
        scratch_shapes=[pltpu.VMEM((nb, t_new + POOL_PAD, POOL_DIM), F32)],
        compiler_params=_cparams(1, VMEM_LIMIT),
        name="pool_sample",
    )(u3, state_pool, w_bd, scale)


def _outproj_kernel(cfg, x_ref, g1p_ref, g1s_ref, shp_ref, shs_ref, scp_ref, scs_ref, pp_ref, ps_ref, *rest):
    att_refs = rest[:4 * N_GROUPS]
    (gain_ref, wo_ref, wr_ref, br_ref, x1_ref, h2_ref, route_ref, cnt_ref,
     run_ref, nat_ref) = rest[4 * N_GROUPS:]
    i = pl.program_id(0)
    n_pt, d = cfg["n_pt"], cfg["d"]
    is_p = i < n_pt

    halves = GROUP_DIM // LANES

    def put(slot, val, rows=None):
        for half in range(halves):
            piece = val[:, half * LANES:(half + 1) * LANES]
            if rows is None:
                nat_ref[slot, half] = piece
            else:
                nat_ref[slot, half, rows, :] = piece

    @pl.when(is_p)
    def _():
        for g, (_, dil) in enumerate(ATT_GROUPS):
            for which in range(2):
                src = att_refs[4 * g + which]
                if dil == 1:
                    put(2 * g + which, src[0])
                else:
                    for r in range(dil):
                        put(2 * g + which, src[r], pl.ds(r, TM // dil, stride=dil))

    @pl.when(jnp.logical_not(is_p))
    def _():
        for g in range(N_GROUPS):
            for which in range(2):
                put(2 * g + which, att_refs[4 * g + 2 + which][...])

    def get(slot):
        return jnp.concatenate([nat_ref[slot, half] for half in range(halves)], axis=1)

    o = [get(2 * g) for g in range(N_GROUPS)]
    lse = [get(2 * g + 1) for g in range(N_GROUPS)]
    mx = functools.reduce(jnp.maximum, lse)
    e = [jnp.exp(l - mx) for l in lse]
    attn = sum(eg * og for eg, og in zip(e, o)) / sum(e)
    pool = jnp.where(is_p, pp_ref[...], ps_ref[...])
    cat = jnp.concatenate([pool, attn.astype(BF16)], axis=1)
    y = jnp.dot(cat, wo_ref[...], preferred_element_type=F32)

    g1 = jnp.where(is_p, g1p_ref[...], g1s_ref[...])
    x1 = x_ref[...] + g1 * y.reshape(GP, SUBLANES, d)
    x1_ref[...] = x1
    sh = jnp.where(is_p, shp_ref[...], shs_ref[...])
    sc = jnp.where(is_p, scp_ref[...], scs_ref[...])
    h2 = _rmsnorm_mod(x1, gain_ref[...], sc, sh).reshape(TM, d)
    _rows_to_slabs(h2_ref, h2)

    h_hi = h2.astype(BF16)
    h_lo = (h2 - h_hi.astype(F32)).astype(BF16)
    wr = wr_ref[...]
    w_hi = wr.astype(BF16)
    w_lo = (wr - w_hi.astype(F32)).astype(BF16)
    logits = (jnp.dot(h_hi, w_hi, preferred_element_type=F32)
              + jnp.dot(h_lo, w_hi, preferred_element_type=F32)
              + jnp.dot(h_hi, w_lo, preferred_element_type=F32)) + br_ref[...]

    lane = lax.broadcasted_iota(jnp.int32, (TM, LANES), 1).astype(F32)
    vals = logits
    top_v, top_i = [], []
    onehot = jnp.zeros((TM, LANES), F32)
    for _ in range(TOP_K):
        m = jnp.max(vals, axis=1, keepdims=True)
        idx = jnp.min(jnp.where(vals == m, lane, float(LANES)), axis=1, keepdims=True)
        hit = lane == idx
        vals = jnp.where(hit, -jnp.inf, vals)
        onehot = jnp.where(hit, 1.0, onehot)
        top_v.append(m)
        top_i.append(idx)
    ev = [jnp.exp(v - top_v[0]) for v in top_v]
    den = sum(ev)

    @pl.when(i == 0)
    def _():
        run_ref[...] = jnp.zeros_like(run_ref)

    row = lax.broadcasted_iota(jnp.int32, (TM, TM), 0)
    col = lax.broadcasted_iota(jnp.int32, (TM, TM), 1)
    before = (col < row).astype(BF16)
    rank_all = jnp.dot(before, onehot.astype(BF16), preferred_element_type=F32) + run_ref[...]
    route = jnp.zeros((TM, LANES), F32)
    for k in range(TOP_K):
        rank_k = jnp.sum(jnp.where(lane == top_i[k], rank_all, 0.0), axis=1, keepdims=True)
        route = jnp.where(lane == float(k), top_i[k], route)
        route = jnp.where(lane == float(TOP_K + k), ev[k] / den, route)
        route = jnp.where(lane == float(2 * TOP_K + k), rank_k, route)
    route_ref[...] = route
    run_ref[...] = run_ref[...] + jnp.sum(onehot, axis=0, keepdims=True)
    cnt_ref[...] = run_ref[...]


def _outproj(cfg, x3, mp, ms, pool_p, pool_s, att, gain, w_out_b, w_r, b_r):
    d, n_pt, n_t = cfg["d"], cfg["n_pt"], cfg["n_t"]
    n = cfg["n"]
    tps = cfg["tps"]
    pidx = lambda i: (jnp.minimum(i, n_pt - 1), 0)
    sidx = lambda i: (jnp.maximum(i - n_pt, 0), 0)
    pspec = pl.BlockSpec((TM, GROUP_DIM), pidx)
    sspec = pl.BlockSpec((TM, GROUP_DIM), sidx)
    att_specs, att_args = [], []
    for (_, dil), (o_p, lse_p, o_s, lse_s) in zip(ATT_GROUPS, att):
        rspec = pl.BlockSpec((None, dil, TM // dil, GROUP_DIM),
                             lambda i: (pidx(i)[0] // tps, 0, pidx(i)[0] % tps, 0))
        att_specs += [rspec, rspec, sspec, sspec]
        att_args += [o_p, lse_p, o_s, lse_s]
    full = lambda a: pl.BlockSpec(a.shape, lambda i: (0,) * a.ndim)
    return pl.pallas_call(
        functools.partial(_outproj_kernel, cfg),
        grid=(n_t,),
        in_specs=[pl.BlockSpec((GP, SUBLANES, d), lambda i: (i, 0, 0))]
        + _mod_specs(cfg, 2) + _mod_specs(cfg, 3) + _mod_specs(cfg, 4)
        + [pspec, sspec] + att_specs + [full(gain), full(w_out_b), full(w_r), full(b_r)],
        out_specs=[pl.BlockSpec((GP, SUBLANES, d), lambda i: (i, 0, 0)),
                   pl.BlockSpec((TM * SUBLANES, LANES), lambda i: (i, 0)),
                   pl.BlockSpec((TM, LANES), lambda i: (i, 0)),
                   pl.BlockSpec((1, LANES), lambda i: (0, 0))],
        out_shape=[jax.ShapeDtypeStruct(x3.shape, F32),
                   jax.ShapeDtypeStruct((n * SUBLANES, LANES), F32),
                   jax.ShapeDtypeStruct((n, LANES), F32),
                   jax.ShapeDtypeStruct((1, LANES), F32)],
        scratch_shapes=[pltpu.VMEM((1, LANES), F32),
                        pltpu.VMEM((2 * N_GROUPS, GROUP_DIM // LANES, TM, LANES), F32)],
        compiler_params=_cparams(1, VMEM_LIMIT),
        name="outproj_router",
    )(x3, mp, ms, mp, ms, mp, ms, pool_p, pool_s, *att_args, gain, w_out_b, w_r, b_r)


def _rows_from_slabs(ref):
    rows = ref.shape[0] // SUBLANES
    return jnp.concatenate([ref[pl.ds(j, rows, stride=SUBLANES), :] for j in range(SUBLANES)], axis=1)


def _rows_to_slabs(ref, val):
    for j in range(SUBLANES):
        ref[pl.ds(j, val.shape[0], stride=SUBLANES), :] = val[:, j * LANES:(j + 1) * LANES]


def _dispatch_kernel(zt_ref, pos_ref, h_ref, x_hbm, zbuf, zsem, sem):
    i = pl.program_id(0)
    rows = h_ref.shape[0] // SUBLANES

    @pl.when(i == 0)
    def _():
        zbuf[...] = jnp.zeros_like(zbuf)

        def fill(z):
            return pltpu.make_async_copy(zbuf, x_hbm.at[pl.ds(zt_ref[z] * MOE_TM, MOE_TM)], zsem)

        for z in range(zt_ref.shape[0]):
            pl.when(zt_ref[z] >= 0)(lambda z=z: fill(z).start())
        for z in range(zt_ref.shape[0]):
            pl.when(zt_ref[z] >= 0)(lambda z=z: fill(z).wait())

    for r in range(rows):
        for k in range(TOP_K):
            pltpu.make_async_copy(h_ref.at[pl.ds(r * SUBLANES, SUBLANES)], x_hbm.at[pos_ref[0, r * TOP_K + k]],
                                  sem.at[k]).start()
    for k in range(TOP_K):
        pltpu.make_async_copy(h_ref, h_ref, sem.at[k]).wait()


def _dispatch(cfg, pos, zero_tiles, h2, m_pad):
    n_t = cfg["n_t"]
    grid_spec = pltpu.PrefetchScalarGridSpec(
        num_scalar_prefetch=1,
        grid=(n_t,),
        in_specs=[pl.BlockSpec((None, 1, TM * TOP_K), lambda i, zt: (i, 0, 0), memory_space=pltpu.SMEM),
                  pl.BlockSpec((TM * SUBLANES, LANES), lambda i, zt: (i, 0))],
        out_specs=pl.BlockSpec(memory_space=pl.ANY),
        scratch_shapes=[pltpu.VMEM((MOE_TM, SUBLANES, LANES), F32),
                        pltpu.SemaphoreType.DMA(()),
                        pltpu.SemaphoreType.DMA((TOP_K,))],
    )
    return pl.pallas_call(
        _dispatch_kernel,
        grid_spec=grid_spec,
        out_shape=jax.ShapeDtypeStruct((m_pad, SUBLANES, LANES), F32),
        compiler_params=_cparams(1, VMEM_LIMIT),
        name="moe_dispatch",
    )(zero_tiles, pos.reshape(n_t, 1, TM * TOP_K), h2)


def _moe_kernel(te_ref, prv_ref, last_ref, x_ref, w1_ref, b1_ref, w2_ref, b2_ref, y_hbm,
                y0, y1, w1b, w2b, ssem):
    i = pl.program_id(0)
    n = pl.num_programs(0)
    ff = w2_ref.shape[0]
    ys = (y0, y1)

    def start_scatter(idx_ref, p):
        for r in range(MOE_TM):
            pltpu.make_async_copy(ys[p].at[pl.ds(r * SUBLANES, SUBLANES)], y_hbm.at[idx_ref[0, r]],
                                  ssem.at[p]).start()

    def wait_scatter(p):
        pltpu.make_async_copy(ys[p], ys[p], ssem.at[p]).wait()

    @pl.when(i == 0)
    def _():
        n_asg = y_hbm.shape[0] - 2 * MOE_TM
        for p in range(2):
            ys[p][...] = jnp.zeros_like(ys[p])
            for r in range(MOE_TM):
                pltpu.make_async_copy(ys[p].at[pl.ds(r * SUBLANES, SUBLANES)],
                                      y_hbm.at[n_asg + p * MOE_TM + r], ssem.at[p]).start()
            wait_scatter(p)

    @pl.when(jnp.logical_or(i == 0, te_ref[i] != te_ref[jnp.maximum(i - 1, 0)]))
    def _():
        w1b[...] = w1_ref[...].astype(BF16)
        w2b[...] = w2_ref[...].astype(BF16)

    def step(p):
        start_scatter(prv_ref, 1 - p)
        x = _rows_from_slabs(x_ref).astype(BF16)
        h1 = jnp.dot(x, w1b[...], preferred_element_type=F32) + b1_ref[...]
        gate = jnp.minimum(h1[:, :ff], SWIGLU_LIMIT)
        up = jnp.clip(h1[:, ff:], -SWIGLU_LIMIT, SWIGLU_LIMIT)
        act = gate * jax.nn.sigmoid(SWIGLU_ALPHA * gate) * (up + 1.0)
        _rows_to_slabs(ys[p], jnp.dot(act.astype(BF16), w2b[...], preferred_element_type=F32) + b2_ref[...])
        wait_scatter(1 - p)

        @pl.when(i == n - 1)
        def _():
            start_scatter(last_ref, p)
            wait_scatter(p)

    for p in range(2):
        pl.when(i % 2 == p)(functools.partial(step, p))


def _moe(layer, n_tiles, tile_expert, slot_dst, x_sorted, w1, b1, w2, b2, n_rows_out):
    depth, n_exp, d, ff2 = w1.shape
    ff = w2.shape[2]
    smem_tile = lambda f: pl.BlockSpec((None, 1, MOE_TM), f, memory_space=pltpu.SMEM)
    grid_spec = pltpu.PrefetchScalarGridSpec(
        num_scalar_prefetch=1,
        grid=(n_tiles,),
        in_specs=[smem_tile(lambda i, te: (jnp.maximum(i - 1, 0), 0, 0)),
                  smem_tile(lambda i, te: (n_tiles - 1, 0, 0)),
                  pl.BlockSpec((MOE_TM * SUBLANES, LANES), lambda i, te: (i, 0)),
                  pl.BlockSpec((None, None, d, ff2), lambda i, te: (layer, te[i], 0, 0)),
                  pl.BlockSpec((None, None, 1, ff2), lambda i, te: (layer, te[i], 0, 0)),
                  pl.BlockSpec((None, None, ff, d), lambda i, te: (layer, te[i], 0, 0)),
                  pl.BlockSpec((None, None, 1, d), lambda i, te: (layer, te[i], 0, 0))],
        out_specs=pl.BlockSpec(memory_space=pl.ANY),
        scratch_shapes=[pltpu.VMEM((MOE_TM * SUBLANES, LANES), F32)] * 2
        + [pltpu.VMEM((d, ff2), BF16),
           pltpu.VMEM((ff, d), BF16),
           pltpu.SemaphoreType.DMA((2,))],
    )
    dst3 = slot_dst.reshape(n_tiles, 1, MOE_TM)
    return pl.pallas_call(
        _moe_kernel,
        grid_spec=grid_spec,
        out_shape=jax.ShapeDtypeStruct((n_rows_out, SUBLANES, LANES), F32),
        compiler_params=_cparams(1, VMEM_LIMIT),
        name="moe_experts",
    )(tile_expert, dst3, dst3, x_sorted.reshape(x_sorted.shape[0] * SUBLANES, LANES),
      w1, b1.reshape(depth, n_exp, 1, ff2), w2, b2.reshape(depth, n_exp, 1, d))


def _moe_plan(cfg, route, counts):
    n = cfg["n"]
    m = n * TOP_K
    n_tiles = m // MOE_TM + N_EXPERTS
    m_pad = n_tiles * MOE_TM
    top_i = route[:, 0:TOP_K].astype(jnp.int32)
    rank = route[:, 2 * TOP_K:3 * TOP_K].astype(jnp.int32)
    cnt = counts[0, :N_EXPERTS].astype(jnp.int32)
    tiles_e = (cnt + MOE_TM - 1) // MOE_TM
    tile_end = jnp.cumsum(tiles_e)
    pstart = (tile_end - tiles_e) * MOE_TM
    expert = jnp.arange(N_EXPERTS, dtype=jnp.int32)
    pos = jnp.sum(jnp.where(top_i[..., None] == expert, pstart, 0), axis=-1) + rank
    t = jnp.arange(n_tiles, dtype=jnp.int32)
    te = jnp.sum((tile_end[None, :] <= t[:, None]).astype(jnp.int32), axis=1)
    last_used = jnp.sum((tile_end <= tile_end[-1] - 1).astype(jnp.int32))
    te = jnp.minimum(te, last_used).astype(jnp.int32)
    s = jnp.arange(m_pad, dtype=jnp.int32)
    dump = m + ((s // MOE_TM) % 2) * MOE_TM + s % MOE_TM
    slot_dst = dump.at[pos.T.reshape(-1)].set(jnp.arange(m, dtype=jnp.int32))
    n_used = tile_end[-1]
    spare = n_used + jnp.arange(n_tiles - m // MOE_TM, dtype=jnp.int32)
    zero_tiles = jnp.concatenate([jnp.where(tiles_e > 0, tile_end - 1, -1),
                                  jnp.where(spare < n_tiles, spare, -1)]).astype(jnp.int32)
    return n_tiles, te, pos, zero_tiles, slot_dst, m + 2 * MOE_TM


def _combine_kernel(cfg, x_ref, g2p_ref, g2s_ref, route_ref, *rest):
    y_refs, o_ref = rest[:TOP_K], rest[TOP_K]
    i = pl.program_id(0)
    is_p = i < cfg["n_pt"]
    route = route_ref[...]
    acc = route[:, TOP_K:TOP_K + 1] * _rows_from_slabs(y_refs[0])
    for k in range(1, TOP_K):
        acc = acc + route[:, TOP_K + k:TOP_K + k + 1] * _rows_from_slabs(y_refs[k])
    g2 = jnp.where(is_p, g2p_ref[...], g2s_ref[...])
    o_ref[...] = x_ref[...] + g2 * acc.reshape(GP, SUBLANES, cfg["d"])


def _combine(cfg, x1, mp, ms, route, ybuf):
    d, n_t = cfg["d"], cfg["n_t"]
    y_specs = [pl.BlockSpec((TM * SUBLANES, LANES), lambda i, k=k: (k * n_t + i, 0)) for k in range(TOP_K)]
    ybuf = ybuf.reshape(ybuf.shape[0] * SUBLANES, LANES)
    return pl.pallas_call(
        functools.partial(_combine_kernel, cfg),
        grid=(n_t,),
        in_specs=[pl.BlockSpec((GP, SUBLANES, d), lambda i: (i, 0, 0))] + _mod_specs(cfg, 5)
        + [pl.BlockSpec((TM, LANES), lambda i: (i, 0))] + y_specs,
        out_specs=pl.BlockSpec((GP, SUBLANES, d), lambda i: (i, 0, 0)),
        out_shape=jax.ShapeDtypeStruct(x1.shape, F32),
        compiler_params=_cparams(1, VMEM_LIMIT),
        name="moe_combine",
    )(x1, mp, ms, route, *([ybuf] * TOP_K))


def kernel(x_prompt, x_sample, cache_kv_w128_d1, cache_kv_w512_d4, cache_kv_w2048_d16, state_pool,
           c_prompt, c_sample, rel_bias, norm_mix, norm_ffn, w_ada, b_ada, w_in, q_norm, k_norm,
           pool_w, pool_scale, w_out, w_router, b_router, w_expert_in, b_expert_in,
           w_expert_out, b_expert_out):
    b, s_len, d = x_prompt.shape
    bd, t_new, _ = x_sample.shape
    depth = w_in.shape[0]
    np_, ns = b * s_len, bd * t_new
    assert t_new == SUBLANES and s_len % TM == 0 and ns % TM == 0
    assert all((s_len // dil) % BAND == 0 for _, dil in ATT_GROUPS)
    cfg = dict(b=b, s=s_len, d=d, bd=bd, t=t_new, np=np_, ns=ns, n=np_ + ns,
               n_pt=np_ // TM, n_t=(np_ + ns) // TM, tps=s_len // TM)
    kv_keep = []
    for win, _ in ATT_GROUPS:
        keep = min(win, s_len)
        assert keep % TM == 0 or TM % keep == 0
        kv_keep.append((keep, max(keep // TM, 1)))
    cfg["kv_keep"] = tuple(kv_keep)
    caches_t = [jnp.transpose(c, (0, 1, 3, 4, 5, 2)).reshape(depth, bd, 2 * GROUP_DIM, c.shape[2])
                for c in (cache_kv_w128_d1, cache_kv_w512_d4, cache_kv_w2048_d16)]

    mods = _ada_mods(jnp.concatenate([c_prompt, c_sample], axis=0), w_ada, b_ada)
    bias_p = _prompt_bias_tables(rel_bias)
    bias_s = _sample_bias_tables(rel_bias, t_new)
    w_in_b = w_in.astype(BF16)
    w_out_b = w_out.astype(BF16)
    eye = jnp.eye(len(POOL_WINDOWS), dtype=F32)
    w_pool_bd = (eye[None, :, None, :, None] * pool_w[:, :, :, None, :]).reshape(depth, POOL_DIM, POOL_DIM)
    w_pool_bd = w_pool_bd.astype(BF16)
    w_r_pad = jnp.pad(w_router, ((0, 0), (0, 0), (0, LANES - N_EXPERTS)))
    b_r_pad = jnp.pad(b_router, ((0, 0), (0, LANES - N_EXPERTS)), constant_values=NEG_INF)

    x3 = jnp.concatenate([x_prompt.reshape(np_, d), x_sample.reshape(ns, d)], axis=0)
    x3 = x3.reshape((np_ + ns) // SUBLANES, SUBLANES, d)

    kv_p = [[] for _ in ATT_GROUPS]
    kv_s = [[] for _ in ATT_GROUPS]
    pool_p_state, pool_s_state = [], []
    for l in range(depth):
        mp = mods[l, :b].reshape(b, 1, 6 * d)
        ms = mods[l, b:].reshape(bd, 1, 6 * d)
        qg = jnp.tile(q_norm[l], HEADS).reshape(1, GROUP_DIM)
        kg = jnp.tile(k_norm[l], HEADS).reshape(1, GROUP_DIM)
        (u_p, u_s, qkv0, qkv1, qkv2, qs, kvs, *kvp) = _inproj(
            cfg, x3, mp, ms, norm_mix[l].reshape(1, d), w_in_b[l], qg, kg)

        att = []
        for g, qkv in enumerate((qkv0, qkv1, qkv2)):
            o_p, lse_p = _attn_prompt(cfg, g, qkv, bias_p[g])
            o_s, lse_s = _attn_sample(cfg, g, l, qs, kvs, caches_t[g], bias_s[g])
            att.append((o_p, lse_p, o_s, lse_s))
            keep = cfg["kv_keep"][g][0]
            kv_p[g].append(kvp[g].reshape(b, keep, 2, HEADS, HEAD_DIM))
            kv_s[g].append(kvs[g].reshape(bd, t_new, 2, HEADS, HEAD_DIM))

        scale = pool_scale[l].reshape(1, POOL_DIM)
        pool_p = _pool_prompt(cfg, u_p, w_pool_bd[l], scale)
        pool_s = _pool_sample(cfg, l, u_s, state_pool, w_pool_bd[l], scale)
        pool_p_state.append(u_p.reshape(b, s_len, POOL_DIM)[:, s_len - POOL_HIST:])
        pool_s_state.append(jnp.concatenate(
            [state_pool[l], u_s.reshape(bd, t_new, POOL_DIM)], axis=1)[:, -POOL_HIST:])

        x1, h2, route, counts = _outproj(cfg, x3, mp, ms, pool_p, pool_s, att,
                                         norm_ffn[l].reshape(1, d), w_out_b[l], w_r_pad[l],
                                         b_r_pad[l].reshape(1, LANES))
        n_tiles, te, pos, zero_tiles, slot_dst, n_rows_out = _moe_plan(cfg, route, counts)
        x_sorted = _dispatch(cfg, pos, zero_tiles, h2, n_tiles * MOE_TM)
        ybuf = _moe(l, n_tiles, te, slot_dst, x_sorted,
                    w_expert_in, b_expert_in, w_expert_out, b_expert_out, n_rows_out)
        x3 = _combine(cfg, x1, mp, ms, route, ybuf)

    x_all = x3.reshape(np_ + ns, d)
    y_prompt = x_all[:np_].reshape(b, s_len, d)
    y_sample = x_all[np_:].reshape(bd, t_new, d)
    return (y_prompt, y_sample,
            jnp.stack(kv_p[0]), jnp.stack(kv_p[1]), jnp.stack(kv_p[2]), jnp.stack(pool_p_state),
            jnp.stack(kv_s[0]), jnp.stack(kv_s[1]), jnp.stack(kv_s[2]), jnp.stack(pool_s_state))
```

```python
import functools
import math

import numpy as np
import jax
import jax.numpy as jnp
from jax import lax
from jax.experimental import pallas as pl
from jax.experimental.pallas import tpu as pltpu

F32 = jnp.float32
BF16 = jnp.bfloat16

HEAD_DIM = 64
HEADS = 4
GROUP_DIM = HEADS * HEAD_DIM
ATT_GROUPS = ((128, 1), (512, 4), (2048, 16))
N_GROUPS = len(ATT_GROUPS)
BAND = 128
POOL_WINDOWS = (2, 4, 8, 16)
POOL_HIST = 15
POOL_DIM = 256
POOL_PAD = 16
REL_BUCKETS = 32
REL_MAX_DIST = 2048
N_EXPERTS = 32
TOP_K = 4
SWIGLU_LIMIT = 7.0
SWIGLU_ALPHA = 1.702
EPS = 1e-6
NEG_INF = -1e30
PAST_LEN = 2048

SUBLANES = 8
LANES = 128
TM = 512
GP = TM // SUBLANES
MOE_TM = 256
VMEM_LIMIT = 52 * 1024 * 1024


def _cparams(n_axes, vmem=None):
    return pltpu.CompilerParams(dimension_semantics=("arbitrary",) * n_axes,
                                vmem_limit_bytes=vmem)


def _lane_head(width=GROUP_DIM):
    return lax.broadcasted_iota(jnp.int32, (1, width), 1) // HEAD_DIM


def _ada_kernel(c_ref, w_ref, b_ref, o_ref):
    c = c_ref[...]
    a = (c * jax.nn.sigmoid(c)).astype(BF16)
    o_ref[...] = jnp.dot(a, w_ref[...].astype(BF16), preferred_element_type=F32) + b_ref[...]


def _ada_mods(c_all, w_ada, b_ada):
    depth, d, d6 = w_ada.shape
    bc = c_all.shape[0]
    tn = d6 // 4
    return pl.pallas_call(
        _ada_kernel,
        grid=(depth, d6 // tn),
        in_specs=[pl.BlockSpec((bc, d), lambda l, j: (0, 0)),
                  pl.BlockSpec((None, d, tn), lambda l, j: (l, 0, j)),
                  pl.BlockSpec((None, 1, tn), lambda l, j: (l, 0, j))],
        out_specs=pl.BlockSpec((None, bc, tn), lambda l, j: (l, 0, j)),
        out_shape=jax.ShapeDtypeStruct((depth, bc, d6), F32),
        compiler_params=_cparams(2, VMEM_LIMIT),
        name="ada_mods",
    )(c_all, w_ada, b_ada.reshape(depth, 1, d6))


def _mod_specs(cfg, col):
    d = cfg["d"]
    n_pt, tps, b = cfg["n_pt"], cfg["tps"], cfg["b"]
    return [pl.BlockSpec((1, 1, d), lambda i: (jnp.minimum(i // tps, b - 1), 0, col)),
            pl.BlockSpec((GP, 1, d), lambda i: (jnp.maximum(i - n_pt, 0), 0, col))]


def _rmsnorm_mod(x, g, sc, sh):
    inv = lax.rsqrt(jnp.mean(x * x, axis=-1, keepdims=True) + EPS)
    return x * inv * g * (1.0 + sc) + sh


def _inproj_kernel(cfg, x_ref, shp_ref, shs_ref, scp_ref, scs_ref, g_ref, w_ref, qg_ref, kg_ref,
                   up_ref, us_ref, qkv0_ref, qkv1_ref, qkv2_ref, qs_ref, kvs_ref,
                   kvp0_ref, kvp1_ref, kvp2_ref, res_ref):
    qkv_refs = (qkv0_ref, qkv1_ref, qkv2_ref)
    kvp_refs = (kvp0_ref, kvp1_ref, kvp2_ref)
    i = pl.program_id(0)
    n_pt, tps, d = cfg["n_pt"], cfg["tps"], cfg["d"]
    is_p = i < n_pt
    sh = jnp.where(is_p, shp_ref[...], shs_ref[...])
    sc = jnp.where(is_p, scp_ref[...], scs_ref[...])
    h = _rmsnorm_mod(x_ref[...], g_ref[...], sc, sh)
    z = jnp.dot(h.reshape(TM, d).astype(BF16), w_ref[...], preferred_element_type=F32)

    r = lax.broadcasted_iota(jnp.int32, (GROUP_DIM, GROUP_DIM), 0) // HEAD_DIM
    c = lax.broadcasted_iota(jnp.int32, (GROUP_DIM, GROUP_DIM), 1) // HEAD_DIM
    head_ones = (r == c).astype(BF16)

    def head_norm(t, gain):
        ss = jnp.dot((t * t).astype(BF16), head_ones, preferred_element_type=F32) * (1.0 / HEAD_DIM)
        return t * lax.rsqrt(ss + EPS) * gain

    u = z[:, :POOL_DIM]
    att = d - POOL_DIM
    q, k, v = [], [], []
    for g in range(N_GROUPS):
        lo = POOL_DIM + g * GROUP_DIM
        q.append(head_norm(z[:, lo:lo + GROUP_DIM], qg_ref[...]) * (HEAD_DIM ** -0.5))
        k.append(head_norm(z[:, lo + att:lo + att + GROUP_DIM], kg_ref[...]))
        v.append(z[:, lo + 2 * att:lo + 2 * att + GROUP_DIM])

    @pl.when(is_p)
    def _():
        up_ref[...] = u
        for g, (_, dil) in enumerate(ATT_GROUPS):
            for which, val in enumerate((q[g], k[g], v[g])):
                if dil == 1:
                    qkv_refs[g][which, 0] = val.astype(BF16)
                else:
                    for half in range(GROUP_DIM // LANES):
                        res_ref[half] = val[:, half * LANES:(half + 1) * LANES]
                    for r in range(dil):
                        rows = [res_ref[half, pl.ds(r, TM // dil, stride=dil), :]
                                for half in range(GROUP_DIM // LANES)]
                        qkv_refs[g][which, r] = jnp.concatenate(rows, axis=1).astype(BF16)

    @pl.when(jnp.logical_not(is_p))
    def _():
        us_ref[...] = u
        for g in range(N_GROUPS):
            qs_ref[g] = q[g]
            kvs_ref[g, :, :GROUP_DIM] = k[g]
            kvs_ref[g, :, GROUP_DIM:] = v[g]

    j = jnp.minimum(i, n_pt - 1) % tps
    for g, (keep, nk) in enumerate(cfg["kv_keep"]):
        rows = min(keep, TM)

        @pl.when(jnp.logical_and(is_p, j >= tps - nk))
        def _(g=g, rows=rows):
            kvp_refs[g][:, :GROUP_DIM] = k[g][TM - rows:]
            kvp_refs[g][:, GROUP_DIM:] = v[g][TM - rows:]


def _inproj(cfg, x3, mp, ms, gain, w_in_b, qg, kg):
    d, n_pt, n_t, tps, b = cfg["d"], cfg["n_pt"], cfg["n_t"], cfg["tps"], cfg["b"]
    np_, ns = cfg["np"], cfg["ns"]
    pidx = lambda i: jnp.minimum(i, n_pt - 1)
    sidx = lambda i: jnp.maximum(i - n_pt, 0)

    kv_specs, kv_shapes = [], []
    for keep, nk in cfg["kv_keep"]:
        rows = min(keep, TM)

        def kv_idx(i, nk=nk):
            ip = pidx(i)
            return ((ip // tps) * nk + jnp.maximum(ip % tps - (tps - nk), 0), 0)

        kv_specs.append(pl.BlockSpec((rows, 2 * GROUP_DIM), kv_idx))
        kv_shapes.append(jax.ShapeDtypeStruct((b * keep, 2 * GROUP_DIM), F32))

    qkv_specs, qkv_shapes = [], []
    for _, dil in ATT_GROUPS:
        qkv_specs.append(pl.BlockSpec((3, None, dil, TM // dil, GROUP_DIM),
                                      lambda i: (0, pidx(i) // tps, 0, pidx(i) % tps, 0)))
        qkv_shapes.append(jax.ShapeDtypeStruct((3, b, dil, cfg["s"] // dil, GROUP_DIM), BF16))

    return pl.pallas_call(
        functools.partial(_inproj_kernel, cfg),
        grid=(n_t,),
        in_specs=[pl.BlockSpec((GP, SUBLANES, d), lambda i: (i, 0, 0))]
        + _mod_specs(cfg, 0) + _mod_specs(cfg, 1)
        + [pl.BlockSpec((1, d), lambda i: (0, 0)),
           pl.BlockSpec(w_in_b.shape, lambda i: (0, 0)),
           pl.BlockSpec((1, GROUP_DIM), lambda i: (0, 0)),
           pl.BlockSpec((1, GROUP_DIM), lambda i: (0, 0))],
        out_specs=[pl.BlockSpec((TM, POOL_DIM), lambda i: (pidx(i), 0)),
                   pl.BlockSpec((TM, POOL_DIM), lambda i: (sidx(i), 0))] + qkv_specs
        + [pl.BlockSpec((N_GROUPS, TM, GROUP_DIM), lambda i: (0, sidx(i), 0)),
           pl.BlockSpec((N_GROUPS, TM, 2 * GROUP_DIM), lambda i: (0, sidx(i), 0))] + kv_specs,
        out_shape=[jax.ShapeDtypeStruct((np_, POOL_DIM), F32),
                   jax.ShapeDtypeStruct((ns, POOL_DIM), F32)] + qkv_shapes
        + [jax.ShapeDtypeStruct((N_GROUPS, ns, GROUP_DIM), F32),
           jax.ShapeDtypeStruct((N_GROUPS, ns, 2 * GROUP_DIM), F32)] + kv_shapes,
        scratch_shapes=[pltpu.VMEM((GROUP_DIM // LANES, TM, LANES), F32)],
        compiler_params=_cparams(1, VMEM_LIMIT),
        name="inproj",
    )(x3, mp, ms, mp, ms, gain, w_in_b, qg, kg)


def _rel_bucket_np(dist):
    dist = np.asarray(dist, np.int32)
    max_exact = REL_BUCKETS // 2
    d_f = np.maximum(dist, 1).astype(np.float32)
    ratio = np.log(d_f / np.float32(max_exact)) / np.float32(math.log(REL_MAX_DIST / max_exact))
    large = max_exact + (ratio * np.float32(REL_BUCKETS - max_exact)).astype(np.int32)
    large = np.minimum(large, REL_BUCKETS - 1)
    return np.where(dist < max_exact, dist, large)


def _bias_lookup(rel_bias, g, bucket, ok):
    table = rel_bias[:, g * HEADS:(g + 1) * HEADS].astype(F32)
    vals = jnp.zeros((HEADS,) + bucket.shape, F32)
    for bkt in np.unique(bucket):
        vals = jnp.where((bucket == bkt)[None], table[bkt].reshape((HEADS,) + (1,) * bucket.ndim), vals)
    return jnp.where(ok[None], vals, NEG_INF)


def _prompt_bias_tables(rel_bias):
    qi = np.arange(BAND)[:, None]
    kj = np.arange(2 * BAND)[None, :]
    delta = qi + BAND - kj
    ok = (delta >= 0) & (delta <= BAND)
    ok_first = ok & (kj >= BAND)
    tables = []
    for g, (_, dil) in enumerate(ATT_GROUPS):
        bucket = _rel_bucket_np(np.maximum(delta, 0) * dil)
        both = jnp.stack([_bias_lookup(rel_bias, g, bucket, ok), _bias_lookup(rel_bias, g, bucket, ok_first)])
        tables.append(both.reshape(2, HEADS * BAND, 2 * BAND))
    return tables


def _sample_bias_tables(rel_bias, t_new):
    tables = []
    for g, (win, dil) in enumerate(ATT_GROUPS):
        lb = min(win, PAST_LEN)
        t = np.arange(t_new)[:, None]
        col = np.arange(lb + LANES)[None, :]
        delta = lb + t - col
        ok = (col < lb + t_new) & (delta >= 0) & (delta % dil == 0) & (delta // dil <= win // dil)
        bucket = _rel_bucket_np(np.maximum(delta, 0))
        tables.append(_bias_lookup(rel_bias, g, bucket, ok).reshape(HEADS * t_new, lb + LANES))
    return tables


def _stack_heads(q):
    lh = _lane_head()
    return jnp.concatenate([jnp.where(lh == h, q, jnp.zeros_like(q)) for h in range(HEADS)], axis=0)


def _softmax(s):
    m = jnp.max(s, axis=1, keepdims=True)
    p = jnp.exp(s - m)
    l = jnp.sum(p, axis=1, keepdims=True)
    return p.astype(BF16), l, m + jnp.log(l)


def _unstack_heads(o4, lse4, rows):
    lh = _lane_head()
    o = o4[0:rows]
    lse = jnp.broadcast_to(lse4[0:rows], (rows, GROUP_DIM))
    for h in range(1, HEADS):
        o = jnp.where(lh == h, o4[h * rows:(h + 1) * rows], o)
        lse = jnp.where(lh == h, lse4[h * rows:(h + 1) * rows], lse)
    return o, lse


_NT = (((1,), (1,)), ((), ()))


def _attn_prompt_kernel(nsub, q_ref, kc_ref, kp_ref, vc_ref, vp_ref, bias_ref, o_ref, lse_ref, kbuf, vbuf):
    i = pl.program_id(2)
    kbuf[0:BAND] = kp_ref[...]
    kbuf[BAND:] = kc_ref[...]
    vbuf[0:BAND] = vp_ref[...]
    vbuf[BAND:] = vc_ref[...]
    for s in range(nsub):
        q4 = _stack_heads(q_ref[s * BAND:(s + 1) * BAND])
        kc = kbuf[s * BAND:(s + 2) * BAND]
        vc = vbuf[s * BAND:(s + 2) * BAND]
        logits = lax.dot_general(q4, kc, _NT, preferred_element_type=F32)
        if s == 0:
            bias = bias_ref[jnp.where(i == 0, 1, 0)]
        else:
            bias = bias_ref[0]
        p, l, lse4 = _softmax(logits + bias)
        o4 = jnp.dot(p, vc, preferred_element_type=F32) / l
        o, lse = _unstack_heads(o4, lse4, BAND)
        o_ref[s * BAND:(s + 1) * BAND] = o
        lse_ref[s * BAND:(s + 1) * BAND] = lse


def _attn_prompt(cfg, g, qkv, bias):
    b, s_len = cfg["b"], cfg["s"]
    _, dil = ATT_GROUPS[g]
    l_len = s_len // dil
    tq = min(TM, l_len)
    nsub = tq // BAND
    nq = l_len // tq

    def cur(which):
        return pl.BlockSpec((None, None, None, tq, GROUP_DIM), lambda bi, r, i: (which, bi, r, i, 0))

    def prev(which):
        return pl.BlockSpec((None, None, None, BAND, GROUP_DIM),
                            lambda bi, r, i: (which, bi, r, jnp.maximum(i * nsub - 1, 0), 0))

    out = pl.BlockSpec((None, None, tq, GROUP_DIM), lambda bi, r, i: (bi, r, i, 0))
    return pl.pallas_call(
        functools.partial(_attn_prompt_kernel, nsub),
        grid=(b, dil, nq),
        in_specs=[cur(0), cur(1), prev(1), cur(2), prev(2),
                  pl.BlockSpec(bias.shape, lambda bi, r, i: (0, 0, 0))],
        out_specs=[out, out],
        out_shape=[jax.ShapeDtypeStruct((b, dil, l_len, GROUP_DIM), F32)] * 2,
        scratch_shapes=[pltpu.VMEM((tq + BAND, GROUP_DIM), BF16)] * 2,
        compiler_params=_cparams(3, VMEM_LIMIT),
        name=f"attn_prompt_g{g}",
    )(qkv, qkv, qkv, qkv, qkv, bias)


def _attn_sample_kernel(bs, lb, t_new, q_ref, kvn_ref, cache_ref, bias_ref, o_ref, lse_ref):
    pad = jnp.zeros((LANES - t_new, GROUP_DIM), BF16)
    for j in range(bs):
        kt = cache_ref[j, 0:GROUP_DIM, :].astype(BF16)
        vt = cache_ref[j, GROUP_DIM:2 * GROUP_DIM, :].astype(BF16)
        kvn = kvn_ref[j].astype(BF16)
        kn = jnp.concatenate([kvn[:, :GROUP_DIM], pad], axis=0)
        vn = jnp.concatenate([kvn[:, GROUP_DIM:], pad], axis=0)
        q4 = _stack_heads(q_ref[j].astype(BF16))
        logits = jnp.concatenate(
            [jnp.dot(q4, kt, preferred_element_type=F32),
             lax.dot_general(q4, kn, _NT, preferred_element_type=F32)], axis=1)
        p, l, lse4 = _softmax(logits + bias_ref[...])
        o4 = (lax.dot_general(p[:, :lb], vt, _NT, preferred_element_type=F32)
              + jnp.dot(p[:, lb:], vn, preferred_element_type=F32)) / l
        o, lse = _unstack_heads(o4, lse4, t_new)
        o_ref[j] = o
        lse_ref[j] = lse


def _attn_sample(cfg, g, layer, qs, kvs, cache_t, bias):
    bd, t_new = cfg["bd"], cfg["t"]
    lb = cache_t.shape[-1]
    bs = max(1, min(8, (4 * 1024 * 1024) // (2 * GROUP_DIM * lb * 4)))
    out = pl.BlockSpec((bs, t_new, GROUP_DIM), lambda i: (i, 0, 0))
    o, lse = pl.pallas_call(
        functools.partial(_attn_sample_kernel, bs, lb, t_new),
        grid=(bd // bs,),
        in_specs=[pl.BlockSpec((None, bs, t_new, GROUP_DIM), lambda i: (g, i, 0, 0)),
                  pl.BlockSpec((None, bs, t_new, 2 * GROUP_DIM), lambda i: (g, i, 0, 0)),
                  pl.BlockSpec((None, bs, 2 * GROUP_DIM, lb), lambda i: (layer, i, 0, 0)),
                  pl.BlockSpec(bias.shape, lambda i: (0, 0))],
        out_specs=[out, out],
        out_shape=[jax.ShapeDtypeStruct((bd, t_new, GROUP_DIM), F32)] * 2,
        compiler_params=_cparams(1, VMEM_LIMIT),
        name=f"attn_sample_g{g}",
    )(qs.reshape(N_GROUPS, bd, t_new, GROUP_DIM), kvs.reshape(N_GROUPS, bd, t_new, 2 * GROUP_DIM),
      cache_t, bias)
    return o.reshape(bd * t_new, GROUP_DIM), lse.reshape(bd * t_new, GROUP_DIM)


def _pool_mix(ext_ref, rows, pos, w_ref, scale_ref, o_ref):
    lo = POOL_PAD
    u = ext_ref[:, lo:lo + rows, :]
    acc = u
    sums = {}
    for j in range(1, max(POOL_WINDOWS)):
        acc = acc + ext_ref[:, lo - j:lo - j + rows, :]
        if j + 1 in POOL_WINDOWS:
            sums[j + 1] = acc
    lane_grp = lax.broadcasted_iota(jnp.int32, (1, 1, POOL_DIM), 2) // (POOL_DIM // len(POOL_WINDOWS))
    z = None
    for gi, w in enumerate(POOL_WINDOWS):
        cnt = jnp.minimum(w, pos + 1).astype(F32)
        zw = sums[w] / cnt - u
        z = zw if z is None else jnp.where(lane_grp == gi, zw, z)
    nb = u.shape[0]
    y = jnp.dot(z.reshape(nb * rows, POOL_DIM).astype(BF16), w_ref[...], preferred_element_type=F32)
    o_ref[...] = (y * scale_ref[...]).astype(BF16)


def _pool_prompt_kernel(u_ref, halo_ref, w_ref, scale_ref, o_ref, ext_ref):
    i = pl.program_id(1)
    halo = halo_ref[...]
    ext_ref[:, 0:POOL_PAD, :] = jnp.where(i == 0, jnp.zeros_like(halo), halo)
    ext_ref[:, POOL_PAD:, :] = u_ref[...]
    pos = i * TM + lax.broadcasted_iota(jnp.int32, (1, TM, 1), 1)
    _pool_mix(ext_ref, TM, pos, w_ref, scale_ref, o_ref)


def _pool_sample_kernel(t_new, u_ref, hist_ref, w_ref, scale_ref, o_ref, ext_ref):
    nb = u_ref.shape[0]
    ext_ref[:, 0:POOL_PAD - POOL_HIST, :] = jnp.zeros((nb, POOL_PAD - POOL_HIST, POOL_DIM), F32)
    ext_ref[:, POOL_PAD - POOL_HIST:POOL_PAD, :] = hist_ref[...]
    ext_ref[:, POOL_PAD:, :] = u_ref[...]
    pos = PAST_LEN + lax.broadcasted_iota(jnp.int32, (1, t_new, 1), 1)
    _pool_mix(ext_ref, t_new, pos, w_ref, scale_ref, o_ref)


def _pool_prompt(cfg, u_p, w_bd, scale):
    b, s_len = cfg["b"], cfg["s"]
    u3 = u_p.reshape(b, s_len, POOL_DIM)
    per = TM // POOL_PAD
    return pl.pallas_call(
        _pool_prompt_kernel,
        grid=(b, s_len // TM),
        in_specs=[pl.BlockSpec((1, TM, POOL_DIM), lambda bi, i: (bi, i, 0)),
                  pl.BlockSpec((1, POOL_PAD, POOL_DIM), lambda bi, i: (bi, jnp.maximum(i * per - 1, 0), 0)),
                  pl.BlockSpec((POOL_DIM, POOL_DIM), lambda bi, i: (0, 0)),
                  pl.BlockSpec((1, POOL_DIM), lambda bi, i: (0, 0))],
        out_specs=pl.BlockSpec((TM, POOL_DIM), lambda bi, i: (bi * (s_len // TM) + i, 0)),
        out_shape=jax.ShapeDtypeStruct((b * s_len, POOL_DIM), BF16),
        scratch_shapes=[pltpu.VMEM((1, TM + POOL_PAD, POOL_DIM), F32)],
        compiler_params=_cparams(2, VMEM_LIMIT),
        name="pool_prompt",
    )(u3, u3, w_bd, scale)


def _pool_sample(cfg, layer, u_s, state_pool, w_bd, scale):
    bd, t_new = cfg["bd"], cfg["t"]
    u3 = u_s.reshape(bd, t_new, POOL_DIM)
    nb = GP
    return pl.pallas_call(
        functools.partial(_pool_sample_kernel, t_new),
        grid=(bd // nb,),
        in_specs=[pl.BlockSpec((nb, t_new, POOL_DIM), lambda i: (i, 0, 0)),
                  pl.BlockSpec((None, nb, POOL_HIST, POOL_DIM), lambda i: (layer, i, 0, 0)),
                  pl.BlockSpec((POOL_DIM, POOL_DIM), lambda i: (0, 0)),
                  pl.BlockSpec((1, POOL_DIM), lambda i: (0, 0))],
        out_specs=pl.BlockSpec((nb * t_new, POOL_DIM), lambda i: (i, 0)),
        out_shape=jax.ShapeDtypeStruct((bd * t_new, POOL_DIM), BF16),
        scratch_shapes=[pltpu.VMEM((nb, t_new + POOL_PAD, POOL_DIM), F32)],
        compiler_params=_cparams(1, VMEM_LIMIT),
        name="pool_sample",
    )(u3, state_pool, w_bd, scale)


def _outproj_kernel(cfg, x_ref, g1p_ref, g1s_ref, shp_ref, shs_ref, scp_ref, scs_ref, pp_ref, ps_ref, *rest):
    att_refs = rest[:4 * N_GROUPS]
    (gain_ref, wo_ref, wr_ref, br_ref, x1_ref, h2_ref, route_ref, cnt_ref,
     run_ref, nat_ref) = rest[4 * N_GROUPS:]
    i = pl.program_id(0)
    n_pt, d = cfg["n_pt"], cfg["d"]
    is_p = i < n_pt

    halves = GROUP_DIM // LANES

    def put(slot, val, rows=None):
        for half in range(halves):
            piece = val[:, half * LANES:(half + 1) * LANES]
            if rows is None:
                nat_ref[slot, half] = piece
            else:
                nat_ref[slot, half, rows, :] = piece

    @pl.when(is_p)
    def _():
        for g, (_, dil) in enumerate(ATT_GROUPS):
            for which in range(2):
                src = att_refs[4 * g + which]
                if dil == 1:
                    put(2 * g + which, src[0])
                else:
                    for r in range(dil):
                        put(2 * g + which, src[r], pl.ds(r, TM // dil, stride=dil))

    @pl.when(jnp.logical_not(is_p))
    def _():
        for g in range(N_GROUPS):
            for which in range(2):
                put(2 * g + which, att_refs[4 * g + 2 + which][...])

    def get(slot):
        return jnp.concatenate([nat_ref[slot, half] for half in range(halves)], axis=1)

    o = [get(2 * g) for g in range(N_GROUPS)]
    lse = [get(2 * g + 1) for g in range(N_GROUPS)]
    mx = functools.reduce(jnp.maximum, lse)
    e = [jnp.exp(l - mx) for l in lse]
    attn = sum(eg * og for eg, og in zip(e, o)) / sum(e)
    pool = jnp.where(is_p, pp_ref[...], ps_ref[...])
    cat = jnp.concatenate([pool, attn.astype(BF16)], axis=1)
    y = jnp.dot(cat, wo_ref[...], preferred_element_type=F32)

    g1 = jnp.where(is_p, g1p_ref[...], g1s_ref[...])
    x1 = x_ref[...] + g1 * y.reshape(GP, SUBLANES, d)
    x1_ref[...] = x1
    sh = jnp.where(is_p, shp_ref[...], shs_ref[...])
    sc = jnp.where(is_p, scp_ref[...], scs_ref[...])
    h2 = _rmsnorm_mod(x1, gain_ref[...], sc, sh).reshape(TM, d)
    _rows_to_slabs(h2_ref, h2)

    h_hi = h2.astype(BF16)
    h_lo = (h2 - h_hi.astype(F32)).astype(BF16)
    wr = wr_ref[...]
    w_hi = wr.astype(BF16)
    w_lo = (wr - w_hi.astype(F32)).astype(BF16)
    logits = (jnp.dot(h_hi, w_hi, preferred_element_type=F32)
              + jnp.dot(h_lo, w_hi, preferred_element_type=F32)
              + jnp.dot(h_hi, w_lo, preferred_element_type=F32)) + br_ref[...]

    lane = lax.broadcasted_iota(jnp.int32, (TM, LANES), 1).astype(F32)
    vals = logits
    top_v, top_i = [], []
    onehot = jnp.zeros((TM, LANES), F32)
    for _ in range(TOP_K):
        m = jnp.max(vals, axis=1, keepdims=True)
        idx = jnp.min(jnp.where(vals == m, lane, float(LANES)), axis=1, keepdims=True)
        hit = lane == idx
        vals = jnp.where(hit, -jnp.inf, vals)
        onehot = jnp.where(hit, 1.0, onehot)
        top_v.append(m)
        top_i.append(idx)
    ev = [jnp.exp(v - top_v[0]) for v in top_v]
    den = sum(ev)

    @pl.when(i == 0)
    def _():
        run_ref[...] = jnp.zeros_like(run_ref)

    row = lax.broadcasted_iota(jnp.int32, (TM, TM), 0)
    col = lax.broadcasted_iota(jnp.int32, (TM, TM), 1)
    before = (col < row).astype(BF16)
    rank_all = jnp.dot(before, onehot.astype(BF16), preferred_element_type=F32) + run_ref[...]
    route = jnp.zeros((TM, LANES), F32)
    for k in range(TOP_K):
        rank_k = jnp.sum(jnp.where(lane == top_i[k], rank_all, 0.0), axis=1, keepdims=True)
        route = jnp.where(lane == float(k), top_i[k], route)
        route = jnp.where(lane == float(TOP_K + k), ev[k] / den, route)
        route = jnp.where(lane == float(2 * TOP_K + k), rank_k, route)
    route_ref[...] = route
    run_ref[...] = run_ref[...] + jnp.sum(onehot, axis=0, keepdims=True)
    cnt_ref[...] = run_ref[...]


def _outproj(cfg, x3, mp, ms, pool_p, pool_s, att, gain, w_out_b, w_r, b_r):
    d, n_pt, n_t = cfg["d"], cfg["n_pt"], cfg["n_t"]
    n = cfg["n"]
    tps = cfg["tps"]
    pidx = lambda i: (jnp.minimum(i, n_pt - 1), 0)
    sidx = lambda i: (jnp.maximum(i - n_pt, 0), 0)
    pspec = pl.BlockSpec((TM, GROUP_DIM), pidx)
    sspec = pl.BlockSpec((TM, GROUP_DIM), sidx)
    att_specs, att_args = [], []
    for (_, dil), (o_p, lse_p, o_s, lse_s) in zip(ATT_GROUPS, att):
        rspec = pl.BlockSpec((None, dil, TM // dil, GROUP_DIM),
                             lambda i: (pidx(i)[0] // tps, 0, pidx(i)[0] % tps, 0))
        att_specs += [rspec, rspec, sspec, sspec]
        att_args += [o_p, lse_p, o_s, lse_s]
    full = lambda a: pl.BlockSpec(a.shape, lambda i: (0,) * a.ndim)
    return pl.pallas_call(
        functools.partial(_outproj_kernel, cfg),
        grid=(n_t,),
        in_specs=[pl.BlockSpec((GP, SUBLANES, d), lambda i: (i, 0, 0))]
        + _mod_specs(cfg, 2) + _mod_specs(cfg, 3) + _mod_specs(cfg, 4)
        + [pspec, sspec] + att_specs + [full(gain), full(w_out_b), full(w_r), full(b_r)],
        out_specs=[pl.BlockSpec((GP, SUBLANES, d), lambda i: (i, 0, 0)),
                   pl.BlockSpec((TM * SUBLANES, LANES), lambda i: (i, 0)),
                   pl.BlockSpec((TM, LANES), lambda i: (i, 0)),
                   pl.BlockSpec((1, LANES), lambda i: (0, 0))],
        out_shape=[jax.ShapeDtypeStruct(x3.shape, F32),
                   jax.ShapeDtypeStruct((n * SUBLANES, LANES), F32),
                   jax.ShapeDtypeStruct((n, LANES), F32),
                   jax.ShapeDtypeStruct((1, LANES), F32)],
        scratch_shapes=[pltpu.VMEM((1, LANES), F32),
                        pltpu.VMEM((2 * N_GROUPS, GROUP_DIM // LANES, TM, LANES), F32)],
        compiler_params=_cparams(1, VMEM_LIMIT),
        name="outproj_router",
    )(x3, mp, ms, mp, ms, mp, ms, pool_p, pool_s, *att_args, gain, w_out_b, w_r, b_r)


def _rows_from_slabs(ref):
    rows = ref.shape[0] // SUBLANES
    return jnp.concatenate([ref[pl.ds(j, rows, stride=SUBLANES), :] for j in range(SUBLANES)], axis=1)


def _rows_to_slabs(ref, val):
    for j in range(SUBLANES):
        ref[pl.ds(j, val.shape[0], stride=SUBLANES), :] = val[:, j * LANES:(j + 1) * LANES]


def _dispatch_kernel(zt_ref, pos_ref, h_ref, x_hbm, zbuf, zsem, sem):
    i = pl.program_id(0)
    rows = h_ref.shape[0] // SUBLANES

    @pl.when(i == 0)
    def _():
        zbuf[...] = jnp.zeros_like(zbuf)

        def fill(z):
            return pltpu.make_async_copy(zbuf, x_hbm.at[pl.ds(zt_ref[z] * MOE_TM, MOE_TM)], zsem)

        for z in range(zt_ref.shape[0]):
            pl.when(zt_ref[z] >= 0)(lambda z=z: fill(z).start())
        for z in range(zt_ref.shape[0]):
            pl.when(zt_ref[z] >= 0)(lambda z=z: fill(z).wait())

    for r in range(rows):
        for k in range(TOP_K):
            pltpu.make_async_copy(h_ref.at[pl.ds(r * SUBLANES, SUBLANES)], x_hbm.at[pos_ref[0, r * TOP_K + k]],
                                  sem.at[k]).start(priority=k % 2)
    for k in range(TOP_K):
        pltpu.make_async_copy(h_ref, h_ref, sem.at[k]).wait()


def _dispatch(cfg, pos, zero_tiles, h2, m_pad):
    n_t = cfg["n_t"]
    grid_spec = pltpu.PrefetchScalarGridSpec(
        num_scalar_prefetch=1,
        grid=(n_t,),
        in_specs=[pl.BlockSpec((None, 1, TM * TOP_K), lambda i, zt: (i, 0, 0), memory_space=pltpu.SMEM),
                  pl.BlockSpec((TM * SUBLANES, LANES), lambda i, zt: (i, 0))],
        out_specs=pl.BlockSpec(memory_space=pl.ANY),
        scratch_shapes=[pltpu.VMEM((MOE_TM, SUBLANES, LANES), F32),
                        pltpu.SemaphoreType.DMA(()),
                        pltpu.SemaphoreType.DMA((TOP_K,))],
    )
    return pl.pallas_call(
        _dispatch_kernel,
        grid_spec=grid_spec,
        out_shape=jax.ShapeDtypeStruct((m_pad, SUBLANES, LANES), F32),
        compiler_params=_cparams(1, VMEM_LIMIT),
        name="moe_dispatch",
    )(zero_tiles, pos.reshape(n_t, 1, TM * TOP_K), h2)


N_YBUF = 3


def _moe_kernel(te_ref, prv_ref, last_ref, x_ref, w1_ref, b1_ref, w2_ref, b2_ref, y_hbm,
                y0, y1, y2, w1b, w2b, ssem):
    i = pl.program_id(0)
    n = pl.num_programs(0)
    ff = w2_ref.shape[0]
    ys = (y0, y1, y2)

    def start_scatter(idx_ref, q):
        for r in range(MOE_TM):
            pltpu.make_async_copy(ys[q].at[pl.ds(r * SUBLANES, SUBLANES)], y_hbm.at[idx_ref[0, r]],
                                  ssem.at[q]).start(priority=1)

    def wait_scatter(q):
        pltpu.make_async_copy(ys[q], ys[q], ssem.at[q]).wait()

    @pl.when(i == 0)
    def _():
        n_asg = y_hbm.shape[0] - 2 * MOE_TM
        for q in range(N_YBUF):
            ys[q][...] = jnp.zeros_like(ys[q])
        for q in range(2):
            for r in range(MOE_TM):
                pltpu.make_async_copy(ys[q].at[pl.ds(r * SUBLANES, SUBLANES)],
                                      y_hbm.at[n_asg + q * MOE_TM + r], ssem.at[q]).start()
            wait_scatter(q)

    @pl.when(jnp.logical_or(i == 0, te_ref[i] != te_ref[jnp.maximum(i - 1, 0)]))
    def _():
        w1b[...] = w1_ref[...].astype(BF16)
        w2b[...] = w2_ref[...].astype(BF16)

    def step(q):
        q_prev, q_old = (q + 2) % N_YBUF, (q + 1) % N_YBUF
        start_scatter(prv_ref, q_prev)
        x = _rows_from_slabs(x_ref).astype(BF16)
        h1 = jnp.dot(x, w1b[...], preferred_element_type=F32) + b1_ref[...]
        gate = jnp.minimum(h1[:, :ff], SWIGLU_LIMIT)
        up = jnp.clip(h1[:, ff:], -SWIGLU_LIMIT, SWIGLU_LIMIT)
        act = gate * jax.nn.sigmoid(SWIGLU_ALPHA * gate) * (up + 1.0)
        _rows_to_slabs(ys[q], jnp.dot(act.astype(BF16), w2b[...], preferred_element_type=F32) + b2_ref[...])

        @pl.when(i >= 1)
        def _():
            wait_scatter(q_old)

        @pl.when(i == n - 1)
        def _():
            wait_scatter(q_prev)
            start_scatter(last_ref, q)
            wait_scatter(q)

    for q in range(N_YBUF):
        pl.when(i % N_YBUF == q)(functools.partial(step, q))


def _moe(layer, n_tiles, tile_expert, slot_dst, x_sorted, w1, b1, w2, b2, n_rows_out):
    depth, n_exp, d, ff2 = w1.shape
    ff = w2.shape[2]
    smem_tile = lambda f: pl.BlockSpec((None, 1, MOE_TM), f, memory_space=pltpu.SMEM)
    grid_spec = pltpu.PrefetchScalarGridSpec(
        num_scalar_prefetch=1,
        grid=(n_tiles,),
        in_specs=[smem_tile(lambda i, te: (i, 0, 0)),
                  smem_tile(lambda i, te: (n_tiles, 0, 0)),
                  pl.BlockSpec((MOE_TM * SUBLANES, LANES), lambda i, te: (i, 0)),
                  pl.BlockSpec((None, None, d, ff2), lambda i, te: (layer, te[i], 0, 0)),
                  pl.BlockSpec((None, None, 1, ff2), lambda i, te: (layer, te[i], 0, 0)),
                  pl.BlockSpec((None, None, ff, d), lambda i, te: (layer, te[i], 0, 0)),
                  pl.BlockSpec((None, None, 1, d), lambda i, te: (layer, te[i], 0, 0))],
        out_specs=pl.BlockSpec(memory_space=pl.ANY),
        scratch_shapes=[pltpu.VMEM((MOE_TM * SUBLANES, LANES), F32)] * N_YBUF
        + [pltpu.VMEM((d, ff2), BF16),
           pltpu.VMEM((ff, d), BF16),
           pltpu.SemaphoreType.DMA((N_YBUF,))],
    )
    lead = n_rows_out - MOE_TM + jnp.arange(MOE_TM, dtype=jnp.int32)
    dst3 = jnp.concatenate([lead, slot_dst]).reshape(n_tiles + 1, 1, MOE_TM)
    return pl.pallas_call(
        _moe_kernel,
        grid_spec=grid_spec,
        out_shape=jax.ShapeDtypeStruct((n_rows_out, SUBLANES, LANES), F32),
        compiler_params=_cparams(1, VMEM_LIMIT),
        name="moe_experts",
    )(tile_expert, dst3, dst3, x_sorted.reshape(x_sorted.shape[0] * SUBLANES, LANES),
      w1, b1.reshape(depth, n_exp, 1, ff2), w2, b2.reshape(depth, n_exp, 1, d))


def _moe_plan(cfg, route, counts):
    n = cfg["n"]
    m = n * TOP_K
    n_tiles = m // MOE_TM + N_EXPERTS
    m_pad = n_tiles * MOE_TM
    top_i = route[:, 0:TOP_K].astype(jnp.int32)
    rank = route[:, 2 * TOP_K:3 * TOP_K].astype(jnp.int32)
    cnt = counts[0, :N_EXPERTS].astype(jnp.int32)
    tiles_e = (cnt + MOE_TM - 1) // MOE_TM
    tile_end = jnp.cumsum(tiles_e)
    pstart = (tile_end - tiles_e) * MOE_TM
    expert = jnp.arange(N_EXPERTS, dtype=jnp.int32)
    pos = jnp.sum(jnp.where(top_i[..., None] == expert, pstart, 0), axis=-1) + rank
    t = jnp.arange(n_tiles, dtype=jnp.int32)
    te = jnp.sum((tile_end[None, :] <= t[:, None]).astype(jnp.int32), axis=1)
    last_used = jnp.sum((tile_end <= tile_end[-1] - 1).astype(jnp.int32))
    te = jnp.minimum(te, last_used).astype(jnp.int32)
    s = jnp.arange(m_pad, dtype=jnp.int32)
    dump = m + ((s // MOE_TM) % 2) * MOE_TM + s % MOE_TM
    slot_dst = dump.at[pos.T.reshape(-1)].set(jnp.arange(m, dtype=jnp.int32))
    n_used = tile_end[-1]
    spare = n_used + jnp.arange(n_tiles - m // MOE_TM, dtype=jnp.int32)
    zero_tiles = jnp.concatenate([jnp.where(tiles_e > 0, tile_end - 1, -1),
                                  jnp.where(spare < n_tiles, spare, -1)]).astype(jnp.int32)
    return n_tiles, te, pos, zero_tiles, slot_dst, m + 2 * MOE_TM


def _combine_kernel(cfg, x_ref, g2p_ref, g2s_ref, route_ref, *rest):
    y_refs, o_ref = rest[:TOP_K], rest[TOP_K]
    i = pl.program_id(0)
    is_p = i < cfg["n_pt"]
    route = route_ref[...]
    acc = route[:, TOP_K:TOP_K + 1] * _rows_from_slabs(y_refs[0])
    for k in range(1, TOP_K):
        acc = acc + route[:, TOP_K + k:TOP_K + k + 1] * _rows_from_slabs(y_refs[k])
    g2 = jnp.where(is_p, g2p_ref[...], g2s_ref[...])
    o_ref[...] = x_ref[...] + g2 * acc.reshape(GP, SUBLANES, cfg["d"])


def _combine(cfg, x1, mp, ms, route, ybuf):
    d, n_t = cfg["d"], cfg["n_t"]
    y_specs = [pl.BlockSpec((TM * SUBLANES, LANES), lambda i, k=k: (k * n_t + i, 0)) for k in range(TOP_K)]
    ybuf = ybuf.reshape(ybuf.shape[0] * SUBLANES, LANES)
    return pl.pallas_call(
        functools.partial(_combine_kernel, cfg),
        grid=(n_t,),
        in_specs=[pl.BlockSpec((GP, SUBLANES, d), lambda i: (i, 0, 0))] + _mod_specs(cfg, 5)
        + [pl.BlockSpec((TM, LANES), lambda i: (i, 0))] + y_specs,
        out_specs=pl.BlockSpec((GP, SUBLANES, d), lambda i: (i, 0, 0)),
        out_shape=jax.ShapeDtypeStruct(x1.shape, F32),
        compiler_params=_cparams(1, VMEM_LIMIT),
        name="moe_combine",
    )(x1, mp, ms, route, *([ybuf] * TOP_K))


def kernel(x_prompt, x_sample, cache_kv_w128_d1, cache_kv_w512_d4, cache_kv_w2048_d16, state_pool,
           c_prompt, c_sample, rel_bias, norm_mix, norm_ffn, w_ada, b_ada, w_in, q_norm, k_norm,
           pool_w, pool_scale, w_out, w_router, b_router, w_expert_in, b_expert_in,
           w_expert_out, b_expert_out):
    b, s_len, d = x_prompt.shape
    bd, t_new, _ = x_sample.shape
    depth = w_in.shape[0]
    np_, ns = b * s_len, bd * t_new
    assert t_new == SUBLANES and s_len % TM == 0 and ns % TM == 0
    assert all((s_len // dil) % BAND == 0 for _, dil in ATT_GROUPS)
    cfg = dict(b=b, s=s_len, d=d, bd=bd, t=t_new, np=np_, ns=ns, n=np_ + ns,
               n_pt=np_ // TM, n_t=(np_ + ns) // TM, tps=s_len // TM)
    kv_keep = []
    for win, _ in ATT_GROUPS:
        keep = min(win, s_len)
        assert keep % TM == 0 or TM % keep == 0
        kv_keep.append((keep, max(keep // TM, 1)))
    cfg["kv_keep"] = tuple(kv_keep)
    caches_t = [jnp.transpose(c, (0, 1, 3, 4, 5, 2)).reshape(depth, bd, 2 * GROUP_DIM, c.shape[2])
                for c in (cache_kv_w128_d1, cache_kv_w512_d4, cache_kv_w2048_d16)]

    mods = _ada_mods(jnp.concatenate([c_prompt, c_sample], axis=0), w_ada, b_ada)
    bias_p = _prompt_bias_tables(rel_bias)
    bias_s = _sample_bias_tables(rel_bias, t_new)
    w_in_b = w_in.astype(BF16)
    w_out_b = w_out.astype(BF16)
    eye = jnp.eye(len(POOL_WINDOWS), dtype=F32)
    w_pool_bd = (eye[None, :, None, :, None] * pool_w[:, :, :, None, :]).reshape(depth, POOL_DIM, POOL_DIM)
    w_pool_bd = w_pool_bd.astype(BF16)
    w_r_pad = jnp.pad(w_router, ((0, 0), (0, 0), (0, LANES - N_EXPERTS)))
    b_r_pad = jnp.pad(b_router, ((0, 0), (0, LANES - N_EXPERTS)), constant_values=NEG_INF)

    x3 = jnp.concatenate([x_prompt.reshape(np_, d), x_sample.reshape(ns, d)], axis=0)
    x3 = x3.reshape((np_ + ns) // SUBLANES, SUBLANES, d)

    kv_p = [[] for _ in ATT_GROUPS]
    kv_s = [[] for _ in ATT_GROUPS]
    pool_p_state, pool_s_state = [], []
    for l in range(depth):
        mp = mods[l, :b].reshape(b, 1, 6 * d)
        ms = mods[l, b:].reshape(bd, 1, 6 * d)
        qg = jnp.tile(q_norm[l], HEADS).reshape(1, GROUP_DIM)
        kg = jnp.tile(k_norm[l], HEADS).reshape(1, GROUP_DIM)
        (u_p, u_s, qkv0, qkv1, qkv2, qs, kvs, *kvp) = _inproj(
            cfg, x3, mp, ms, norm_mix[l].reshape(1, d), w_in_b[l], qg, kg)

        att = []
        for g, qkv in enumerate((qkv0, qkv1, qkv2)):
            o_p, lse_p = _attn_prompt(cfg, g, qkv, bias_p[g])
            o_s, lse_s = _attn_sample(cfg, g, l, qs, kvs, caches_t[g], bias_s[g])
            att.append((o_p, lse_p, o_s, lse_s))
            keep = cfg["kv_keep"][g][0]
            kv_p[g].append(kvp[g].reshape(b, keep, 2, HEADS, HEAD_DIM))
            kv_s[g].append(kvs[g].reshape(bd, t_new, 2, HEADS, HEAD_DIM))

        scale = pool_scale[l].reshape(1, POOL_DIM)
        pool_p = _pool_prompt(cfg, u_p, w_pool_bd[l], scale)
        pool_s = _pool_sample(cfg, l, u_s, state_pool, w_pool_bd[l], scale)
        pool_p_state.append(u_p.reshape(b, s_len, POOL_DIM)[:, s_len - POOL_HIST:])
        pool_s_state.append(jnp.concatenate(
            [state_pool[l], u_s.reshape(bd, t_new, POOL_DIM)], axis=1)[:, -POOL_HIST:])

        x1, h2, route, counts = _outproj(cfg, x3, mp, ms, pool_p, pool_s, att,
                                         norm_ffn[l].reshape(1, d), w_out_b[l], w_r_pad[l],
                                         b_r_pad[l].reshape(1, LANES))
        n_tiles, te, pos, zero_tiles, slot_dst, n_rows_out = _moe_plan(cfg, route, counts)
        x_sorted = _dispatch(cfg, pos, zero_tiles, h2, n_tiles * MOE_TM)
        ybuf = _moe(l, n_tiles, te, slot_dst, x_sorted,
                    w_expert_in, b_expert_in, w_expert_out, b_expert_out, n_rows_out)
        x3 = _combine(cfg, x1, mp, ms, route, ybuf)

    x_all = x3.reshape(np_ + ns, d)
    y_prompt = x_all[:np_].reshape(b, s_len, d)
    y_sample = x_all[np_:].reshape(bd, t_new, d)
    return (y_prompt, y_sample,
            jnp.stack(kv_p[0]), jnp.stack(kv_p[1]), jnp.stack(kv_p[2]), jnp.stack(pool_p_state),
            jnp.stack(kv_s[0]), jnp.stack(kv_s[1]), jnp.stack(kv_s[2]), jnp.stack(pool_s_state))
```

```python
import functools
import math

import numpy as np
import jax
import jax.numpy as jnp
from jax import lax
from jax.experimental import pallas as pl
from jax.experimental.pallas import tpu as pltpu

F32 = jnp.float32
BF16 = jnp.bfloat16

HEAD_DIM = 64
HEADS = 4
GROUP_DIM = HEADS * HEAD_DIM
ATT_GROUPS = ((128, 1), (512, 4), (2048, 16))
N_GROUPS = len(ATT_GROUPS)
BAND = 128
POOL_WINDOWS = (2, 4, 8, 16)
POOL_HIST = 15
POOL_DIM = 256
POOL_PAD = 16
REL_BUCKETS = 32
REL_MAX_DIST = 2048
N_EXPERTS = 32
TOP_K = 4
SWIGLU_LIMIT = 7.0
SWIGLU_ALPHA = 1.702
EPS = 1e-6
NEG_INF = -1e30
PAST_LEN = 2048

SUBLANES = 8
LANES = 128
TM = 512
GP = TM // SUBLANES
MOE_TM = 256
VMEM_LIMIT = 52 * 1024 * 1024


def _cparams(n_axes, vmem=None):
    return pltpu.CompilerParams(dimension_semantics=("arbitrary",) * n_axes,
                                vmem_limit_bytes=vmem)


def _lane_head(width=GROUP_DIM):
    return lax.broadcasted_iota(jnp.int32, (1, width), 1) // HEAD_DIM


def _ada_kernel(c_ref, w_ref, b_ref, o_ref):
    c = c_ref[...]
    a = (c * jax.nn.sigmoid(c)).astype(BF16)
    o_ref[...] = jnp.dot(a, w_ref[...].astype(BF16), preferred_element_type=F32) + b_ref[...]


def _ada_mods(c_all, w_ada, b_ada):
    depth, d, d6 = w_ada.shape
    bc = c_all.shape[0]
    tn = d6 // 4
    return pl.pallas_call(
        _ada_kernel,
        grid=(depth, d6 // tn),
        in_specs=[pl.BlockSpec((bc, d), lambda l, j: (0, 0)),
                  pl.BlockSpec((None, d, tn), lambda l, j: (l, 0, j)),
                  pl.BlockSpec((None, 1, tn), lambda l, j: (l, 0, j))],
        out_specs=pl.BlockSpec((None, bc, tn), lambda l, j: (l, 0, j)),
        out_shape=jax.ShapeDtypeStruct((depth, bc, d6), F32),
        compiler_params=_cparams(2, VMEM_LIMIT),
        name="ada_mods",
    )(c_all, w_ada, b_ada.reshape(depth, 1, d6))


def _mod_specs(cfg, col):
    d = cfg["d"]
    n_pt, tps, b = cfg["n_pt"], cfg["tps"], cfg["b"]
    return [pl.BlockSpec((1, 1, d), lambda i: (jnp.minimum(i // tps, b - 1), 0, col)),
            pl.BlockSpec((GP, 1, d), lambda i: (jnp.maximum(i - n_pt, 0), 0, col))]


def _rmsnorm_mod(x, g, sc, sh):
    inv = lax.rsqrt(jnp.mean(x * x, axis=-1, keepdims=True) + EPS)
    return x * inv * g * (1.0 + sc) + sh


def _inproj_kernel(cfg, x_ref, shp_ref, shs_ref, scp_ref, scs_ref, g_ref, w_ref, qg_ref, kg_ref,
                   up_ref, us_ref, qkv0_ref, qkv1_ref, qkv2_ref, qs_ref, kvs_ref,
                   kvp0_ref, kvp1_ref, kvp2_ref, res_ref):
    qkv_refs = (qkv0_ref, qkv1_ref, qkv2_ref)
    kvp_refs = (kvp0_ref, kvp1_ref, kvp2_ref)
    i = pl.program_id(0)
    n_pt, tps, d = cfg["n_pt"], cfg["tps"], cfg["d"]
    is_p = i < n_pt
    sh = jnp.where(is_p, shp_ref[...], shs_ref[...])
    sc = jnp.where(is_p, scp_ref[...], scs_ref[...])
    h = _rmsnorm_mod(x_ref[...], g_ref[...], sc, sh)
    z = jnp.dot(h.reshape(TM, d).astype(BF16), w_ref[...], preferred_element_type=F32)

    r = lax.broadcasted_iota(jnp.int32, (GROUP_DIM, GROUP_DIM), 0) // HEAD_DIM
    c = lax.broadcasted_iota(jnp.int32, (GROUP_DIM, GROUP_DIM), 1) // HEAD_DIM
    head_ones = (r == c).astype(BF16)

    def head_norm(t, gain):
        ss = jnp.dot((t * t).astype(BF16), head_ones, preferred_element_type=F32) * (1.0 / HEAD_DIM)
        return t * lax.rsqrt(ss + EPS) * gain

    u = z[:, :POOL_DIM]
    att = d - POOL_DIM
    q, k, v = [], [], []
    for g in range(N_GROUPS):
        lo = POOL_DIM + g * GROUP_DIM
        q.append(head_norm(z[:, lo:lo + GROUP_DIM], qg_ref[...]) * (HEAD_DIM ** -0.5))
        k.append(head_norm(z[:, lo + att:lo + att + GROUP_DIM], kg_ref[...]))
        v.append(z[:, lo + 2 * att:lo + 2 * att + GROUP_DIM])

    @pl.when(is_p)
    def _():
        up_ref[...] = u
        for g, (_, dil) in enumerate(ATT_GROUPS):
            for which, val in enumerate((q[g], k[g], v[g])):
                if dil == 1:
                    qkv_refs[g][which, 0] = val.astype(BF16)
                else:
                    for half in range(GROUP_DIM // LANES):
                        res_ref[half] = val[:, half * LANES:(half + 1) * LANES]
                    for r in range(dil):
                        rows = [res_ref[half, pl.ds(r, TM // dil, stride=dil), :]
                                for half in range(GROUP_DIM // LANES)]
                        qkv_refs[g][which, r] = jnp.concatenate(rows, axis=1).astype(BF16)

    @pl.when(jnp.logical_not(is_p))
    def _():
        us_ref[...] = u
        for g in range(N_GROUPS):
            qs_ref[g] = q[g]
            kvs_ref[g, :, :GROUP_DIM] = k[g]
            kvs_ref[g, :, GROUP_DIM:] = v[g]

    j = jnp.minimum(i, n_pt - 1) % tps
    for g, (keep, nk) in enumerate(cfg["kv_keep"]):
        rows = min(keep, TM)

        @pl.when(jnp.logical_and(is_p, j >= tps - nk))
        def _(g=g, rows=rows):
            kvp_refs[g][:, :GROUP_DIM] = k[g][TM - rows:]
            kvp_refs[g][:, GROUP_DIM:] = v[g][TM - rows:]


def _inproj(cfg, x3, mp, ms, gain, w_in_b, qg, kg):
    d, n_pt, n_t, tps, b = cfg["d"], cfg["n_pt"], cfg["n_t"], cfg["tps"], cfg["b"]
    np_, ns = cfg["np"], cfg["ns"]
    pidx = lambda i: jnp.minimum(i, n_pt - 1)
    sidx = lambda i: jnp.maximum(i - n_pt, 0)

    kv_specs, kv_shapes = [], []
    for keep, nk in cfg["kv_keep"]:
        rows = min(keep, TM)

        def kv_idx(i, nk=nk):
            ip = pidx(i)
            return ((ip // tps) * nk + jnp.maximum(ip % tps - (tps - nk), 0), 0)

        kv_specs.append(pl.BlockSpec((rows, 2 * GROUP_DIM), kv_idx))
        kv_shapes.append(jax.ShapeDtypeStruct((b * keep, 2 * GROUP_DIM), F32))

    qkv_specs, qkv_shapes = [], []
    for _, dil in ATT_GROUPS:
        qkv_specs.append(pl.BlockSpec((3, None, dil, TM // dil, GROUP_DIM),
                                      lambda i: (0, pidx(i) // tps, 0, pidx(i) % tps, 0)))
        qkv_shapes.append(jax.ShapeDtypeStruct((3, b, dil, cfg["s"] // dil, GROUP_DIM), BF16))

    return pl.pallas_call(
        functools.partial(_inproj_kernel, cfg),
        grid=(n_t,),
        in_specs=[pl.BlockSpec((GP, SUBLANES, d), lambda i: (i, 0, 0))]
        + _mod_specs(cfg, 0) + _mod_specs(cfg, 1)
        + [pl.BlockSpec((1, d), lambda i: (0, 0)),
           pl.BlockSpec(w_in_b.shape, lambda i: (0, 0)),
           pl.BlockSpec((1, GROUP_DIM), lambda i: (0, 0)),
           pl.BlockSpec((1, GROUP_DIM), lambda i: (0, 0))],
        out_specs=[pl.BlockSpec((TM, POOL_DIM), lambda i: (pidx(i), 0)),
                   pl.BlockSpec((TM, POOL_DIM), lambda i: (sidx(i), 0))] + qkv_specs
        + [pl.BlockSpec((N_GROUPS, TM, GROUP_DIM), lambda i: (0, sidx(i), 0)),
           pl.BlockSpec((N_GROUPS, TM, 2 * GROUP_DIM), lambda i: (0, sidx(i), 0))] + kv_specs,
        out_shape=[jax.ShapeDtypeStruct((np_, POOL_DIM), F32),
                   jax.ShapeDtypeStruct((ns, POOL_DIM), F32)] + qkv_shapes
        + [jax.ShapeDtypeStruct((N_GROUPS, ns, GROUP_DIM), F32),
           jax.ShapeDtypeStruct((N_GROUPS, ns, 2 * GROUP_DIM), F32)] + kv_shapes,
        scratch_shapes=[pltpu.VMEM((GROUP_DIM // LANES, TM, LANES), F32)],
        compiler_params=_cparams(1, VMEM_LIMIT),
        name="inproj",
    )(x3, mp, ms, mp, ms, gain, w_in_b, qg, kg)


def _rel_bucket_np(dist):
    dist = np.asarray(dist, np.int32)
    max_exact = REL_BUCKETS // 2
    d_f = np.maximum(dist, 1).astype(np.float32)
    ratio = np.log(d_f / np.float32(max_exact)) / np.float32(math.log(REL_MAX_DIST / max_exact))
    large = max_exact + (ratio * np.float32(REL_BUCKETS - max_exact)).astype(np.int32)
    large = np.minimum(large, REL_BUCKETS - 1)
    return np.where(dist < max_exact, dist, large)


def _bias_lookup(rel_bias, g, bucket, ok):
    table = rel_bias[:, g * HEADS:(g + 1) * HEADS].astype(F32)
    vals = jnp.zeros((HEADS,) + bucket.shape, F32)
    for bkt in np.unique(bucket):
        vals = jnp.where((bucket == bkt)[None], table[bkt].reshape((HEADS,) + (1,) * bucket.ndim), vals)
    return jnp.where(ok[None], vals, NEG_INF)


def _prompt_bias_tables(rel_bias):
    qi = np.arange(BAND)[:, None]
    kj = np.arange(2 * BAND)[None, :]
    delta = qi + BAND - kj
    ok = (delta >= 0) & (delta <= BAND)
    ok_first = ok & (kj >= BAND)
    tables = []
    for g, (_, dil) in enumerate(ATT_GROUPS):
        bucket = _rel_bucket_np(np.maximum(delta, 0) * dil)
        both = jnp.stack([_bias_lookup(rel_bias, g, bucket, ok), _bias_lookup(rel_bias, g, bucket, ok_first)])
        tables.append(both.reshape(2, HEADS * BAND, 2 * BAND))
    return tables


def _sample_bias_tables(rel_bias, t_new):
    tables = []
    for g, (win, dil) in enumerate(ATT_GROUPS):
        lb = min(win, PAST_LEN)
        t = np.arange(t_new)[:, None]
        col = np.arange(lb + LANES)[None, :]
        delta = lb + t - col
        ok = (col < lb + t_new) & (delta >= 0) & (delta % dil == 0) & (delta // dil <= win // dil)
        bucket = _rel_bucket_np(np.maximum(delta, 0))
        tables.append(_bias_lookup(rel_bias, g, bucket, ok).reshape(HEADS * t_new, lb + LANES))
    return tables


def _stack_heads(q):
    lh = _lane_head()
    return jnp.concatenate([jnp.where(lh == h, q, jnp.zeros_like(q)) for h in range(HEADS)], axis=0)


def _softmax(s):
    m = jnp.max(s, axis=1, keepdims=True)
    p = jnp.exp(s - m)
    l = jnp.sum(p, axis=1, keepdims=True)
    return p.astype(BF16), l, m + jnp.log(l)


def _unstack_heads(o4, lse4, rows):
    lh = _lane_head()
    o = o4[0:rows]
    lse = jnp.broadcast_to(lse4[0:rows], (rows, GROUP_DIM))
    for h in range(1, HEADS):
        o = jnp.where(lh == h, o4[h * rows:(h + 1) * rows], o)
        lse = jnp.where(lh == h, lse4[h * rows:(h + 1) * rows], lse)
    return o, lse


_NT = (((1,), (1,)), ((), ()))


def _attn_prompt_kernel(nsub, q_ref, kc_ref, kp_ref, vc_ref, vp_ref, bias_ref, o_ref, lse_ref, kbuf, vbuf):
    i = pl.program_id(2)
    kbuf[0:BAND] = kp_ref[...]
    kbuf[BAND:] = kc_ref[...]
    vbuf[0:BAND] = vp_ref[...]
    vbuf[BAND:] = vc_ref[...]
    for s in range(nsub):
        q4 = _stack_heads(q_ref[s * BAND:(s + 1) * BAND])
        kc = kbuf[s * BAND:(s + 2) * BAND]
        vc = vbuf[s * BAND:(s + 2) * BAND]
        logits = lax.dot_general(q4, kc, _NT, preferred_element_type=F32)
        if s == 0:
            bias = bias_ref[jnp.where(i == 0, 1, 0)]
        else:
            bias = bias_ref[0]
        p, l, lse4 = _softmax(logits + bias)
        o4 = jnp.dot(p, vc, preferred_element_type=F32) / l
        o, lse = _unstack_heads(o4, lse4, BAND)
        o_ref[s * BAND:(s + 1) * BAND] = o
        lse_ref[s * BAND:(s + 1) * BAND] = lse


def _attn_prompt(cfg, g, qkv, bias):
    b, s_len = cfg["b"], cfg["s"]
    _, dil = ATT_GROUPS[g]
    l_len = s_len // dil
    tq = min(TM, l_len)
    nsub = tq // BAND
    nq = l_len // tq

    def cur(which):
        return pl.BlockSpec((None, None, None, tq, GROUP_DIM), lambda bi, r, i: (which, bi, r, i, 0))

    def prev(which):
        return pl.BlockSpec((None, None, None, BAND, GROUP_DIM),
                            lambda bi, r, i: (which, bi, r, jnp.maximum(i * nsub - 1, 0), 0))

    out = pl.BlockSpec((None, None, tq, GROUP_DIM), lambda bi, r, i: (bi, r, i, 0))
    return pl.pallas_call(
        functools.partial(_attn_prompt_kernel, nsub),
        grid=(b, dil, nq),
        in_specs=[cur(0), cur(1), prev(1), cur(2), prev(2),
                  pl.BlockSpec(bias.shape, lambda bi, r, i: (0, 0, 0))],
        out_specs=[out, out],
        out_shape=[jax.ShapeDtypeStruct((b, dil, l_len, GROUP_DIM), F32)] * 2,
        scratch_shapes=[pltpu.VMEM((tq + BAND, GROUP_DIM), BF16)] * 2,
        compiler_params=_cparams(3, VMEM_LIMIT),
        name=f"attn_prompt_g{g}",
    )(qkv, qkv, qkv, qkv, qkv, bias)


def _attn_sample_kernel(bs, lb, t_new, q_ref, kvn_ref, cache_ref, bias_ref, o_ref, lse_ref):
    pad = jnp.zeros((LANES - t_new, GROUP_DIM), BF16)
    for j in range(bs):
        kt = cache_ref[j, 0:GROUP_DIM, :].astype(BF16)
        vt = cache_ref[j, GROUP_DIM:2 * GROUP_DIM, :].astype(BF16)
        kvn = kvn_ref[j].astype(BF16)
        kn = jnp.concatenate([kvn[:, :GROUP_DIM], pad], axis=0)
        vn = jnp.concatenate([kvn[:, GROUP_DIM:], pad], axis=0)
        q4 = _stack_heads(q_ref[j].astype(BF16))
        logits = jnp.concatenate(
            [jnp.dot(q4, kt, preferred_element_type=F32),
             lax.dot_general(q4, kn, _NT, preferred_element_type=F32)], axis=1)
        p, l, lse4 = _softmax(logits + bias_ref[...])
        o4 = (lax.dot_general(p[:, :lb], vt, _NT, preferred_element_type=F32)
              + jnp.dot(p[:, lb:], vn, preferred_element_type=F32)) / l
        o, lse = _unstack_heads(o4, lse4, t_new)
        o_ref[j] = o
        lse_ref[j] = lse


def _attn_sample(cfg, g, layer, qs, kvs, cache_t, bias):
    bd, t_new = cfg["bd"], cfg["t"]
    lb = cache_t.shape[-1]
    bs = max(1, min(8, (4 * 1024 * 1024) // (2 * GROUP_DIM * lb * 4)))
    out = pl.BlockSpec((bs, t_new, GROUP_DIM), lambda i: (i, 0, 0))
    o, lse = pl.pallas_call(
        functools.partial(_attn_sample_kernel, bs, lb, t_new),
        grid=(bd // bs,),
        in_specs=[pl.BlockSpec((None, bs, t_new, GROUP_DIM), lambda i: (g, i, 0, 0)),
                  pl.BlockSpec((None, bs, t_new, 2 * GROUP_DIM), lambda i: (g, i, 0, 0)),
                  pl.BlockSpec((None, bs, 2 * GROUP_DIM, lb), lambda i: (layer, i, 0, 0)),
                  pl.BlockSpec(bias.shape, lambda i: (0, 0))],
        out_specs=[out, out],
        out_shape=[jax.ShapeDtypeStruct((bd, t_new, GROUP_DIM), F32)] * 2,
        compiler_params=_cparams(1, VMEM_LIMIT),
        name=f"attn_sample_g{g}",
    )(qs.reshape(N_GROUPS, bd, t_new, GROUP_DIM), kvs.reshape(N_GROUPS, bd, t_new, 2 * GROUP_DIM),
      cache_t, bias)
    return o.reshape(bd * t_new, GROUP_DIM), lse.reshape(bd * t_new, GROUP_DIM)


def _pool_mix(ext_ref, rows, pos, w_ref, scale_ref, o_ref):
    lo = POOL_PAD
    u = ext_ref[:, lo:lo + rows, :]
    acc = u
    sums = {}
    for j in range(1, max(POOL_WINDOWS)):
        acc = acc + ext_ref[:, lo - j:lo - j + rows, :]
        if j + 1 in POOL_WINDOWS:
            sums[j + 1] = acc
    lane_grp = lax.broadcasted_iota(jnp.int32, (1, 1, POOL_DIM), 2) // (POOL_DIM // len(POOL_WINDOWS))
    z = None
    for gi, w in enumerate(POOL_WINDOWS):
        cnt = jnp.minimum(w, pos + 1).astype(F32)
        zw = sums[w] / cnt - u
        z = zw if z is None else jnp.where(lane_grp == gi, zw, z)
    nb = u.shape[0]
    y = jnp.dot(z.reshape(nb * rows, POOL_DIM).astype(BF16), w_ref[...], preferred_element_type=F32)
    o_ref[...] = (y * scale_ref[...]).astype(BF16)


def _pool_prompt_kernel(u_ref, halo_ref, w_ref, scale_ref, o_ref, ext_ref):
    i = pl.program_id(1)
    halo = halo_ref[...]
    ext_ref[:, 0:POOL_PAD, :] = jnp.where(i == 0, jnp.zeros_like(halo), halo)
    ext_ref[:, POOL_PAD:, :] = u_ref[...]
    pos = i * TM + lax.broadcasted_iota(jnp.int32, (1, TM, 1), 1)
    _pool_mix(ext_ref, TM, pos, w_ref, scale_ref, o_ref)


def _pool_sample_kernel(t_new, u_ref, hist_ref, w_ref, scale_ref, o_ref, ext_ref):
    nb = u_ref.shape[0]
    ext_ref[:, 0:POOL_PAD - POOL_HIST, :] = jnp.zeros((nb, POOL_PAD - POOL_HIST, POOL_DIM), F32)
    ext_ref[:, POOL_PAD - POOL_HIST:POOL_PAD, :] = hist_ref[...]
    ext_ref[:, POOL_PAD:, :] = u_ref[...]
    pos = PAST_LEN + lax.broadcasted_iota(jnp.int32, (1, t_new, 1), 1)
    _pool_mix(ext_ref, t_new, pos, w_ref, scale_ref, o_ref)


def _pool_prompt(cfg, u_p, w_bd, scale):
    b, s_len = cfg["b"], cfg["s"]
    u3 = u_p.reshape(b, s_len, POOL_DIM)
    per = TM // POOL_PAD
    return pl.pallas_call(
        _pool_prompt_kernel,
        grid=(b, s_len // TM),
        in_specs=[pl.BlockSpec((1, TM, POOL_DIM), lambda bi, i: (bi, i, 0)),
                  pl.BlockSpec((1, POOL_PAD, POOL_DIM), lambda bi, i: (bi, jnp.maximum(i * per - 1, 0), 0)),
                  pl.BlockSpec((POOL_DIM, POOL_DIM), lambda bi, i: (0, 0)),
                  pl.BlockSpec((1, POOL_DIM), lambda bi, i: (0, 0))],
        out_specs=pl.BlockSpec((TM, POOL_DIM), lambda bi, i: (bi * (s_len // TM) + i, 0)),
        out_shape=jax.ShapeDtypeStruct((b * s_len, POOL_DIM), BF16),
        scratch_shapes=[pltpu.VMEM((1, TM + POOL_PAD, POOL_DIM), F32)],
        compiler_params=_cparams(2, VMEM_LIMIT),
        name="pool_prompt",
    )(u3, u3, w_bd, scale)


def _pool_sample(cfg, layer, u_s, state_pool, w_bd, scale):
    bd, t_new = cfg["bd"], cfg["t"]
    u3 = u_s.reshape(bd, t_new, POOL_DIM)
    nb = GP
    return pl.pallas_call(
        functools.partial(_pool_sample_kernel, t_new),
        grid=(bd // nb,),
        in_specs=[pl.BlockSpec((nb, t_new, POOL_DIM), lambda i: (i, 0, 0)),
                  pl.BlockSpec((None, nb, POOL_HIST, POOL_DIM), lambda i: (layer, i, 0, 0)),
                  pl.BlockSpec((POOL_DIM, POOL_DIM), lambda i: (0, 0)),
                  pl.BlockSpec((1, POOL_DIM), lambda i: (0, 0))],
        out_specs=pl.BlockSpec((nb * t_new, POOL_DIM), lambda i: (i, 0)),
        out_shape=jax.ShapeDtypeStruct((bd * t_new, POOL_DIM), BF16),
        scratch_shapes=[pltpu.VMEM((nb, t_new + POOL_PAD, POOL_DIM), F32)],
        compiler_params=_cparams(1, VMEM_LIMIT),
        name="pool_sample",
    )(u3, state_pool, w_bd, scale)


def _outproj_kernel(cfg, x_ref, g1p_ref, g1s_ref, shp_ref, shs_ref, scp_ref, scs_ref, pp_ref, ps_ref, *rest):
    att_refs = rest[:4 * N_GROUPS]
    (gain_ref, wo_ref, wr_ref, br_ref, x1_ref, h2_ref, route_ref, cnt_ref,
     run_ref, nat_ref) = rest[4 * N_GROUPS:]
    i = pl.program_id(0)
    n_pt, d = cfg["n_pt"], cfg["d"]
    is_p = i < n_pt

    halves = GROUP_DIM // LANES

    def put(slot, val, rows=None):
        for half in range(halves):
            piece = val[:, half * LANES:(half + 1) * LANES]
            if rows is None:
                nat_ref[slot, half] = piece
            else:
                nat_ref[slot, half, rows, :] = piece

    @pl.when(is_p)
    def _():
        for g, (_, dil) in enumerate(ATT_GROUPS):
            for which in range(2):
                src = att_refs[4 * g + which]
                if dil == 1:
                    put(2 * g + which, src[0])
                else:
                    for r in range(dil):
                        put(2 * g + which, src[r], pl.ds(r, TM // dil, stride=dil))

    @pl.when(jnp.logical_not(is_p))
    def _():
        for g in range(N_GROUPS):
            for which in range(2):
                put(2 * g + which, att_refs[4 * g + 2 + which][...])

    def get(slot):
        return jnp.concatenate([nat_ref[slot, half] for half in range(halves)], axis=1)

    o = [get(2 * g) for g in range(N_GROUPS)]
    lse = [get(2 * g + 1) for g in range(N_GROUPS)]
    mx = functools.reduce(jnp.maximum, lse)
    e = [jnp.exp(l - mx) for l in lse]
    attn = sum(eg * og for eg, og in zip(e, o)) / sum(e)
    pool = jnp.where(is_p, pp_ref[...], ps_ref[...])
    cat = jnp.concatenate([pool, attn.astype(BF16)], axis=1)
    y = jnp.dot(cat, wo_ref[...], preferred_element_type=F32)

    g1 = jnp.where(is_p, g1p_ref[...], g1s_ref[...])
    x1 = x_ref[...] + g1 * y.reshape(GP, SUBLANES, d)
    x1_ref[...] = x1
    sh = jnp.where(is_p, shp_ref[...], shs_ref[...])
    sc = jnp.where(is_p, scp_ref[...], scs_ref[...])
    h2 = _rmsnorm_mod(x1, gain_ref[...], sc, sh).reshape(TM, d)
    _rows_to_slabs(h2_ref, h2)

    h_hi = h2.astype(BF16)
    h_lo = (h2 - h_hi.astype(F32)).astype(BF16)
    wr = wr_ref[...]
    w_hi = wr.astype(BF16)
    w_lo = (wr - w_hi.astype(F32)).astype(BF16)
    logits = (jnp.dot(h_hi, w_hi, preferred_element_type=F32)
              + jnp.dot(h_lo, w_hi, preferred_element_type=F32)
              + jnp.dot(h_hi, w_lo, preferred_element_type=F32)) + br_ref[...]

    lane = lax.broadcasted_iota(jnp.int32, (TM, LANES), 1).astype(F32)
    vals = logits
    top_v, top_i = [], []
    onehot = jnp.zeros((TM, LANES), F32)
    for _ in range(TOP_K):
        m = jnp.max(vals, axis=1, keepdims=True)
        idx = jnp.min(jnp.where(vals == m, lane, float(LANES)), axis=1, keepdims=True)
        hit = lane == idx
        vals = jnp.where(hit, -jnp.inf, vals)
        onehot = jnp.where(hit, 1.0, onehot)
        top_v.append(m)
        top_i.append(idx)
    ev = [jnp.exp(v - top_v[0]) for v in top_v]
    den = sum(ev)

    @pl.when(i == 0)
    def _():
        run_ref[...] = jnp.zeros_like(run_ref)

    row = lax.broadcasted_iota(jnp.int32, (TM, TM), 0)
    col = lax.broadcasted_iota(jnp.int32, (TM, TM), 1)
    before = (col < row).astype(BF16)
    rank_all = jnp.dot(before, onehot.astype(BF16), preferred_element_type=F32) + run_ref[...]
    route = jnp.zeros((TM, LANES), F32)
    for k in range(TOP_K):
        rank_k = jnp.sum(jnp.where(lane == top_i[k], rank_all, 0.0), axis=1, keepdims=True)
        route = jnp.where(lane == float(k), top_i[k], route)
        route = jnp.where(lane == float(TOP_K + k), ev[k] / den, route)
        route = jnp.where(lane == float(2 * TOP_K + k), rank_k, route)
    route_ref[...] = route
    run_ref[...] = run_ref[...] + jnp.sum(onehot, axis=0, keepdims=True)
    cnt_ref[...] = run_ref[...]


def _outproj(cfg, x3, mp, ms, pool_p, pool_s, att, gain, w_out_b, w_r, b_r):
    d, n_pt, n_t = cfg["d"], cfg["n_pt"], cfg["n_t"]
    n = cfg["n"]
    tps = cfg["tps"]
    pidx = lambda i: (jnp.minimum(i, n_pt - 1), 0)
    sidx = lambda i: (jnp.maximum(i - n_pt, 0), 0)
    pspec = pl.BlockSpec((TM, GROUP_DIM), pidx)
    sspec = pl.BlockSpec((TM, GROUP_DIM), sidx)
    att_specs, att_args = [], []
    for (_, dil), (o_p, lse_p, o_s, lse_s) in zip(ATT_GROUPS, att):
        rspec = pl.BlockSpec((None, dil, TM // dil, GROUP_DIM),
                             lambda i: (pidx(i)[0] // tps, 0, pidx(i)[0] % tps, 0))
        att_specs += [rspec, rspec, sspec, sspec]
        att_args += [o_p, lse_p, o_s, lse_s]
    full = lambda a: pl.BlockSpec(a.shape, lambda i: (0,) * a.ndim)
    return pl.pallas_call(
        functools.partial(_outproj_kernel, cfg),
        grid=(n_t,),
        in_specs=[pl.BlockSpec((GP, SUBLANES, d), lambda i: (i, 0, 0))]
        + _mod_specs(cfg, 2) + _mod_specs(cfg, 3) + _mod_specs(cfg, 4)
        + [pspec, sspec] + att_specs + [full(gain), full(w_out_b), full(w_r), full(b_r)],
        out_specs=[pl.BlockSpec((GP, SUBLANES, d), lambda i: (i, 0, 0)),
                   pl.BlockSpec((TM * SUBLANES, LANES), lambda i: (i, 0)),
                   pl.BlockSpec((TM, LANES), lambda i: (i, 0)),
                   pl.BlockSpec((1, LANES), lambda i: (0, 0))],
        out_shape=[jax.ShapeDtypeStruct(x3.shape, F32),
                   jax.ShapeDtypeStruct((n * SUBLANES, LANES), F32),
                   jax.ShapeDtypeStruct((n, LANES), F32),
                   jax.ShapeDtypeStruct((1, LANES), F32)],
        scratch_shapes=[pltpu.VMEM((1, LANES), F32),
                        pltpu.VMEM((2 * N_GROUPS, GROUP_DIM // LANES, TM, LANES), F32)],
        compiler_params=_cparams(1, VMEM_LIMIT),
        name="outproj_router",
    )(x3, mp, ms, mp, ms, mp, ms, pool_p, pool_s, *att_args, gain, w_out_b, w_r, b_r)


def _rows_from_slabs(ref):
    rows = ref.shape[0] // SUBLANES
    return jnp.concatenate([ref[pl.ds(j, rows, stride=SUBLANES), :] for j in range(SUBLANES)], axis=1)


def _rows_to_slabs(ref, val):
    for j in range(SUBLANES):
        ref[pl.ds(j, val.shape[0], stride=SUBLANES), :] = val[:, j * LANES:(j + 1) * LANES]


def _dispatch_kernel(n_tok, zt_ref, pos_ref, h_ref, x_hbm, dst_ref, zbuf, zsem, sem):
    i = pl.program_id(0)
    rows = h_ref.shape[0] // SUBLANES
    n_asg = n_tok * TOP_K

    @pl.when(i == 0)
    def _():
        zbuf[...] = jnp.zeros_like(zbuf)

        def fill(z):
            return pltpu.make_async_copy(zbuf, x_hbm.at[pl.ds(zt_ref[z] * MOE_TM, MOE_TM)], zsem)

        for z in range(zt_ref.shape[0]):
            pl.when(zt_ref[z] >= 0)(lambda z=z: fill(z).start())

        def pad_rows(t, carry):
            base = n_asg + ((t + 1) % 2) * MOE_TM
            for r in range(MOE_TM):
                dst_ref[t * MOE_TM + r] = base + r
            return carry

        lax.fori_loop(0, dst_ref.shape[0] // MOE_TM, pad_rows, 0)
        for z in range(zt_ref.shape[0]):
            pl.when(zt_ref[z] >= 0)(lambda z=z: fill(z).wait())

    for r in range(rows):
        for k in range(TOP_K):
            slot = pos_ref[0, r * TOP_K + k]
            dst_ref[MOE_TM + slot] = k * n_tok + i * rows + r
            pltpu.make_async_copy(h_ref.at[pl.ds(r * SUBLANES, SUBLANES)], x_hbm.at[slot],
                                  sem.at[k]).start(priority=k % 2)
    for k in range(TOP_K):
        pltpu.make_async_copy(h_ref, h_ref, sem.at[k]).wait()


def _dispatch(cfg, pos, zero_tiles, h2, m_pad):
    n_t = cfg["n_t"]
    grid_spec = pltpu.PrefetchScalarGridSpec(
        num_scalar_prefetch=1,
        grid=(n_t,),
        in_specs=[pl.BlockSpec((None, 1, TM * TOP_K), lambda i, zt: (i, 0, 0), memory_space=pltpu.SMEM),
                  pl.BlockSpec((TM * SUBLANES, LANES), lambda i, zt: (i, 0))],
        out_specs=[pl.BlockSpec(memory_space=pl.ANY), pl.BlockSpec(memory_space=pltpu.SMEM)],
        scratch_shapes=[pltpu.VMEM((MOE_TM, SUBLANES, LANES), F32),
                        pltpu.SemaphoreType.DMA(()),
                        pltpu.SemaphoreType.DMA((TOP_K,))],
    )
    return pl.pallas_call(
        functools.partial(_dispatch_kernel, cfg["n"]),
        grid_spec=grid_spec,
        out_shape=[jax.ShapeDtypeStruct((m_pad, SUBLANES, LANES), F32),
                   jax.ShapeDtypeStruct((m_pad + MOE_TM,), jnp.int32)],
        compiler_params=_cparams(1, VMEM_LIMIT),
        name="moe_dispatch",
    )(zero_tiles, pos.reshape(n_t, 1, TM * TOP_K), h2)


N_YBUF = 3


def _moe_kernel(te_ref, prv_ref, last_ref, x_ref, w1_ref, b1_ref, w2_ref, b2_ref, y_hbm,
                y0, y1, y2, w1b, w2b, ssem):
    i = pl.program_id(0)
    n = pl.num_programs(0)
    ff = w2_ref.shape[0]
    ys = (y0, y1, y2)

    def start_scatter(idx_ref, q):
        for r in range(MOE_TM):
            pltpu.make_async_copy(ys[q].at[pl.ds(r * SUBLANES, SUBLANES)], y_hbm.at[idx_ref[0, r]],
                                  ssem.at[q]).start(priority=r % 2)

    def wait_scatter(q):
        pltpu.make_async_copy(ys[q], ys[q], ssem.at[q]).wait()

    @pl.when(i == 0)
    def _():
        n_asg = y_hbm.shape[0] - 2 * MOE_TM
        for q in range(N_YBUF):
            ys[q][...] = jnp.zeros_like(ys[q])
        for q in range(2):
            for r in range(MOE_TM):
                pltpu.make_async_copy(ys[q].at[pl.ds(r * SUBLANES, SUBLANES)],
                                      y_hbm.at[n_asg + q * MOE_TM + r], ssem.at[q]).start()
            wait_scatter(q)

    @pl.when(jnp.logical_or(i == 0, te_ref[i] != te_ref[jnp.maximum(i - 1, 0)]))
    def _():
        w1b[...] = w1_ref[...].astype(BF16)
        w2b[...] = w2_ref[...].astype(BF16)

    def step(q):
        q_prev, q_old = (q + 2) % N_YBUF, (q + 1) % N_YBUF
        start_scatter(prv_ref, q_prev)
        x = _rows_from_slabs(x_ref).astype(BF16)
        h1 = jnp.dot(x, w1b[...], preferred_element_type=F32) + b1_ref[...]
        gate = jnp.minimum(h1[:, :ff], SWIGLU_LIMIT)
        up = jnp.clip(h1[:, ff:], -SWIGLU_LIMIT, SWIGLU_LIMIT)
        act = gate * jax.nn.sigmoid(SWIGLU_ALPHA * gate) * (up + 1.0)
        _rows_to_slabs(ys[q], jnp.dot(act.astype(BF16), w2b[...], preferred_element_type=F32) + b2_ref[...])

        @pl.when(i >= 1)
        def _():
            wait_scatter(q_old)

        @pl.when(i == n - 1)
        def _():
            wait_scatter(q_prev)
            start_scatter(last_ref, q)
            wait_scatter(q)

    for q in range(N_YBUF):
        pl.when(i % N_YBUF == q)(functools.partial(step, q))


def _moe(layer, n_tiles, tile_expert, slot_dst, x_sorted, w1, b1, w2, b2, n_rows_out):
    depth, n_exp, d, ff2 = w1.shape
    ff = w2.shape[2]
    smem_tile = lambda f: pl.BlockSpec((None, 1, MOE_TM), f, memory_space=pltpu.SMEM)
    grid_spec = pltpu.PrefetchScalarGridSpec(
        num_scalar_prefetch=1,
        grid=(n_tiles,),
        in_specs=[smem_tile(lambda i, te: (i, 0, 0)),
                  smem_tile(lambda i, te: (n_tiles, 0, 0)),
                  pl.BlockSpec((MOE_TM * SUBLANES, LANES), lambda i, te: (i, 0)),
                  pl.BlockSpec((None, None, d, ff2), lambda i, te: (layer, te[i], 0, 0)),
                  pl.BlockSpec((None, None, 1, ff2), lambda i, te: (layer, te[i], 0, 0)),
                  pl.BlockSpec((None, None, ff, d), lambda i, te: (layer, te[i], 0, 0)),
                  pl.BlockSpec((None, None, 1, d), lambda i, te: (layer, te[i], 0, 0))],
        out_specs=pl.BlockSpec(memory_space=pl.ANY),
        scratch_shapes=[pltpu.VMEM((MOE_TM * SUBLANES, LANES), F32)] * N_YBUF
        + [pltpu.VMEM((d, ff2), BF16),
           pltpu.VMEM((ff, d), BF16),
           pltpu.SemaphoreType.DMA((N_YBUF,))],
    )
    dst3 = slot_dst.reshape(n_tiles + 1, 1, MOE_TM)
    return pl.pallas_call(
        _moe_kernel,
        grid_spec=grid_spec,
        out_shape=jax.ShapeDtypeStruct((n_rows_out, SUBLANES, LANES), F32),
        compiler_params=_cparams(1, VMEM_LIMIT),
        name="moe_experts",
    )(tile_expert, dst3, dst3, x_sorted.reshape(x_sorted.shape[0] * SUBLANES, LANES),
      w1, b1.reshape(depth, n_exp, 1, ff2), w2, b2.reshape(depth, n_exp, 1, d))


def _moe_plan(cfg, route, counts):
    n = cfg["n"]
    m = n * TOP_K
    n_tiles = m // MOE_TM + N_EXPERTS
    m_pad = n_tiles * MOE_TM
    top_i = route[:, 0:TOP_K].astype(jnp.int32)
    rank = route[:, 2 * TOP_K:3 * TOP_K].astype(jnp.int32)
    cnt = counts[0, :N_EXPERTS].astype(jnp.int32)
    tiles_e = (cnt + MOE_TM - 1) // MOE_TM
    tile_end = jnp.cumsum(tiles_e)
    pstart = (tile_end - tiles_e) * MOE_TM
    expert = jnp.arange(N_EXPERTS, dtype=jnp.int32)
    pos = jnp.sum(jnp.where(top_i[..., None] == expert, pstart, 0), axis=-1) + rank
    t = jnp.arange(n_tiles, dtype=jnp.int32)
    te = jnp.sum((tile_end[None, :] <= t[:, None]).astype(jnp.int32), axis=1)
    last_used = jnp.sum((tile_end <= tile_end[-1] - 1).astype(jnp.int32))
    te = jnp.minimum(te, last_used).astype(jnp.int32)
    n_used = tile_end[-1]
    spare = n_used + jnp.arange(n_tiles - m // MOE_TM, dtype=jnp.int32)
    zero_tiles = jnp.concatenate([jnp.where(tiles_e > 0, tile_end - 1, -1),
                                  jnp.where(spare < n_tiles, spare, -1)]).astype(jnp.int32)
    return n_tiles, te, pos, zero_tiles, m + 2 * MOE_TM


def _combine_kernel(cfg, x_ref, g2p_ref, g2s_ref, route_ref, *rest):
    y_refs, o_ref = rest[:TOP_K], rest[TOP_K]
    i = pl.program_id(0)
    is_p = i < cfg["n_pt"]
    route = route_ref[...]
    acc = route[:, TOP_K:TOP_K + 1] * _rows_from_slabs(y_refs[0])
    for k in range(1, TOP_K):
        acc = acc + route[:, TOP_K + k:TOP_K + k + 1] * _rows_from_slabs(y_refs[k])
    g2 = jnp.where(is_p, g2p_ref[...], g2s_ref[...])
    o_ref[...] = x_ref[...] + g2 * acc.reshape(GP, SUBLANES, cfg["d"])


def _combine(cfg, x1, mp, ms, route, ybuf):
    d, n_t = cfg["d"], cfg["n_t"]
    y_specs = [pl.BlockSpec((TM * SUBLANES, LANES), lambda i, k=k: (k * n_t + i, 0)) for k in range(TOP_K)]
    ybuf = ybuf.reshape(ybuf.shape[0] * SUBLANES, LANES)
    return pl.pallas_call(
        functools.partial(_combine_kernel, cfg),
        grid=(n_t,),
        in_specs=[pl.BlockSpec((GP, SUBLANES, d), lambda i: (i, 0, 0))] + _mod_specs(cfg, 5)
        + [pl.BlockSpec((TM, LANES), lambda i: (i, 0))] + y_specs,
        out_specs=pl.BlockSpec((GP, SUBLANES, d), lambda i: (i, 0, 0)),
        out_shape=jax.ShapeDtypeStruct(x1.shape, F32),
        compiler_params=_cparams(1, VMEM_LIMIT),
        name="moe_combine",
    )(x1, mp, ms, route, *([ybuf] * TOP_K))


def kernel(x_prompt, x_sample, cache_kv_w128_d1, cache_kv_w512_d4, cache_kv_w2048_d16, state_pool,
           c_prompt, c_sample, rel_bias, norm_mix, norm_ffn, w_ada, b_ada, w_in, q_norm, k_norm,
           pool_w, pool_scale, w_out, w_router, b_router, w_expert_in, b_expert_in,
           w_expert_out, b_expert_out):
    b, s_len, d = x_prompt.shape
    bd, t_new, _ = x_sample.shape
    depth = w_in.shape[0]
    np_, ns = b * s_len, bd * t_new
    assert t_new == SUBLANES and s_len % TM == 0 and ns % TM == 0
    assert all((s_len // dil) % BAND == 0 for _, dil in ATT_GROUPS)
    cfg = dict(b=b, s=s_len, d=d, bd=bd, t=t_new, np=np_, ns=ns, n=np_ + ns,
               n_pt=np_ // TM, n_t=(np_ + ns) // TM, tps=s_len // TM)
    kv_keep = []
    for win, _ in ATT_GROUPS:
        keep = min(win, s_len)
        assert keep % TM == 0 or TM % keep == 0
        kv_keep.append((keep, max(keep // TM, 1)))
    cfg["kv_keep"] = tuple(kv_keep)
    caches_t = [jnp.transpose(c, (0, 1, 3, 4, 5, 2)).reshape(depth, bd, 2 * GROUP_DIM, c.shape[2])
                for c in (cache_kv_w128_d1, cache_kv_w512_d4, cache_kv_w2048_d16)]

    mods = _ada_mods(jnp.concatenate([c_prompt, c_sample], axis=0), w_ada, b_ada)
    bias_p = _prompt_bias_tables(rel_bias)
    bias_s = _sample_bias_tables(rel_bias, t_new)
    w_in_b = w_in.astype(BF16)
    w_out_b = w_out.astype(BF16)
    eye = jnp.eye(len(POOL_WINDOWS), dtype=F32)
    w_pool_bd = (eye[None, :, None, :, None] * pool_w[:, :, :, None, :]).reshape(depth, POOL_DIM, POOL_DIM)
    w_pool_bd = w_pool_bd.astype(BF16)
    w_r_pad = jnp.pad(w_router, ((0, 0), (0, 0), (0, LANES - N_EXPERTS)))
    b_r_pad = jnp.pad(b_router, ((0, 0), (0, LANES - N_EXPERTS)), constant_values=NEG_INF)

    x3 = jnp.concatenate([x_prompt.reshape(np_, d), x_sample.reshape(ns, d)], axis=0)
    x3 = x3.reshape((np_ + ns) // SUBLANES, SUBLANES, d)

    kv_p = [[] for _ in ATT_GROUPS]
    kv_s = [[] for _ in ATT_GROUPS]
    pool_p_state, pool_s_state = [], []
    for l in range(depth):
        mp = mods[l, :b].reshape(b, 1, 6 * d)
        ms = mods[l, b:].reshape(bd, 1, 6 * d)
        qg = jnp.tile(q_norm[l], HEADS).reshape(1, GROUP_DIM)
        kg = jnp.tile(k_norm[l], HEADS).reshape(1, GROUP_DIM)
        (u_p, u_s, qkv0, qkv1, qkv2, qs, kvs, *kvp) = _inproj(
            cfg, x3, mp, ms, norm_mix[l].reshape(1, d), w_in_b[l], qg, kg)

        att = []
        for g, qkv in enumerate((qkv0, qkv1, qkv2)):
            o_p, lse_p = _attn_prompt(cfg, g, qkv, bias_p[g])
            o_s, lse_s = _attn_sample(cfg, g, l, qs, kvs, caches_t[g], bias_s[g])
            att.append((o_p, lse_p, o_s, lse_s))
            keep = cfg["kv_keep"][g][0]
            kv_p[g].append(kvp[g].reshape(b, keep, 2, HEADS, HEAD_DIM))
            kv_s[g].append(kvs[g].reshape(bd, t_new, 2, HEADS, HEAD_DIM))

        scale = pool_scale[l].reshape(1, POOL_DIM)
        pool_p = _pool_prompt(cfg, u_p, w_pool_bd[l], scale)
        pool_s = _pool_sample(cfg, l, u_s, state_pool, w_pool_bd[l], scale)
        pool_p_state.append(u_p.reshape(b, s_len, POOL_DIM)[:, s_len - POOL_HIST:])
        pool_s_state.append(jnp.concatenate(
            [state_pool[l], u_s.reshape(bd, t_new, POOL_DIM)], axis=1)[:, -POOL_HIST:])

        x1, h2, route, counts = _outproj(cfg, x3, mp, ms, pool_p, pool_s, att,
                                         norm_ffn[l].reshape(1, d), w_out_b[l], w_r_pad[l],
                                         b_r_pad[l].reshape(1, LANES))
        n_tiles, te, pos, zero_tiles, n_rows_out = _moe_plan(cfg, route, counts)
        x_sorted, slot_dst = _dispatch(cfg, pos, zero_tiles, h2, n_tiles * MOE_TM)
        ybuf = _moe(l, n_tiles, te, slot_dst, x_sorted,
                    w_expert_in, b_expert_in, w_expert_out, b_expert_out, n_rows_out)
        x3 = _combine(cfg, x1, mp, ms, route, ybuf)

    x_all = x3.reshape(np_ + ns, d)
    y_prompt = x_all[:np_].reshape(b, s_len, d)
    y_sample = x_all[np_:].reshape(bd, t_new, d)
    return (y_prompt, y_sample,
            jnp.stack(kv_p[0]), jnp.stack(kv_p[1]), jnp.stack(kv_p[2]), jnp.stack(pool_p_state),
            jnp.stack(kv_s[0]), jnp.stack(kv_s[1]), jnp.stack(kv_s[2]), jnp.stack(pool_s_state))
```

```python
import functools
import math

import numpy as np
import jax
import jax.numpy as jnp
from jax import lax
from jax.experimental import pallas as pl
from jax.experimental.pallas import tpu as pltpu

F32 = jnp.float32
BF16 = jnp.bfloat16

HEAD_DIM = 64
HEADS = 4
GROUP_DIM = HEADS * HEAD_DIM
ATT_GROUPS = ((128, 1), (512, 4), (2048, 16))
N_GROUPS = len(ATT_GROUPS)
BAND = 128
POOL_WINDOWS = (2, 4, 8, 16)
POOL_HIST = 15
POOL_DIM = 256
POOL_PAD = 16
REL_BUCKETS = 32
REL_MAX_DIST = 2048
N_EXPERTS = 32
TOP_K = 4
SWIGLU_LIMIT = 7.0
SWIGLU_ALPHA = 1.702
EPS = 1e-6
NEG_INF = -1e30
PAST_LEN = 2048

SUBLANES = 8
LANES = 128
TM = 512
GP = TM // SUBLANES
MOE_TM = 512
VMEM_LIMIT = 58 * 1024 * 1024


def _cparams(n_axes, vmem=None):
    return pltpu.CompilerParams(dimension_semantics=("arbitrary",) * n_axes,
                                vmem_limit_bytes=vmem)


def _lane_head(width=GROUP_DIM):
    return lax.broadcasted_iota(jnp.int32, (1, width), 1) // HEAD_DIM


def _ada_kernel(c_ref, w_ref, b_ref, o_ref):
    c = c_ref[...]
    a = (c * jax.nn.sigmoid(c)).astype(BF16)
    o_ref[...] = jnp.dot(a, w_ref[...].astype(BF16), preferred_element_type=F32) + b_ref[...]


def _ada_mods(c_all, w_ada, b_ada):
    depth, d, d6 = w_ada.shape
    bc = c_all.shape[0]
    tn = d6 // 4
    return pl.pallas_call(
        _ada_kernel,
        grid=(depth, d6 // tn),
        in_specs=[pl.BlockSpec((bc, d), lambda l, j: (0, 0)),
                  pl.BlockSpec((None, d, tn), lambda l, j: (l, 0, j)),
                  pl.BlockSpec((None, 1, tn), lambda l, j: (l, 0, j))],
        out_specs=pl.BlockSpec((None, bc, tn), lambda l, j: (l, 0, j)),
        out_shape=jax.ShapeDtypeStruct((depth, bc, d6), F32),
        compiler_params=_cparams(2, VMEM_LIMIT),
        name="ada_mods",
    )(c_all, w_ada, b_ada.reshape(depth, 1, d6))


def _mod_specs(cfg, col):
    d = cfg["d"]
    n_pt, tps, b = cfg["n_pt"], cfg["tps"], cfg["b"]
    return [pl.BlockSpec((1, 1, d), lambda i: (jnp.minimum(i // tps, b - 1), 0, col)),
            pl.BlockSpec((GP, 1, d), lambda i: (jnp.maximum(i - n_pt, 0), 0, col))]


def _rmsnorm_mod(x, g, sc, sh):
    inv = lax.rsqrt(jnp.mean(x * x, axis=-1, keepdims=True) + EPS)
    return x * inv * g * (1.0 + sc) + sh


def _inproj_kernel(cfg, x_ref, shp_ref, shs_ref, scp_ref, scs_ref, g_ref, w_ref, qg_ref, kg_ref,
                   up_ref, us_ref, qkv0_ref, qkv1_ref, qkv2_ref, qs_ref, kvs_ref,
                   kvp0_ref, kvp1_ref, kvp2_ref, res_ref):
    qkv_refs = (qkv0_ref, qkv1_ref, qkv2_ref)
    kvp_refs = (kvp0_ref, kvp1_ref, kvp2_ref)
    i = pl.program_id(0)
    n_pt, tps, d = cfg["n_pt"], cfg["tps"], cfg["d"]
    is_p = i < n_pt
    sh = jnp.where(is_p, shp_ref[...], shs_ref[...])
    sc = jnp.where(is_p, scp_ref[...], scs_ref[...])
    h = _rmsnorm_mod(x_ref[...], g_ref[...], sc, sh)
    z = jnp.dot(h.reshape(TM, d).astype(BF16), w_ref[...], preferred_element_type=F32)

    r = lax.broadcasted_iota(jnp.int32, (GROUP_DIM, GROUP_DIM), 0) // HEAD_DIM
    c = lax.broadcasted_iota(jnp.int32, (GROUP_DIM, GROUP_DIM), 1) // HEAD_DIM
    head_ones = (r == c).astype(BF16)

    def head_norm(t, gain):
        ss = jnp.dot((t * t).astype(BF16), head_ones, preferred_element_type=F32) * (1.0 / HEAD_DIM)
        return t * lax.rsqrt(ss + EPS) * gain

    u = z[:, :POOL_DIM]
    att = d - POOL_DIM
    q, k, v = [], [], []
    for g in range(N_GROUPS):
        lo = POOL_DIM + g * GROUP_DIM
        q.append(head_norm(z[:, lo:lo + GROUP_DIM], qg_ref[...]) * (HEAD_DIM ** -0.5))
        k.append(head_norm(z[:, lo + att:lo + att + GROUP_DIM], kg_ref[...]))
        v.append(z[:, lo + 2 * att:lo + 2 * att + GROUP_DIM])

    @pl.when(is_p)
    def _():
        up_ref[...] = u
        for g, (_, dil) in enumerate(ATT_GROUPS):
            for which, val in enumerate((q[g], k[g], v[g])):
                if dil == 1:
                    qkv_refs[g][which, 0] = val.astype(BF16)
                else:
                    for half in range(GROUP_DIM // LANES):
                        res_ref[half] = val[:, half * LANES:(half + 1) * LANES]
                    for r in range(dil):
                        rows = [res_ref[half, pl.ds(r, TM // dil, stride=dil), :]
                                for half in range(GROUP_DIM // LANES)]
                        qkv_refs[g][which, r] = jnp.concatenate(rows, axis=1).astype(BF16)

    @pl.when(jnp.logical_not(is_p))
    def _():
        us_ref[...] = u
        for g in range(N_GROUPS):
            qs_ref[g] = q[g]
            kvs_ref[g, :, :GROUP_DIM] = k[g]
            kvs_ref[g, :, GROUP_DIM:] = v[g]

    j = jnp.minimum(i, n_pt - 1) % tps
    for g, (keep, nk) in enumerate(cfg["kv_keep"]):
        rows = min(keep, TM)

        @pl.when(jnp.logical_and(is_p, j >= tps - nk))
        def _(g=g, rows=rows):
            kvp_refs[g][:, :GROUP_DIM] = k[g][TM - rows:]
            kvp_refs[g][:, GROUP_DIM:] = v[g][TM - rows:]


def _inproj(cfg, x3, mp, ms, gain, w_in_b, qg, kg):
    d, n_pt, n_t, tps, b = cfg["d"], cfg["n_pt"], cfg["n_t"], cfg["tps"], cfg["b"]
    np_, ns = cfg["np"], cfg["ns"]
    pidx = lambda i: jnp.minimum(i, n_pt - 1)
    sidx = lambda i: jnp.maximum(i - n_pt, 0)

    kv_specs, kv_shapes = [], []
    for keep, nk in cfg["kv_keep"]:
        rows = min(keep, TM)

        def kv_idx(i, nk=nk):
            ip = pidx(i)
            return ((ip // tps) * nk + jnp.maximum(ip % tps - (tps - nk), 0), 0)

        kv_specs.append(pl.BlockSpec((rows, 2 * GROUP_DIM), kv_idx))
        kv_shapes.append(jax.ShapeDtypeStruct((b * keep, 2 * GROUP_DIM), F32))

    qkv_specs, qkv_shapes = [], []
    for _, dil in ATT_GROUPS:
        qkv_specs.append(pl.BlockSpec((3, None, dil, TM // dil, GROUP_DIM),
                                      lambda i: (0, pidx(i) // tps, 0, pidx(i) % tps, 0)))
        qkv_shapes.append(jax.ShapeDtypeStruct((3, b, dil, cfg["s"] // dil, GROUP_DIM), BF16))

    return pl.pallas_call(
        functools.partial(_inproj_kernel, cfg),
        grid=(n_t,),
        in_specs=[pl.BlockSpec((GP, SUBLANES, d), lambda i: (i, 0, 0))]
        + _mod_specs(cfg, 0) + _mod_specs(cfg, 1)
        + [pl.BlockSpec((1, d), lambda i: (0, 0)),
           pl.BlockSpec(w_in_b.shape, lambda i: (0, 0)),
           pl.BlockSpec((1, GROUP_DIM), lambda i: (0, 0)),
           pl.BlockSpec((1, GROUP_DIM), lambda i: (0, 0))],
        out_specs=[pl.BlockSpec((TM, POOL_DIM), lambda i: (pidx(i), 0)),
                   pl.BlockSpec((TM, POOL_DIM), lambda i: (sidx(i), 0))] + qkv_specs
        + [pl.BlockSpec((N_GROUPS, TM, GROUP_DIM), lambda i: (0, sidx(i), 0)),
           pl.BlockSpec((N_GROUPS, TM, 2 * GROUP_DIM), lambda i: (0, sidx(i), 0))] + kv_specs,
        out_shape=[jax.ShapeDtypeStruct((np_, POOL_DIM), F32),
                   jax.ShapeDtypeStruct((ns, POOL_DIM), F32)] + qkv_shapes
        + [jax.ShapeDtypeStruct((N_GROUPS, ns, GROUP_DIM), F32),
           jax.ShapeDtypeStruct((N_GROUPS, ns, 2 * GROUP_DIM), F32)] + kv_shapes,
        scratch_shapes=[pltpu.VMEM((GROUP_DIM // LANES, TM, LANES), F32)],
        compiler_params=_cparams(1, VMEM_LIMIT),
        name="inproj",
    )(x3, mp, ms, mp, ms, gain, w_in_b, qg, kg)


def _rel_bucket_np(dist):
    dist = np.asarray(dist, np.int32)
    max_exact = REL_BUCKETS // 2
    d_f = np.maximum(dist, 1).astype(np.float32)
    ratio = np.log(d_f / np.float32(max_exact)) / np.float32(math.log(REL_MAX_DIST / max_exact))
    large = max_exact + (ratio * np.float32(REL_BUCKETS - max_exact)).astype(np.int32)
    large = np.minimum(large, REL_BUCKETS - 1)
    return np.where(dist < max_exact, dist, large)


def _bias_lookup(rel_bias, g, bucket, ok):
    table = rel_bias[:, g * HEADS:(g + 1) * HEADS].astype(F32)
    vals = jnp.zeros((HEADS,) + bucket.shape, F32)
    for bkt in np.unique(bucket):
        vals = jnp.where((bucket == bkt)[None], table[bkt].reshape((HEADS,) + (1,) * bucket.ndim), vals)
    return jnp.where(ok[None], vals, NEG_INF)


def _prompt_bias_tables(rel_bias):
    qi = np.arange(BAND)[:, None]
    kj = np.arange(2 * BAND)[None, :]
    delta = qi + BAND - kj
    ok = (delta >= 0) & (delta <= BAND)
    ok_first = ok & (kj >= BAND)
    tables = []
    for g, (_, dil) in enumerate(ATT_GROUPS):
        bucket = _rel_bucket_np(np.maximum(delta, 0) * dil)
        both = jnp.stack([_bias_lookup(rel_bias, g, bucket, ok), _bias_lookup(rel_bias, g, bucket, ok_first)])
        tables.append(both.reshape(2, HEADS * BAND, 2 * BAND))
    return tables


def _sample_bias_tables(rel_bias, t_new):
    tables = []
    for g, (win, dil) in enumerate(ATT_GROUPS):
        lb = min(win, PAST_LEN)
        t = np.arange(t_new)[:, None]
        col = np.arange(lb + LANES)[None, :]
        delta = lb + t - col
        ok = (col < lb + t_new) & (delta >= 0) & (delta % dil == 0) & (delta // dil <= win // dil)
        bucket = _rel_bucket_np(np.maximum(delta, 0))
        tables.append(_bias_lookup(rel_bias, g, bucket, ok).reshape(HEADS * t_new, lb + LANES))
    return tables


def _stack_heads(q):
    lh = _lane_head()
    return jnp.concatenate([jnp.where(lh == h, q, jnp.zeros_like(q)) for h in range(HEADS)], axis=0)


def _softmax(s):
    m = jnp.max(s, axis=1, keepdims=True)
    p = jnp.exp(s - m)
    l = jnp.sum(p, axis=1, keepdims=True)
    return p.astype(BF16), l, m + jnp.log(l)


def _unstack_heads(o4, lse4, rows):
    lh = _lane_head()
    o = o4[0:rows]
    lse = jnp.broadcast_to(lse4[0:rows], (rows, GROUP_DIM))
    for h in range(1, HEADS):
        o = jnp.where(lh == h, o4[h * rows:(h + 1) * rows], o)
        lse = jnp.where(lh == h, lse4[h * rows:(h + 1) * rows], lse)
    return o, lse


_NT = (((1,), (1,)), ((), ()))


def _attn_prompt_kernel(nsub, q_ref, kc_ref, kp_ref, vc_ref, vp_ref, bias_ref, o_ref, lse_ref, kbuf, vbuf):
    i = pl.program_id(2)
    kbuf[0:BAND] = kp_ref[...]
    kbuf[BAND:] = kc_ref[...]
    vbuf[0:BAND] = vp_ref[...]
    vbuf[BAND:] = vc_ref[...]
    for s in range(nsub):
        q4 = _stack_heads(q_ref[s * BAND:(s + 1) * BAND])
        kc = kbuf[s * BAND:(s + 2) * BAND]
        vc = vbuf[s * BAND:(s + 2) * BAND]
        logits = lax.dot_general(q4, kc, _NT, preferred_element_type=F32)
        if s == 0:
            bias = bias_ref[jnp.where(i == 0, 1, 0)]
        else:
            bias = bias_ref[0]
        p, l, lse4 = _softmax(logits + bias)
        o4 = jnp.dot(p, vc, preferred_element_type=F32) / l
        o, lse = _unstack_heads(o4, lse4, BAND)
        o_ref[s * BAND:(s + 1) * BAND] = o
        lse_ref[s * BAND:(s + 1) * BAND] = lse


def _attn_prompt(cfg, g, qkv, bias):
    b, s_len = cfg["b"], cfg["s"]
    _, dil = ATT_GROUPS[g]
    l_len = s_len // dil
    tq = min(TM, l_len)
    nsub = tq // BAND
    nq = l_len // tq

    def cur(which):
        return pl.BlockSpec((None, None, None, tq, GROUP_DIM), lambda bi, r, i: (which, bi, r, i, 0))

    def prev(which):
        return pl.BlockSpec((None, None, None, BAND, GROUP_DIM),
                            lambda bi, r, i: (which, bi, r, jnp.maximum(i * nsub - 1, 0), 0))

    out = pl.BlockSpec((None, None, tq, GROUP_DIM), lambda bi, r, i: (bi, r, i, 0))
    return pl.pallas_call(
        functools.partial(_attn_prompt_kernel, nsub),
        grid=(b, dil, nq),
        in_specs=[cur(0), cur(1), prev(1), cur(2), prev(2),
                  pl.BlockSpec(bias.shape, lambda bi, r, i: (0, 0, 0))],
        out_specs=[out, out],
        out_shape=[jax.ShapeDtypeStruct((b, dil, l_len, GROUP_DIM), F32)] * 2,
        scratch_shapes=[pltpu.VMEM((tq + BAND, GROUP_DIM), BF16)] * 2,
        compiler_params=_cparams(3, VMEM_LIMIT),
        name=f"attn_prompt_g{g}",
    )(qkv, qkv, qkv, qkv, qkv, bias)


def _attn_sample_kernel(bs, lb, t_new, q_ref, kvn_ref, cache_ref, bias_ref, o_ref, lse_ref):
    pad = jnp.zeros((LANES - t_new, GROUP_DIM), BF16)
    for j in range(bs):
        kt = cache_ref[j, 0:GROUP_DIM, :].astype(BF16)
        vt = cache_ref[j, GROUP_DIM:2 * GROUP_DIM, :].astype(BF16)
        kvn = kvn_ref[j].astype(BF16)
        kn = jnp.concatenate([kvn[:, :GROUP_DIM], pad], axis=0)
        vn = jnp.concatenate([kvn[:, GROUP_DIM:], pad], axis=0)
        q4 = _stack_heads(q_ref[j].astype(BF16))
        logits = jnp.concatenate(
            [jnp.dot(q4, kt, preferred_element_type=F32),
             lax.dot_general(q4, kn, _NT, preferred_element_type=F32)], axis=1)
        p, l, lse4 = _softmax(logits + bias_ref[...])
        o4 = (lax.dot_general(p[:, :lb], vt, _NT, preferred_element_type=F32)
              + jnp.dot(p[:, lb:], vn, preferred_element_type=F32)) / l
        o, lse = _unstack_heads(o4, lse4, t_new)
        o_ref[j] = o
        lse_ref[j] = lse


def _attn_sample(cfg, g, layer, qs, kvs, cache_t, bias):
    bd, t_new = cfg["bd"], cfg["t"]
    lb = cache_t.shape[-1]
    bs = max(1, min(8, (4 * 1024 * 1024) // (2 * GROUP_DIM * lb * 4)))
    out = pl.BlockSpec((bs, t_new, GROUP_DIM), lambda i: (i, 0, 0))
    o, lse = pl.pallas_call(
        functools.partial(_attn_sample_kernel, bs, lb, t_new),
        grid=(bd // bs,),
        in_specs=[pl.BlockSpec((None, bs, t_new, GROUP_DIM), lambda i: (g, i, 0, 0)),
                  pl.BlockSpec((None, bs, t_new, 2 * GROUP_DIM), lambda i: (g, i, 0, 0)),
                  pl.BlockSpec((None, bs, 2 * GROUP_DIM, lb), lambda i: (layer, i, 0, 0)),
                  pl.BlockSpec(bias.shape, lambda i: (0, 0))],
        out_specs=[out, out],
        out_shape=[jax.ShapeDtypeStruct((bd, t_new, GROUP_DIM), F32)] * 2,
        compiler_params=_cparams(1, VMEM_LIMIT),
        name=f"attn_sample_g{g}",
    )(qs.reshape(N_GROUPS, bd, t_new, GROUP_DIM), kvs.reshape(N_GROUPS, bd, t_new, 2 * GROUP_DIM),
      cache_t, bias)
    return o.reshape(bd * t_new, GROUP_DIM), lse.reshape(bd * t_new, GROUP_DIM)


def _pool_mix(ext_ref, rows, pos, w_ref, scale_ref, o_ref):
    lo = POOL_PAD
    u = ext_ref[:, lo:lo + rows, :]
    acc = u
    sums = {}
    for j in range(1, max(POOL_WINDOWS)):
        acc = acc + ext_ref[:, lo - j:lo - j + rows, :]
        if j + 1 in POOL_WINDOWS:
            sums[j + 1] = acc
    lane_grp = lax.broadcasted_iota(jnp.int32, (1, 1, POOL_DIM), 2) // (POOL_DIM // len(POOL_WINDOWS))
    z = None
    for gi, w in enumerate(POOL_WINDOWS):
        cnt = jnp.minimum(w, pos + 1).astype(F32)
        zw = sums[w] / cnt - u
        z = zw if z is None else jnp.where(lane_grp == gi, zw, z)
    nb = u.shape[0]
    y = jnp.dot(z.reshape(nb * rows, POOL_DIM).astype(BF16), w_ref[...], preferred_element_type=F32)
    o_ref[...] = (y * scale_ref[...]).astype(BF16)


def _pool_prompt_kernel(u_ref, halo_ref, w_ref, scale_ref, o_ref, ext_ref):
    i = pl.program_id(1)
    halo = halo_ref[...]
    ext_ref[:, 0:POOL_PAD, :] = jnp.where(i == 0, jnp.zeros_like(halo), halo)
    ext_ref[:, POOL_PAD:, :] = u_ref[...]
    pos = i * TM + lax.broadcasted_iota(jnp.int32, (1, TM, 1), 1)
    _pool_mix(ext_ref, TM, pos, w_ref, scale_ref, o_ref)


def _pool_sample_kernel(t_new, u_ref, hist_ref, w_ref, scale_ref, o_ref, ext_ref):
    nb = u_ref.shape[0]
    ext_ref[:, 0:POOL_PAD - POOL_HIST, :] = jnp.zeros((nb, POOL_PAD - POOL_HIST, POOL_DIM), F32)
    ext_ref[:, POOL_PAD - POOL_HIST:POOL_PAD, :] = hist_ref[...]
    ext_ref[:, POOL_PAD:, :] = u_ref[...]
    pos = PAST_LEN + lax.broadcasted_iota(jnp.int32, (1, t_new, 1), 1)
    _pool_mix(ext_ref, t_new, pos, w_ref, scale_ref, o_ref)


def _pool_prompt(cfg, u_p, w_bd, scale):
    b, s_len = cfg["b"], cfg["s"]
    u3 = u_p.reshape(b, s_len, POOL_DIM)
    per = TM // POOL_PAD
    return pl.pallas_call(
        _pool_prompt_kernel,
        grid=(b, s_len // TM),
        in_specs=[pl.BlockSpec((1, TM, POOL_DIM), lambda bi, i: (bi, i, 0)),
                  pl.BlockSpec((1, POOL_PAD, POOL_DIM), lambda bi, i: (bi, jnp.maximum(i * per - 1, 0), 0)),
                  pl.BlockSpec((POOL_DIM, POOL_DIM), lambda bi, i: (0, 0)),
                  pl.BlockSpec((1, POOL_DIM), lambda bi, i: (0, 0))],
        out_specs=pl.BlockSpec((TM, POOL_DIM), lambda bi, i: (bi * (s_len // TM) + i, 0)),
        out_shape=jax.ShapeDtypeStruct((b * s_len, POOL_DIM), BF16),
        scratch_shapes=[pltpu.VMEM((1, TM + POOL_PAD, POOL_DIM), F32)],
        compiler_params=_cparams(2, VMEM_LIMIT),
        name="pool_prompt",
    )(u3, u3, w_bd, scale)


def _pool_sample(cfg, layer, u_s, state_pool, w_bd, scale):
    bd, t_new = cfg["bd"], cfg["t"]
    u3 = u_s.reshape(bd, t_new, POOL_DIM)
    nb = GP
    return pl.pallas_call(
        functools.partial(_pool_sample_kernel, t_new),
        grid=(bd // nb,),
        in_specs=[pl.BlockSpec((nb, t_new, POOL_DIM), lambda i: (i, 0, 0)),
                  pl.BlockSpec((None, nb, POOL_HIST, POOL_DIM), lambda i: (layer, i, 0, 0)),
                  pl.BlockSpec((POOL_DIM, POOL_DIM), lambda i: (0, 0)),
                  pl.BlockSpec((1, POOL_DIM), lambda i: (0, 0))],
        out_specs=pl.BlockSpec((nb * t_new, POOL_DIM), lambda i: (i, 0)),
        out_shape=jax.ShapeDtypeStruct((bd * t_new, POOL_DIM), BF16),
        scratch_shapes=[pltpu.VMEM((nb, t_new + POOL_PAD, POOL_DIM), F32)],
        compiler_params=_cparams(1, VMEM_LIMIT),
        name="pool_sample",
    )(u3, state_pool, w_bd, scale)


def _outproj_kernel(cfg, x_ref, g1p_ref, g1s_ref, shp_ref, shs_ref, scp_ref, scs_ref, pp_ref, ps_ref, *rest):
    att_refs = rest[:4 * N_GROUPS]
    (gain_ref, wo_ref, wr_ref, br_ref, x1_ref, h2_ref, route_ref, cnt_ref,
     run_ref, nat_ref) = rest[4 * N_GROUPS:]
    i = pl.program_id(0)
    n_pt, d = cfg["n_pt"], cfg["d"]
    is_p = i < n_pt

    halves = GROUP_DIM // LANES

    def put(slot, val, rows=None):
        for half in range(halves):
            piece = val[:, half * LANES:(half + 1) * LANES]
            if rows is None:
                nat_ref[slot, half] = piece
            else:
                nat_ref[slot, half, rows, :] = piece

    @pl.when(is_p)
    def _():
        for g, (_, dil) in enumerate(ATT_GROUPS):
            for which in range(2):
                src = att_refs[4 * g + which]
                if dil == 1:
                    put(2 * g + which, src[0])
                else:
                    for r in range(dil):
                        put(2 * g + which, src[r], pl.ds(r, TM // dil, stride=dil))

    @pl.when(jnp.logical_not(is_p))
    def _():
        for g in range(N_GROUPS):
            for which in range(2):
                put(2 * g + which, att_refs[4 * g + 2 + which][...])

    def get(slot):
        return jnp.concatenate([nat_ref[slot, half] for half in range(halves)], axis=1)

    o = [get(2 * g) for g in range(N_GROUPS)]
    lse = [get(2 * g + 1) for g in range(N_GROUPS)]
    mx = functools.reduce(jnp.maximum, lse)
    e = [jnp.exp(l - mx) for l in lse]
    attn = sum(eg * og for eg, og in zip(e, o)) / sum(e)
    pool = jnp.where(is_p, pp_ref[...], ps_ref[...])
    cat = jnp.concatenate([pool, attn.astype(BF16)], axis=1)
    y = jnp.dot(cat, wo_ref[...], preferred_element_type=F32)

    g1 = jnp.where(is_p, g1p_ref[...], g1s_ref[...])
    x1 = x_ref[...] + g1 * y.reshape(GP, SUBLANES, d)
    x1_ref[...] = x1
    sh = jnp.where(is_p, shp_ref[...], shs_ref[...])
    sc = jnp.where(is_p, scp_ref[...], scs_ref[...])
    h2 = _rmsnorm_mod(x1, gain_ref[...], sc, sh).reshape(TM, d)
    _rows_to_slabs(h2_ref, h2)

    h_hi = h2.astype(BF16)
    h_lo = (h2 - h_hi.astype(F32)).astype(BF16)
    wr = wr_ref[...]
    w_hi = wr.astype(BF16)
    w_lo = (wr - w_hi.astype(F32)).astype(BF16)
    logits = (jnp.dot(h_hi, w_hi, preferred_element_type=F32)
              + jnp.dot(h_lo, w_hi, preferred_element_type=F32)
              + jnp.dot(h_hi, w_lo, preferred_element_type=F32)) + br_ref[...]

    lane = lax.broadcasted_iota(jnp.int32, (TM, LANES), 1).astype(F32)
    vals = logits
    top_v, top_i = [], []
    onehot = jnp.zeros((TM, LANES), F32)
    for _ in range(TOP_K):
        m = jnp.max(vals, axis=1, keepdims=True)
        idx = jnp.min(jnp.where(vals == m, lane, float(LANES)), axis=1, keepdims=True)
        hit = lane == idx
        vals = jnp.where(hit, -jnp.inf, vals)
        onehot = jnp.where(hit, 1.0, onehot)
        top_v.append(m)
        top_i.append(idx)
    ev = [jnp.exp(v - top_v[0]) for v in top_v]
    den = sum(ev)

    @pl.when(i == 0)
    def _():
        run_ref[...] = jnp.zeros_like(run_ref)

    row = lax.broadcasted_iota(jnp.int32, (TM, TM), 0)
    col = lax.broadcasted_iota(jnp.int32, (TM, TM), 1)
    before = (col < row).astype(BF16)
    rank_all = jnp.dot(before, onehot.astype(BF16), preferred_element_type=F32) + run_ref[...]
    route = jnp.zeros((TM, LANES), F32)
    for k in range(TOP_K):
        rank_k = jnp.sum(jnp.where(lane == top_i[k], rank_all, 0.0), axis=1, keepdims=True)
        route = jnp.where(lane == float(k), top_i[k], route)
        route = jnp.where(lane == float(TOP_K + k), ev[k] / den, route)
        route = jnp.where(lane == float(2 * TOP_K + k), rank_k, route)
    route_ref[...] = route
    run_ref[...] = run_ref[...] + jnp.sum(onehot, axis=0, keepdims=True)
    cnt_ref[...] = run_ref[...]


def _outproj(cfg, x3, mp, ms, pool_p, pool_s, att, gain, w_out_b, w_r, b_r):
    d, n_pt, n_t = cfg["d"], cfg["n_pt"], cfg["n_t"]
    n = cfg["n"]
    tps = cfg["tps"]
    pidx = lambda i: (jnp.minimum(i, n_pt - 1), 0)
    sidx = lambda i: (jnp.maximum(i - n_pt, 0), 0)
    pspec = pl.BlockSpec((TM, GROUP_DIM), pidx)
    sspec = pl.BlockSpec((TM, GROUP_DIM), sidx)
    att_specs, att_args = [], []
    for (_, dil), (o_p, lse_p, o_s, lse_s) in zip(ATT_GROUPS, att):
        rspec = pl.BlockSpec((None, dil, TM // dil, GROUP_DIM),
                             lambda i: (pidx(i)[0] // tps, 0, pidx(i)[0] % tps, 0))
        att_specs += [rspec, rspec, sspec, sspec]
        att_args += [o_p, lse_p, o_s, lse_s]
    full = lambda a: pl.BlockSpec(a.shape, lambda i: (0,) * a.ndim)
    return pl.pallas_call(
        functools.partial(_outproj_kernel, cfg),
        grid=(n_t,),
        in_specs=[pl.BlockSpec((GP, SUBLANES, d), lambda i: (i, 0, 0))]
        + _mod_specs(cfg, 2) + _mod_specs(cfg, 3) + _mod_specs(cfg, 4)
        + [pspec, sspec] + att_specs + [full(gain), full(w_out_b), full(w_r), full(b_r)],
        out_specs=[pl.BlockSpec((GP, SUBLANES, d), lambda i: (i, 0, 0)),
                   pl.BlockSpec((TM * SUBLANES, LANES), lambda i: (i, 0)),
                   pl.BlockSpec((TM, LANES), lambda i: (i, 0)),
                   pl.BlockSpec((1, LANES), lambda i: (0, 0))],
        out_shape=[jax.ShapeDtypeStruct(x3.shape, F32),
                   jax.ShapeDtypeStruct((n * SUBLANES, LANES), F32),
                   jax.ShapeDtypeStruct((n, LANES), F32),
                   jax.ShapeDtypeStruct((1, LANES), F32)],
        scratch_shapes=[pltpu.VMEM((1, LANES), F32),
                        pltpu.VMEM((2 * N_GROUPS, GROUP_DIM // LANES, TM, LANES), F32)],
        compiler_params=_cparams(1, VMEM_LIMIT),
        name="outproj_router",
    )(x3, mp, ms, mp, ms, mp, ms, pool_p, pool_s, *att_args, gain, w_out_b, w_r, b_r)


def _rows_from_slabs(ref):
    rows = ref.shape[0] // SUBLANES
    return jnp.concatenate([ref[pl.ds(j, rows, stride=SUBLANES), :] for j in range(SUBLANES)], axis=1)


def _rows_to_slabs(ref, val):
    for j in range(SUBLANES):
        ref[pl.ds(j, val.shape[0], stride=SUBLANES), :] = val[:, j * LANES:(j + 1) * LANES]


def _dispatch_kernel(n_tok, zt_ref, pos_ref, h_ref, x_hbm, dst_ref, zbuf, zsem, sem):
    i = pl.program_id(0)
    rows = h_ref.shape[0] // SUBLANES
    n_asg = n_tok * TOP_K

    @pl.when(i == 0)
    def _():
        zbuf[...] = jnp.zeros_like(zbuf)

        def fill(z):
            return pltpu.make_async_copy(zbuf, x_hbm.at[pl.ds(zt_ref[z] * MOE_TM, MOE_TM)], zsem)

        for z in range(zt_ref.shape[0]):
            pl.when(zt_ref[z] >= 0)(lambda z=z: fill(z).start())

        def pad_rows(t, carry):
            base = n_asg + ((t + 1) % 2) * MOE_TM
            for r in range(MOE_TM):
                dst_ref[t * MOE_TM + r] = base + r
            return carry

        lax.fori_loop(0, dst_ref.shape[0] // MOE_TM, pad_rows, 0)
        for z in range(zt_ref.shape[0]):
            pl.when(zt_ref[z] >= 0)(lambda z=z: fill(z).wait())

    for r in range(rows):
        for k in range(TOP_K):
            slot = pos_ref[0, r * TOP_K + k]
            dst_ref[MOE_TM + slot] = k * n_tok + i * rows + r
            pltpu.make_async_copy(h_ref.at[pl.ds(r * SUBLANES, SUBLANES)], x_hbm.at[slot],
                                  sem.at[k]).start(priority=k % 2)
    for k in range(TOP_K):
        pltpu.make_async_copy(h_ref, h_ref, sem.at[k]).wait()


def _dispatch(cfg, pos, zero_tiles, h2, m_pad):
    n_t = cfg["n_t"]
    grid_spec = pltpu.PrefetchScalarGridSpec(
        num_scalar_prefetch=1,
        grid=(n_t,),
        in_specs=[pl.BlockSpec((None, 1, TM * TOP_K), lambda i, zt: (i, 0, 0), memory_space=pltpu.SMEM),
                  pl.BlockSpec((TM * SUBLANES, LANES), lambda i, zt: (i, 0))],
        out_specs=[pl.BlockSpec(memory_space=pl.ANY), pl.BlockSpec(memory_space=pltpu.SMEM)],
        scratch_shapes=[pltpu.VMEM((MOE_TM, SUBLANES, LANES), F32),
                        pltpu.SemaphoreType.DMA(()),
                        pltpu.SemaphoreType.DMA((TOP_K,))],
    )
    return pl.pallas_call(
        functools.partial(_dispatch_kernel, cfg["n"]),
        grid_spec=grid_spec,
        out_shape=[jax.ShapeDtypeStruct((m_pad, SUBLANES, LANES), F32),
                   jax.ShapeDtypeStruct((m_pad + MOE_TM,), jnp.int32)],
        compiler_params=_cparams(1, VMEM_LIMIT),
        name="moe_dispatch",
    )(zero_tiles, pos.reshape(n_t, 1, TM * TOP_K), h2)


N_YBUF = 3


def _moe_kernel(te_ref, prv_ref, last_ref, x_ref, w1_ref, b1_ref, w2_ref, b2_ref, y_hbm,
                y0, y1, y2, w1b, w2b, ssem):
    i = pl.program_id(0)
    n = pl.num_programs(0)
    ff = w2_ref.shape[0]
    ys = (y0, y1, y2)

    def start_scatter(idx_ref, q):
        for r in range(MOE_TM):
            pltpu.make_async_copy(ys[q].at[pl.ds(r * SUBLANES, SUBLANES)], y_hbm.at[idx_ref[0, r]],
                                  ssem.at[q]).start(priority=r % 2)

    def wait_scatter(q):
        pltpu.make_async_copy(ys[q], ys[q], ssem.at[q]).wait()

    @pl.when(i == 0)
    def _():
        n_asg = y_hbm.shape[0] - 2 * MOE_TM
        for q in range(N_YBUF):
            ys[q][...] = jnp.zeros_like(ys[q])
        for q in range(2):
            for r in range(MOE_TM):
                pltpu.make_async_copy(ys[q].at[pl.ds(r * SUBLANES, SUBLANES)],
                                      y_hbm.at[n_asg + q * MOE_TM + r], ssem.at[q]).start()
            wait_scatter(q)

    @pl.when(jnp.logical_or(i == 0, te_ref[i] != te_ref[jnp.maximum(i - 1, 0)]))
    def _():
        w1b[...] = w1_ref[...].astype(BF16)
        w2b[...] = w2_ref[...].astype(BF16)

    def step(q):
        q_prev, q_old = (q + 2) % N_YBUF, (q + 1) % N_YBUF
        start_scatter(prv_ref, q_prev)
        x = _rows_from_slabs(x_ref).astype(BF16)
        h1 = jnp.dot(x, w1b[...], preferred_element_type=F32) + b1_ref[...]
        gate = jnp.minimum(h1[:, :ff], SWIGLU_LIMIT)
        up = jnp.clip(h1[:, ff:], -SWIGLU_LIMIT, SWIGLU_LIMIT)
        act = gate * jax.nn.sigmoid(SWIGLU_ALPHA * gate) * (up + 1.0)
        _rows_to_slabs(ys[q], jnp.dot(act.astype(BF16), w2b[...], preferred_element_type=F32) + b2_ref[...])

        @pl.when(i >= 1)
        def _():
            wait_scatter(q_old)

        @pl.when(i == n - 1)
        def _():
            wait_scatter(q_prev)
            start_scatter(last_ref, q)
            wait_scatter(q)

    for q in range(N_YBUF):
        pl.when(i % N_YBUF == q)(functools.partial(step, q))


def _moe(layer, n_tiles, tile_expert, slot_dst, x_sorted, w1, b1, w2, b2, n_rows_out):
    depth, n_exp, d, ff2 = w1.shape
    ff = w2.shape[2]
    smem_tile = lambda f: pl.BlockSpec((None, 1, MOE_TM), f, memory_space=pltpu.SMEM)
    grid_spec = pltpu.PrefetchScalarGridSpec(
        num_scalar_prefetch=1,
        grid=(n_tiles,),
        in_specs=[smem_tile(lambda i, te: (i, 0, 0)),
                  smem_tile(lambda i, te: (n_tiles, 0, 0)),
                  pl.BlockSpec((MOE_TM * SUBLANES, LANES), lambda i, te: (i, 0)),
                  pl.BlockSpec((None, None, d, ff2), lambda i, te: (layer, te[i], 0, 0)),
                  pl.BlockSpec((None, None, 1, ff2), lambda i, te: (layer, te[i], 0, 0)),
                  pl.BlockSpec((None, None, ff, d), lambda i, te: (layer, te[i], 0, 0)),
                  pl.BlockSpec((None, None, 1, d), lambda i, te: (layer, te[i], 0, 0))],
        out_specs=pl.BlockSpec(memory_space=pl.ANY),
        scratch_shapes=[pltpu.VMEM((MOE_TM * SUBLANES, LANES), F32)] * N_YBUF
        + [pltpu.VMEM((d, ff2), BF16),
           pltpu.VMEM((ff, d), BF16),
           pltpu.SemaphoreType.DMA((N_YBUF,))],
    )
    dst3 = slot_dst.reshape(n_tiles + 1, 1, MOE_TM)
    return pl.pallas_call(
        _moe_kernel,
        grid_spec=grid_spec,
        out_shape=jax.ShapeDtypeStruct((n_rows_out, SUBLANES, LANES), F32),
        compiler_params=_cparams(1, VMEM_LIMIT),
        name="moe_experts",
    )(tile_expert, dst3, dst3, x_sorted.reshape(x_sorted.shape[0] * SUBLANES, LANES),
      w1, b1.reshape(depth, n_exp, 1, ff2), w2, b2.reshape(depth, n_exp, 1, d))


def _moe_plan(cfg, route, counts):
    n = cfg["n"]
    m = n * TOP_K
    n_tiles = m // MOE_TM + N_EXPERTS
    m_pad = n_tiles * MOE_TM
    top_i = route[:, 0:TOP_K].astype(jnp.int32)
    rank = route[:, 2 * TOP_K:3 * TOP_K].astype(jnp.int32)
    cnt = counts[0, :N_EXPERTS].astype(jnp.int32)
    tiles_e = (cnt + MOE_TM - 1) // MOE_TM
    tile_end = jnp.cumsum(tiles_e)
    pstart = (tile_end - tiles_e) * MOE_TM
    expert = jnp.arange(N_EXPERTS, dtype=jnp.int32)
    pos = jnp.sum(jnp.where(top_i[..., None] == expert, pstart, 0), axis=-1) + rank
    t = jnp.arange(n_tiles, dtype=jnp.int32)
    te = jnp.sum((tile_end[None, :] <= t[:, None]).astype(jnp.int32), axis=1)
    last_used = jnp.sum((tile_end <= tile_end[-1] - 1).astype(jnp.int32))
    te = jnp.minimum(te, last_used).astype(jnp.int32)
    n_used = tile_end[-1]
    spare = n_used + jnp.arange(n_tiles - m // MOE_TM, dtype=jnp.int32)
    zero_tiles = jnp.concatenate([jnp.where(tiles_e > 0, tile_end - 1, -1),
                                  jnp.where(spare < n_tiles, spare, -1)]).astype(jnp.int32)
    return n_tiles, te, pos, zero_tiles, m + 2 * MOE_TM


def _combine_kernel(cfg, x_ref, g2p_ref, g2s_ref, route_ref, *rest):
    y_refs, o_ref = rest[:TOP_K], rest[TOP_K]
    i = pl.program_id(0)
    is_p = i < cfg["n_pt"]
    route = route_ref[...]
    acc = route[:, TOP_K:TOP_K + 1] * _rows_from_slabs(y_refs[0])
    for k in range(1, TOP_K):
        acc = acc + route[:, TOP_K + k:TOP_K + k + 1] * _rows_from_slabs(y_refs[k])
    g2 = jnp.where(is_p, g2p_ref[...], g2s_ref[...])
    o_ref[...] = x_ref[...] + g2 * acc.reshape(GP, SUBLANES, cfg["d"])


def _combine(cfg, x1, mp, ms, route, ybuf):
    d, n_t = cfg["d"], cfg["n_t"]
    y_specs = [pl.BlockSpec((TM * SUBLANES, LANES), lambda i, k=k: (k * n_t + i, 0)) for k in range(TOP_K)]
    ybuf = ybuf.reshape(ybuf.shape[0] * SUBLANES, LANES)
    return pl.pallas_call(
        functools.partial(_combine_kernel, cfg),
        grid=(n_t,),
        in_specs=[pl.BlockSpec((GP, SUBLANES, d), lambda i: (i, 0, 0))] + _mod_specs(cfg, 5)
        + [pl.BlockSpec((TM, LANES), lambda i: (i, 0))] + y_specs,
        out_specs=pl.BlockSpec((GP, SUBLANES, d), lambda i: (i, 0, 0)),
        out_shape=jax.ShapeDtypeStruct(x1.shape, F32),
        compiler_params=_cparams(1, VMEM_LIMIT),
        name="moe_combine",
    )(x1, mp, ms, route, *([ybuf] * TOP_K))


def kernel(x_prompt, x_sample, cache_kv_w128_d1, cache_kv_w512_d4, cache_kv_w2048_d16, state_pool,
           c_prompt, c_sample, rel_bias, norm_mix, norm_ffn, w_ada, b_ada, w_in, q_norm, k_norm,
           pool_w, pool_scale, w_out, w_router, b_router, w_expert_in, b_expert_in,
           w_expert_out, b_expert_out):
    b, s_len, d = x_prompt.shape
    bd, t_new, _ = x_sample.shape
    depth = w_in.shape[0]
    np_, ns = b * s_len, bd * t_new
    assert t_new == SUBLANES and s_len % TM == 0 and ns % TM == 0
    assert all((s_len // dil) % BAND == 0 for _, dil in ATT_GROUPS)
    cfg = dict(b=b, s=s_len, d=d, bd=bd, t=t_new, np=np_, ns=ns, n=np_ + ns,
               n_pt=np_ // TM, n_t=(np_ + ns) // TM, tps=s_len // TM)
    kv_keep = []
    for win, _ in ATT_GROUPS:
        keep = min(win, s_len)
        assert keep % TM == 0 or TM % keep == 0
        kv_keep.append((keep, max(keep // TM, 1)))
    cfg["kv_keep"] = tuple(kv_keep)
    caches_t = [jnp.transpose(c, (0, 1, 3, 4, 5, 2)).reshape(depth, bd, 2 * GROUP_DIM, c.shape[2])
                for c in (cache_kv_w128_d1, cache_kv_w512_d4, cache_kv_w2048_d16)]

    mods = _ada_mods(jnp.concatenate([c_prompt, c_sample], axis=0), w_ada, b_ada)
    bias_p = _prompt_bias_tables(rel_bias)
    bias_s = _sample_bias_tables(rel_bias, t_new)
    w_in_b = w_in.astype(BF16)
    w_out_b = w_out.astype(BF16)
    eye = jnp.eye(len(POOL_WINDOWS), dtype=F32)
    w_pool_bd = (eye[None, :, None, :, None] * pool_w[:, :, :, None, :]).reshape(depth, POOL_DIM, POOL_DIM)
    w_pool_bd = w_pool_bd.astype(BF16)
    w_r_pad = jnp.pad(w_router, ((0, 0), (0, 0), (0, LANES - N_EXPERTS)))
    b_r_pad = jnp.pad(b_router, ((0, 0), (0, LANES - N_EXPERTS)), constant_values=NEG_INF)

    x3 = jnp.concatenate([x_prompt.reshape(np_, d), x_sample.reshape(ns, d)], axis=0)
    x3 = x3.reshape((np_ + ns) // SUBLANES, SUBLANES, d)

    kv_p = [[] for _ in ATT_GROUPS]
    kv_s = [[] for _ in ATT_GROUPS]
    pool_p_state, pool_s_state = [], []
    for l in range(depth):
        mp = mods[l, :b].reshape(b, 1, 6 * d)
        ms = mods[l, b:].reshape(bd, 1, 6 * d)
        qg = jnp.tile(q_norm[l], HEADS).reshape(1, GROUP_DIM)
        kg = jnp.tile(k_norm[l], HEADS).reshape(1, GROUP_DIM)
        (u_p, u_s, qkv0, qkv1, qkv2, qs, kvs, *kvp) = _inproj(
            cfg, x3, mp, ms, norm_mix[l].reshape(1, d), w_in_b[l], qg, kg)

        att = []
        for g, qkv in enumerate((qkv0, qkv1, qkv2)):
            o_p, lse_p = _attn_prompt(cfg, g, qkv, bias_p[g])
            o_s, lse_s = _attn_sample(cfg, g, l, qs, kvs, caches_t[g], bias_s[g])
            att.append((o_p, lse_p, o_s, lse_s))
            keep = cfg["kv_keep"][g][0]
            kv_p[g].append(kvp[g].reshape(b, keep, 2, HEADS, HEAD_DIM))
            kv_s[g].append(kvs[g].reshape(bd, t_new, 2, HEADS, HEAD_DIM))

        scale = pool_scale[l].reshape(1, POOL_DIM)
        pool_p = _pool_prompt(cfg, u_p, w_pool_bd[l], scale)
        pool_s = _pool_sample(cfg, l, u_s, state_pool, w_pool_bd[l], scale)
        pool_p_state.append(u_p.reshape(b, s_len, POOL_DIM)[:, s_len - POOL_HIST:])
        pool_s_state.append(jnp.concatenate(
            [state_pool[l], u_s.reshape(bd, t_new, POOL_DIM)], axis=1)[:, -POOL_HIST:])

        x1, h2, route, counts = _outproj(cfg, x3, mp, ms, pool_p, pool_s, att,
                                         norm_ffn[l].reshape(1, d), w_out_b[l], w_r_pad[l],
                                         b_r_pad[l].reshape(1, LANES))
        n_tiles, te, pos, zero_tiles, n_rows_out = _moe_plan(cfg, route, counts)
        x_sorted, slot_dst = _dispatch(cfg, pos, zero_tiles, h2, n_tiles * MOE_TM)
        ybuf = _moe(l, n_tiles, te, slot_dst, x_sorted,
                    w_expert_in, b_expert_in, w_expert_out, b_expert_out, n_rows_out)
        x3 = _combine(cfg, x1, mp, ms, route, ybuf)

    x_all = x3.reshape(np_ + ns, d)
    y_prompt = x_all[:np_].reshape(b, s_len, d)
    y_sample = x_all[np_:].reshape(bd, t_new, d)
    return (y_prompt, y_sample,
            jnp.stack(kv_p[0]), jnp.stack(kv_p[1]), jnp.stack(kv_p[2]), jnp.stack(pool_p_state),
            jnp.stack(kv_s[0]), jnp.stack(kv_s[1]), jnp.stack(kv_s[2]), jnp.stack(pool_s_state))
```

```python
import functools
import math

import numpy as np
import jax
import jax.numpy as jnp
from jax import lax
from jax.experimental import pallas as pl
from jax.experimental.pallas import tpu as pltpu

F32 = jnp.float32
BF16 = jnp.bfloat16

HEAD_DIM = 64
HEADS = 4
GROUP_DIM = HEADS * HEAD_DIM
ATT_GROUPS = ((128, 1), (512, 4), (2048, 16))
N_GROUPS = len(ATT_GROUPS)
BAND = 128
POOL_WINDOWS = (2, 4, 8, 16)
POOL_HIST = 15
POOL_DIM = 256
POOL_PAD = 16
REL_BUCKETS = 32
REL_MAX_DIST = 2048
N_EXPERTS = 32
TOP_K = 4
SWIGLU_LIMIT = 7.0
SWIGLU_ALPHA = 1.702
EPS = 1e-6
NEG_INF = -1e30
PAST_LEN = 2048

SUBLANES = 8
LANES = 128
TM = 512
GP = TM // SUBLANES
MOE_TM = 512
VMEM_LIMIT = 58 * 1024 * 1024


def _cparams(n_axes, vmem=None):
    return pltpu.CompilerParams(dimension_semantics=("arbitrary",) * n_axes,
                                vmem_limit_bytes=vmem)


def _lane_head(width=GROUP_DIM):
    return lax.broadcasted_iota(jnp.int32, (1, width), 1) // HEAD_DIM


def _ada_kernel(c_ref, w_ref, b_ref, o_ref):
    c = c_ref[...]
    a = (c * jax.nn.sigmoid(c)).astype(BF16)
    o_ref[...] = jnp.dot(a, w_ref[...].astype(BF16), preferred_element_type=F32) + b_ref[...]


def _ada_mods(c_all, w_ada, b_ada):
    depth, d, d6 = w_ada.shape
    bc = c_all.shape[0]
    tn = d6 // 4
    return pl.pallas_call(
        _ada_kernel,
        grid=(depth, d6 // tn),
        in_specs=[pl.BlockSpec((bc, d), lambda l, j: (0, 0)),
                  pl.BlockSpec((None, d, tn), lambda l, j: (l, 0, j)),
                  pl.BlockSpec((None, 1, tn), lambda l, j: (l, 0, j))],
        out_specs=pl.BlockSpec((None, bc, tn), lambda l, j: (l, 0, j)),
        out_shape=jax.ShapeDtypeStruct((depth, bc, d6), F32),
        compiler_params=_cparams(2, VMEM_LIMIT),
        name="ada_mods",
    )(c_all, w_ada, b_ada.reshape(depth, 1, d6))


def _mod_specs(cfg, col):
    d = cfg["d"]
    n_pt, tps, b = cfg["n_pt"], cfg["tps"], cfg["b"]
    return [pl.BlockSpec((1, 1, d), lambda i: (jnp.minimum(i // tps, b - 1), 0, col)),
            pl.BlockSpec((GP, 1, d), lambda i: (jnp.maximum(i - n_pt, 0), 0, col))]


def _rmsnorm_mod(x, g, sc, sh):
    inv = lax.rsqrt(jnp.mean(x * x, axis=-1, keepdims=True) + EPS)
    return x * inv * g * (1.0 + sc) + sh


def _inproj_kernel(cfg, x_ref, shp_ref, shs_ref, scp_ref, scs_ref, g_ref, w_ref, qg_ref, kg_ref,
                   up_ref, us_ref, qkv0_ref, qkv1_ref, qkv2_ref, qs_ref, kvs_ref,
                   kvp0_ref, kvp1_ref, kvp2_ref, res_ref):
    qkv_refs = (qkv0_ref, qkv1_ref, qkv2_ref)
    kvp_refs = (kvp0_ref, kvp1_ref, kvp2_ref)
    i = pl.program_id(0)
    n_pt, tps, d = cfg["n_pt"], cfg["tps"], cfg["d"]
    is_p = i < n_pt
    sh = jnp.where(is_p, shp_ref[...], shs_ref[...])
    sc = jnp.where(is_p, scp_ref[...], scs_ref[...])
    h = _rmsnorm_mod(x_ref[...], g_ref[...], sc, sh)
    z = jnp.dot(h.reshape(TM, d).astype(BF16), w_ref[...], preferred_element_type=F32)

    r = lax.broadcasted_iota(jnp.int32, (GROUP_DIM, GROUP_DIM), 0) // HEAD_DIM
    c = lax.broadcasted_iota(jnp.int32, (GROUP_DIM, GROUP_DIM), 1) // HEAD_DIM
    head_ones = (r == c).astype(BF16)

    def head_norm(t, gain):
        ss = jnp.dot((t * t).astype(BF16), head_ones, preferred_element_type=F32) * (1.0 / HEAD_DIM)
        return t * lax.rsqrt(ss + EPS) * gain

    u = z[:, :POOL_DIM]
    att = d - POOL_DIM
    q, k, v = [], [], []
    for g in range(N_GROUPS):
        lo = POOL_DIM + g * GROUP_DIM
        q.append(head_norm(z[:, lo:lo + GROUP_DIM], qg_ref[...]) * (HEAD_DIM ** -0.5))
        k.append(head_norm(z[:, lo + att:lo + att + GROUP_DIM], kg_ref[...]))
        v.append(z[:, lo + 2 * att:lo + 2 * att + GROUP_DIM])

    @pl.when(is_p)
    def _():
        up_ref[...] = u
        for g, (_, dil) in enumerate(ATT_GROUPS):
            for which, val in enumerate((q[g], k[g], v[g])):
                if dil == 1:
                    qkv_refs[g][which, 0] = val.astype(BF16)
                else:
                    for half in range(GROUP_DIM // LANES):
                        res_ref[half] = val[:, half * LANES:(half + 1) * LANES]
                    for r in range(dil):
                        rows = [res_ref[half, pl.ds(r, TM // dil, stride=dil), :]
                                for half in range(GROUP_DIM // LANES)]
                        qkv_refs[g][which, r] = jnp.concatenate(rows, axis=1).astype(BF16)

    @pl.when(jnp.logical_not(is_p))
    def _():
        us_ref[...] = u
        for g in range(N_GROUPS):
            qs_ref[g] = q[g]
            kvs_ref[g, :, :GROUP_DIM] = k[g]
            kvs_ref[g, :, GROUP_DIM:] = v[g]

    j = jnp.minimum(i, n_pt - 1) % tps
    for g, (keep, nk) in enumerate(cfg["kv_keep"]):
        rows = min(keep, TM)

        @pl.when(jnp.logical_and(is_p, j >= tps - nk))
        def _(g=g, rows=rows):
            kvp_refs[g][:, :GROUP_DIM] = k[g][TM - rows:]
            kvp_refs[g][:, GROUP_DIM:] = v[g][TM - rows:]


def _inproj(cfg, x3, mp, ms, gain, w_in_b, qg, kg):
    d, n_pt, n_t, tps, b = cfg["d"], cfg["n_pt"], cfg["n_t"], cfg["tps"], cfg["b"]
    np_, ns = cfg["np"], cfg["ns"]
    pidx = lambda i: jnp.minimum(i, n_pt - 1)
    sidx = lambda i: jnp.maximum(i - n_pt, 0)

    kv_specs, kv_shapes = [], []
    for keep, nk in cfg["kv_keep"]:
        rows = min(keep, TM)

        def kv_idx(i, nk=nk):
            ip = pidx(i)
            return ((ip // tps) * nk + jnp.maximum(ip % tps - (tps - nk), 0), 0)

        kv_specs.append(pl.BlockSpec((rows, 2 * GROUP_DIM), kv_idx))
        kv_shapes.append(jax.ShapeDtypeStruct((b * keep, 2 * GROUP_DIM), F32))

    qkv_specs, qkv_shapes = [], []
    for _, dil in ATT_GROUPS:
        qkv_specs.append(pl.BlockSpec((3, None, dil, TM // dil, GROUP_DIM),
                                      lambda i: (0, pidx(i) // tps, 0, pidx(i) % tps, 0)))
        qkv_shapes.append(jax.ShapeDtypeStruct((3, b, dil, cfg["s"] // dil, GROUP_DIM), BF16))

    return pl.pallas_call(
        functools.partial(_inproj_kernel, cfg),
        grid=(n_t,),
        in_specs=[pl.BlockSpec((GP, SUBLANES, d), lambda i: (i, 0, 0))]
        + _mod_specs(cfg, 0) + _mod_specs(cfg, 1)
        + [pl.BlockSpec((1, d), lambda i: (0, 0)),
           pl.BlockSpec(w_in_b.shape, lambda i: (0, 0)),
           pl.BlockSpec((1, GROUP_DIM), lambda i: (0, 0)),
           pl.BlockSpec((1, GROUP_DIM), lambda i: (0, 0))],
        out_specs=[pl.BlockSpec((TM, POOL_DIM), lambda i: (pidx(i), 0)),
                   pl.BlockSpec((TM, POOL_DIM), lambda i: (sidx(i), 0))] + qkv_specs
        + [pl.BlockSpec((N_GROUPS, TM, GROUP_DIM), lambda i: (0, sidx(i), 0)),
           pl.BlockSpec((N_GROUPS, TM, 2 * GROUP_DIM), lambda i: (0, sidx(i), 0))] + kv_specs,
        out_shape=[jax.ShapeDtypeStruct((np_, POOL_DIM), F32),
                   jax.ShapeDtypeStruct((ns, POOL_DIM), F32)] + qkv_shapes
        + [jax.ShapeDtypeStruct((N_GROUPS, ns, GROUP_DIM), F32),
           jax.ShapeDtypeStruct((N_GROUPS, ns, 2 * GROUP_DIM), F32)] + kv_shapes,
        scratch_shapes=[pltpu.VMEM((GROUP_DIM // LANES, TM, LANES), F32)],
        compiler_params=_cparams(1, VMEM_LIMIT),
        name="inproj",
    )(x3, mp, ms, mp, ms, gain, w_in_b, qg, kg)


def _rel_bucket_np(dist):
    dist = np.asarray(dist, np.int32)
    max_exact = REL_BUCKETS // 2
    d_f = np.maximum(dist, 1).astype(np.float32)
    ratio = np.log(d_f / np.float32(max_exact)) / np.float32(math.log(REL_MAX_DIST / max_exact))
    large = max_exact + (ratio * np.float32(REL_BUCKETS - max_exact)).astype(np.int32)
    large = np.minimum(large, REL_BUCKETS - 1)
    return np.where(dist < max_exact, dist, large)


def _bias_lookup(rel_bias, g, bucket, ok):
    table = rel_bias[:, g * HEADS:(g + 1) * HEADS].astype(F32)
    vals = jnp.zeros((HEADS,) + bucket.shape, F32)
    for bkt in np.unique(bucket):
        vals = jnp.where((bucket == bkt)[None], table[bkt].reshape((HEADS,) + (1,) * bucket.ndim), vals)
    return jnp.where(ok[None], vals, NEG_INF)


def _prompt_bias_tables(rel_bias):
    qi = np.arange(BAND)[:, None]
    kj = np.arange(2 * BAND)[None, :]
    delta = qi + BAND - kj
    ok = (delta >= 0) & (delta <= BAND)
    ok_first = ok & (kj >= BAND)
    tables = []
    for g, (_, dil) in enumerate(ATT_GROUPS):
        bucket = _rel_bucket_np(np.maximum(delta, 0) * dil)
        both = jnp.stack([_bias_lookup(rel_bias, g, bucket, ok), _bias_lookup(rel_bias, g, bucket, ok_first)])
        tables.append(both.reshape(2, HEADS * BAND, 2 * BAND))
    return tables


def _sample_bias_tables(rel_bias, t_new):
    tables = []
    for g, (win, dil) in enumerate(ATT_GROUPS):
        lb = min(win, PAST_LEN)
        t = np.arange(t_new)[:, None]
        col = np.arange(lb + LANES)[None, :]
        delta = lb + t - col
        ok = (col < lb + t_new) & (delta >= 0) & (delta % dil == 0) & (delta // dil <= win // dil)
        bucket = _rel_bucket_np(np.maximum(delta, 0))
        tables.append(_bias_lookup(rel_bias, g, bucket, ok).reshape(HEADS * t_new, lb + LANES))
    return tables


def _stack_heads(q):
    lh = _lane_head()
    return jnp.concatenate([jnp.where(lh == h, q, jnp.zeros_like(q)) for h in range(HEADS)], axis=0)


def _softmax(s):
    m = jnp.max(s, axis=1, keepdims=True)
    p = jnp.exp(s - m)
    l = jnp.sum(p, axis=1, keepdims=True)
    return p.astype(BF16), l, m + jnp.log(l)


def _unstack_heads(o4, lse4, rows):
    lh = _lane_head()
    o = o4[0:rows]
    lse = jnp.broadcast_to(lse4[0:rows], (rows, GROUP_DIM))
    for h in range(1, HEADS):
        o = jnp.where(lh == h, o4[h * rows:(h + 1) * rows], o)
        lse = jnp.where(lh == h, lse4[h * rows:(h + 1) * rows], lse)
    return o, lse


_NT = (((1,), (1,)), ((), ()))


def _attn_prompt_kernel(nsub, q_ref, kc_ref, kp_ref, vc_ref, vp_ref, bias_ref, o_ref, lse_ref, kbuf, vbuf):
    i = pl.program_id(2)
    kbuf[0:BAND] = kp_ref[...]
    kbuf[BAND:] = kc_ref[...]
    vbuf[0:BAND] = vp_ref[...]
    vbuf[BAND:] = vc_ref[...]
    for s in range(nsub):
        q4 = _stack_heads(q_ref[s * BAND:(s + 1) * BAND])
        kc = kbuf[s * BAND:(s + 2) * BAND]
        vc = vbuf[s * BAND:(s + 2) * BAND]
        logits = lax.dot_general(q4, kc, _NT, preferred_element_type=F32)
        if s == 0:
            bias = bias_ref[jnp.where(i == 0, 1, 0)]
        else:
            bias = bias_ref[0]
        p, l, lse4 = _softmax(logits + bias)
        o4 = jnp.dot(p, vc, preferred_element_type=F32) / l
        o, lse = _unstack_heads(o4, lse4, BAND)
        o_ref[s * BAND:(s + 1) * BAND] = o
        lse_ref[s * BAND:(s + 1) * BAND] = lse


def _attn_prompt(cfg, g, qkv, bias):
    b, s_len = cfg["b"], cfg["s"]
    _, dil = ATT_GROUPS[g]
    l_len = s_len // dil
    tq = min(TM, l_len)
    nsub = tq // BAND
    nq = l_len // tq

    def cur(which):
        return pl.BlockSpec((None, None, None, tq, GROUP_DIM), lambda bi, r, i: (which, bi, r, i, 0))

    def prev(which):
        return pl.BlockSpec((None, None, None, BAND, GROUP_DIM),
                            lambda bi, r, i: (which, bi, r, jnp.maximum(i * nsub - 1, 0), 0))

    out = pl.BlockSpec((None, None, tq, GROUP_DIM), lambda bi, r, i: (bi, r, i, 0))
    return pl.pallas_call(
        functools.partial(_attn_prompt_kernel, nsub),
        grid=(b, dil, nq),
        in_specs=[cur(0), cur(1), prev(1), cur(2), prev(2),
                  pl.BlockSpec(bias.shape, lambda bi, r, i: (0, 0, 0))],
        out_specs=[out, out],
        out_shape=[jax.ShapeDtypeStruct((b, dil, l_len, GROUP_DIM), F32)] * 2,
        scratch_shapes=[pltpu.VMEM((tq + BAND, GROUP_DIM), BF16)] * 2,
        compiler_params=_cparams(3, VMEM_LIMIT),
        name=f"attn_prompt_g{g}",
    )(qkv, qkv, qkv, qkv, qkv, bias)


def _attn_sample_kernel(bs, lb, t_new, q_ref, kvn_ref, cache_ref, bias_ref, o_ref, lse_ref):
    pad = jnp.zeros((LANES - t_new, GROUP_DIM), BF16)
    for j in range(bs):
        kt = cache_ref[j, 0:GROUP_DIM, :].astype(BF16)
        vt = cache_ref[j, GROUP_DIM:2 * GROUP_DIM, :].astype(BF16)
        kvn = kvn_ref[j].astype(BF16)
        kn = jnp.concatenate([kvn[:, :GROUP_DIM], pad], axis=0)
        vn = jnp.concatenate([kvn[:, GROUP_DIM:], pad], axis=0)
        q4 = _stack_heads(q_ref[j].astype(BF16))
        logits = jnp.concatenate(
            [jnp.dot(q4, kt, preferred_element_type=F32),
             lax.dot_general(q4, kn, _NT, preferred_element_type=F32)], axis=1)
        p, l, lse4 = _softmax(logits + bias_ref[...])
        o4 = (lax.dot_general(p[:, :lb], vt, _NT, preferred_element_type=F32)
              + jnp.dot(p[:, lb:], vn, preferred_element_type=F32)) / l
        o, lse = _unstack_heads(o4, lse4, t_new)
        o_ref[j] = o
        lse_ref[j] = lse


def _attn_sample(cfg, g, layer, qs, kvs, cache_t, bias):
    bd, t_new = cfg["bd"], cfg["t"]
    lb = cache_t.shape[-1]
    bs = max(1, min(8, (4 * 1024 * 1024) // (2 * GROUP_DIM * lb * 4)))
    out = pl.BlockSpec((bs, t_new, GROUP_DIM), lambda i: (i, 0, 0))
    o, lse = pl.pallas_call(
        functools.partial(_attn_sample_kernel, bs, lb, t_new),
        grid=(bd // bs,),
        in_specs=[pl.BlockSpec((None, bs, t_new, GROUP_DIM), lambda i: (g, i, 0, 0)),
                  pl.BlockSpec((None, bs, t_new, 2 * GROUP_DIM), lambda i: (g, i, 0, 0)),
                  pl.BlockSpec((None, bs, 2 * GROUP_DIM, lb), lambda i: (layer, i, 0, 0)),
                  pl.BlockSpec(bias.shape, lambda i: (0, 0))],
        out_specs=[out, out],
        out_shape=[jax.ShapeDtypeStruct((bd, t_new, GROUP_DIM), F32)] * 2,
        compiler_params=_cparams(1, VMEM_LIMIT),
        name=f"attn_sample_g{g}",
    )(qs.reshape(N_GROUPS, bd, t_new, GROUP_DIM), kvs.reshape(N_GROUPS, bd, t_new, 2 * GROUP_DIM),
      cache_t, bias)
    return o.reshape(bd * t_new, GROUP_DIM), lse.reshape(bd * t_new, GROUP_DIM)


def _pool_mix(ext_ref, rows, pos, w_ref, scale_ref, o_ref):
    lo = POOL_PAD
    u = ext_ref[:, lo:lo + rows, :]
    acc = u
    sums = {}
    for j in range(1, max(POOL_WINDOWS)):
        acc = acc + ext_ref[:, lo - j:lo - j + rows, :]
        if j + 1 in POOL_WINDOWS:
            sums[j + 1] = acc
    lane_grp = lax.broadcasted_iota(jnp.int32, (1, 1, POOL_DIM), 2) // (POOL_DIM // len(POOL_WINDOWS))
    z = None
    for gi, w in enumerate(POOL_WINDOWS):
        cnt = jnp.minimum(w, pos + 1).astype(F32)
        zw = sums[w] / cnt - u
        z = zw if z is None else jnp.where(lane_grp == gi, zw, z)
    nb = u.shape[0]
    y = jnp.dot(z.reshape(nb * rows, POOL_DIM).astype(BF16), w_ref[...], preferred_element_type=F32)
    o_ref[...] = (y * scale_ref[...]).astype(BF16)


def _pool_prompt_kernel(u_ref, halo_ref, w_ref, scale_ref, o_ref, ext_ref):
    i = pl.program_id(1)
    halo = halo_ref[...]
    ext_ref[:, 0:POOL_PAD, :] = jnp.where(i == 0, jnp.zeros_like(halo), halo)
    ext_ref[:, POOL_PAD:, :] = u_ref[...]
    pos = i * TM + lax.broadcasted_iota(jnp.int32, (1, TM, 1), 1)
    _pool_mix(ext_ref, TM, pos, w_ref, scale_ref, o_ref)


def _pool_sample_kernel(t_new, u_ref, hist_ref, w_ref, scale_ref, o_ref, ext_ref):
    nb = u_ref.shape[0]
    ext_ref[:, 0:POOL_PAD - POOL_HIST, :] = jnp.zeros((nb, POOL_PAD - POOL_HIST, POOL_DIM), F32)
    ext_ref[:, POOL_PAD - POOL_HIST:POOL_PAD, :] = hist_ref[...]
    ext_ref[:, POOL_PAD:, :] = u_ref[...]
    pos = PAST_LEN + lax.broadcasted_iota(jnp.int32, (1, t_new, 1), 1)
    _pool_mix(ext_ref, t_new, pos, w_ref, scale_ref, o_ref)


def _pool_prompt(cfg, u_p, w_bd, scale):
    b, s_len = cfg["b"], cfg["s"]
    u3 = u_p.reshape(b, s_len, POOL_DIM)
    per = TM // POOL_PAD
    return pl.pallas_call(
        _pool_prompt_kernel,
        grid=(b, s_len // TM),
        in_specs=[pl.BlockSpec((1, TM, POOL_DIM), lambda bi, i: (bi, i, 0)),
                  pl.BlockSpec((1, POOL_PAD, POOL_DIM), lambda bi, i: (bi, jnp.maximum(i * per - 1, 0), 0)),
                  pl.BlockSpec((POOL_DIM, POOL_DIM), lambda bi, i: (0, 0)),
                  pl.BlockSpec((1, POOL_DIM), lambda bi, i: (0, 0))],
        out_specs=pl.BlockSpec((TM, POOL_DIM), lambda bi, i: (bi * (s_len // TM) + i, 0)),
        out_shape=jax.ShapeDtypeStruct((b * s_len, POOL_DIM), BF16),
        scratch_shapes=[pltpu.VMEM((1, TM + POOL_PAD, POOL_DIM), F32)],
        compiler_params=_cparams(2, VMEM_LIMIT),
        name="pool_prompt",
    )(u3, u3, w_bd, scale)


def _pool_sample(cfg, layer, u_s, state_pool, w_bd, scale):
    bd, t_new = cfg["bd"], cfg["t"]
    u3 = u_s.reshape(bd, t_new, POOL_DIM)
    nb = GP
    return pl.pallas_call(
        functools.partial(_pool_sample_kernel, t_new),
        grid=(bd // nb,),
        in_specs=[pl.BlockSpec((nb, t_new, POOL_DIM), lambda i: (i, 0, 0)),
                  pl.BlockSpec((None, nb, POOL_HIST, POOL_DIM), lambda i: (layer, i, 0, 0)),
                  pl.BlockSpec((POOL_DIM, POOL_DIM), lambda i: (0, 0)),
                  pl.BlockSpec((1, POOL_DIM), lambda i: (0, 0))],
        out_specs=pl.BlockSpec((nb * t_new, POOL_DIM), lambda i: (i, 0)),
        out_shape=jax.ShapeDtypeStruct((bd * t_new, POOL_DIM), BF16),
        scratch_shapes=[pltpu.VMEM((nb, t_new + POOL_PAD, POOL_DIM), F32)],
        compiler_params=_cparams(1, VMEM_LIMIT),
        name="pool_sample",
    )(u3, state_pool, w_bd, scale)


def _outproj_kernel(cfg, x_ref, g1p_ref, g1s_ref, shp_ref, shs_ref, scp_ref, scs_ref, pp_ref, ps_ref, *rest):
    att_refs = rest[:4 * N_GROUPS]
    (gain_ref, wo_ref, wr_ref, br_ref, x1_ref, h2_ref, route_ref, cnt_ref,
     run_ref, nat_ref) = rest[4 * N_GROUPS:]
    i = pl.program_id(0)
    n_pt, d = cfg["n_pt"], cfg["d"]
    is_p = i < n_pt

    halves = GROUP_DIM // LANES

    def put(slot, val, rows=None):
        for half in range(halves):
            piece = val[:, half * LANES:(half + 1) * LANES]
            if rows is None:
                nat_ref[slot, half] = piece
            else:
                nat_ref[slot, half, rows, :] = piece

    @pl.when(is_p)
    def _():
        for g, (_, dil) in enumerate(ATT_GROUPS):
            for which in range(2):
                src = att_refs[4 * g + which]
                if dil == 1:
                    put(2 * g + which, src[0])
                else:
                    for r in range(dil):
                        put(2 * g + which, src[r], pl.ds(r, TM // dil, stride=dil))

    @pl.when(jnp.logical_not(is_p))
    def _():
        for g in range(N_GROUPS):
            for which in range(2):
                put(2 * g + which, att_refs[4 * g + 2 + which][...])

    def get(slot):
        return jnp.concatenate([nat_ref[slot, half] for half in range(halves)], axis=1)

    o = [get(2 * g) for g in range(N_GROUPS)]
    lse = [get(2 * g + 1) for g in range(N_GROUPS)]
    mx = functools.reduce(jnp.maximum, lse)
    e = [jnp.exp(l - mx) for l in lse]
    attn = sum(eg * og for eg, og in zip(e, o)) / sum(e)
    pool = jnp.where(is_p, pp_ref[...], ps_ref[...])
    cat = jnp.concatenate([pool, attn.astype(BF16)], axis=1)
    y = jnp.dot(cat, wo_ref[...], preferred_element_type=F32)

    g1 = jnp.where(is_p, g1p_ref[...], g1s_ref[...])
    x1 = x_ref[...] + g1 * y.reshape(GP, SUBLANES, d)
    x1_ref[...] = x1
    sh = jnp.where(is_p, shp_ref[...], shs_ref[...])
    sc = jnp.where(is_p, scp_ref[...], scs_ref[...])
    h2 = _rmsnorm_mod(x1, gain_ref[...], sc, sh).reshape(TM, d)
    _rows_to_slabs(h2_ref, h2)

    h_hi = h2.astype(BF16)
    h_lo = (h2 - h_hi.astype(F32)).astype(BF16)
    wr = wr_ref[...]
    w_hi = wr.astype(BF16)
    w_lo = (wr - w_hi.astype(F32)).astype(BF16)
    logits = (jnp.dot(h_hi, w_hi, preferred_element_type=F32)
              + jnp.dot(h_lo, w_hi, preferred_element_type=F32)
              + jnp.dot(h_hi, w_lo, preferred_element_type=F32)) + br_ref[...]

    lane = lax.broadcasted_iota(jnp.int32, (TM, LANES), 1).astype(F32)
    vals = logits
    top_v, top_i = [], []
    onehot = jnp.zeros((TM, LANES), F32)
    for _ in range(TOP_K):
        m = jnp.max(vals, axis=1, keepdims=True)
        idx = jnp.min(jnp.where(vals == m, lane, float(LANES)), axis=1, keepdims=True)
        hit = lane == idx
        vals = jnp.where(hit, -jnp.inf, vals)
        onehot = jnp.where(hit, 1.0, onehot)
        top_v.append(m)
        top_i.append(idx)
    ev = [jnp.exp(v - top_v[0]) for v in top_v]
    den = sum(ev)

    @pl.when(i == 0)
    def _():
        run_ref[...] = jnp.zeros_like(run_ref)

    row = lax.broadcasted_iota(jnp.int32, (TM, TM), 0)
    col = lax.broadcasted_iota(jnp.int32, (TM, TM), 1)
    before = (col < row).astype(BF16)
    rank_all = jnp.dot(before, onehot.astype(BF16), preferred_element_type=F32) + run_ref[...]
    route = jnp.zeros((TM, LANES), F32)
    for k in range(TOP_K):
        rank_k = jnp.sum(jnp.where(lane == top_i[k], rank_all, 0.0), axis=1, keepdims=True)
        route = jnp.where(lane == float(k), top_i[k], route)
        route = jnp.where(lane == float(TOP_K + k), ev[k] / den, route)
        route = jnp.where(lane == float(2 * TOP_K + k), rank_k, route)
    route_ref[...] = route
    run_ref[...] = run_ref[...] + jnp.sum(onehot, axis=0, keepdims=True)
    cnt_ref[...] = run_ref[...]


def _outproj(cfg, x3, mp, ms, pool_p, pool_s, att, gain, w_out_b, w_r, b_r):
    d, n_pt, n_t = cfg["d"], cfg["n_pt"], cfg["n_t"]
    n = cfg["n"]
    tps = cfg["tps"]
    pidx = lambda i: (jnp.minimum(i, n_pt - 1), 0)
    sidx = lambda i: (jnp.maximum(i - n_pt, 0), 0)
    pspec = pl.BlockSpec((TM, GROUP_DIM), pidx)
    sspec = pl.BlockSpec((TM, GROUP_DIM), sidx)
    att_specs, att_args = [], []
    for (_, dil), (o_p, lse_p, o_s, lse_s) in zip(ATT_GROUPS, att):
        rspec = pl.BlockSpec((None, dil, TM // dil, GROUP_DIM),
                             lambda i: (pidx(i)[0] // tps, 0, pidx(i)[0] % tps, 0))
        att_specs += [rspec, rspec, sspec, sspec]
        att_args += [o_p, lse_p, o_s, lse_s]
    full = lambda a: pl.BlockSpec(a.shape, lambda i: (0,) * a.ndim)
    return pl.pallas_call(
        functools.partial(_outproj_kernel, cfg),
        grid=(n_t,),
        in_specs=[pl.BlockSpec((GP, SUBLANES, d), lambda i: (i, 0, 0))]
        + _mod_specs(cfg, 2) + _mod_specs(cfg, 3) + _mod_specs(cfg, 4)
        + [pspec, sspec] + att_specs + [full(gain), full(w_out_b), full(w_r), full(b_r)],
        out_specs=[pl.BlockSpec((GP, SUBLANES, d), lambda i: (i, 0, 0)),
                   pl.BlockSpec((TM * SUBLANES, LANES), lambda i: (i, 0)),
                   pl.BlockSpec((TM, LANES), lambda i: (i, 0)),
                   pl.BlockSpec((1, LANES), lambda i: (0, 0))],
        out_shape=[jax.ShapeDtypeStruct(x3.shape, F32),
                   jax.ShapeDtypeStruct((n * SUBLANES, LANES), F32),
                   jax.ShapeDtypeStruct((n, LANES), F32),
                   jax.ShapeDtypeStruct((1, LANES), F32)],
        scratch_shapes=[pltpu.VMEM((1, LANES), F32),
                        pltpu.VMEM((2 * N_GROUPS, GROUP_DIM // LANES, TM, LANES), F32)],
        compiler_params=_cparams(1, VMEM_LIMIT),
        name="outproj_router",
    )(x3, mp, ms, mp, ms, mp, ms, pool_p, pool_s, *att_args, gain, w_out_b, w_r, b_r)


def _rows_from_slabs(ref):
    rows = ref.shape[0] // SUBLANES
    return jnp.concatenate([ref[pl.ds(j, rows, stride=SUBLANES), :] for j in range(SUBLANES)], axis=1)


def _rows_to_slabs(ref, val):
    for j in range(SUBLANES):
        ref[pl.ds(j, val.shape[0], stride=SUBLANES), :] = val[:, j * LANES:(j + 1) * LANES]


def _dispatch_kernel(n_tok, zt_ref, pos_ref, h_ref, x_hbm, dst_ref, zbuf, zsem, sem):
    i = pl.program_id(0)
    rows = h_ref.shape[0] // SUBLANES
    n_asg = n_tok * TOP_K

    @pl.when(i == 0)
    def _():
        zbuf[...] = jnp.zeros_like(zbuf)

        def fill(z):
            return pltpu.make_async_copy(zbuf, x_hbm.at[pl.ds(zt_ref[z] * MOE_TM, MOE_TM)], zsem)

        for z in range(zt_ref.shape[0]):
            pl.when(zt_ref[z] >= 0)(lambda z=z: fill(z).start())

        def pad_rows(t, carry):
            base = n_asg + ((t + 1) % 2) * MOE_TM
            for r in range(MOE_TM):
                dst_ref[t * MOE_TM + r] = base + r
            return carry

        lax.fori_loop(0, dst_ref.shape[0] // MOE_TM, pad_rows, 0)
        for z in range(zt_ref.shape[0]):
            pl.when(zt_ref[z] >= 0)(lambda z=z: fill(z).wait())

    for r in range(rows):
        for k in range(TOP_K):
            slot = pos_ref[0, r * TOP_K + k]
            dst_ref[MOE_TM + slot] = k * n_tok + i * rows + r
            pltpu.make_async_copy(h_ref.at[pl.ds(r * SUBLANES, SUBLANES)], x_hbm.at[slot],
                                  sem.at[k]).start(priority=k % 2)
    for k in range(TOP_K):
        pltpu.make_async_copy(h_ref, h_ref, sem.at[k]).wait()


def _dispatch(cfg, pos, zero_tiles, h2, m_pad):
    n_t = cfg["n_t"]
    grid_spec = pltpu.PrefetchScalarGridSpec(
        num_scalar_prefetch=1,
        grid=(n_t,),
        in_specs=[pl.BlockSpec((None, 1, TM * TOP_K), lambda i, zt: (i, 0, 0), memory_space=pltpu.SMEM),
                  pl.BlockSpec((TM * SUBLANES, LANES), lambda i, zt: (i, 0))],
        out_specs=[pl.BlockSpec(memory_space=pl.ANY), pl.BlockSpec(memory_space=pltpu.SMEM)],
        scratch_shapes=[pltpu.VMEM((MOE_TM, SUBLANES, LANES), F32),
                        pltpu.SemaphoreType.DMA(()),
                        pltpu.SemaphoreType.DMA((TOP_K,))],
    )
    return pl.pallas_call(
        functools.partial(_dispatch_kernel, cfg["n"]),
        grid_spec=grid_spec,
        out_shape=[jax.ShapeDtypeStruct((m_pad, SUBLANES, LANES), F32),
                   jax.ShapeDtypeStruct((m_pad + MOE_TM,), jnp.int32)],
        compiler_params=_cparams(1, VMEM_LIMIT),
        name="moe_dispatch",
    )(zero_tiles, pos.reshape(n_t, 1, TM * TOP_K), h2)


N_YBUF = 3


def _moe_kernel(te_ref, nu_ref, prv_ref, x_ref, w1_ref, b1_ref, w2_ref, b2_ref, y_hbm,
                y0, y1, y2, w1b, w2b, ssem):
    i = pl.program_id(0)
    n_used = nu_ref[0]
    ff = w2_ref.shape[0]
    ys = (y0, y1, y2)

    def start_scatter(idx_ref, q):
        for r in range(MOE_TM):
            pltpu.make_async_copy(ys[q].at[pl.ds(r * SUBLANES, SUBLANES)], y_hbm.at[idx_ref[0, r]],
                                  ssem.at[q]).start(priority=r % 2)

    def wait_scatter(q):
        pltpu.make_async_copy(ys[q], ys[q], ssem.at[q]).wait()

    @pl.when(i == 0)
    def _():
        n_asg = y_hbm.shape[0] - 2 * MOE_TM
        for q in range(N_YBUF):
            ys[q][...] = jnp.zeros_like(ys[q])
        for q in range(2):
            for r in range(MOE_TM):
                pltpu.make_async_copy(ys[q].at[pl.ds(r * SUBLANES, SUBLANES)],
                                      y_hbm.at[n_asg + q * MOE_TM + r], ssem.at[q]).start()
            wait_scatter(q)

    @pl.when(jnp.logical_and(i < n_used,
                             jnp.logical_or(i == 0, te_ref[i] != te_ref[jnp.maximum(i - 1, 0)])))
    def _():
        w1b[...] = w1_ref[...].astype(BF16)
        w2b[...] = w2_ref[...].astype(BF16)

    def step(q):
        q_prev, q_old = (q + 2) % N_YBUF, (q + 1) % N_YBUF
        start_scatter(prv_ref, q_prev)
        x = _rows_from_slabs(x_ref).astype(BF16)
        h1 = jnp.dot(x, w1b[...], preferred_element_type=F32) + b1_ref[...]
        gate = jnp.minimum(h1[:, :ff], SWIGLU_LIMIT)
        up = jnp.clip(h1[:, ff:], -SWIGLU_LIMIT, SWIGLU_LIMIT)
        act = gate * jax.nn.sigmoid(SWIGLU_ALPHA * gate) * (up + 1.0)
        _rows_to_slabs(ys[q], jnp.dot(act.astype(BF16), w2b[...], preferred_element_type=F32) + b2_ref[...])

        @pl.when(i >= 1)
        def _():
            wait_scatter(q_old)

    def drain(q):
        q_prev, q_old = (q + 2) % N_YBUF, (q + 1) % N_YBUF
        start_scatter(prv_ref, q_prev)
        wait_scatter(q_old)
        wait_scatter(q_prev)

    for q in range(N_YBUF):
        pl.when(jnp.logical_and(i % N_YBUF == q, i < n_used))(functools.partial(step, q))
        pl.when(jnp.logical_and(i % N_YBUF == q, i == n_used))(functools.partial(drain, q))


def _moe(layer, n_tiles, tile_expert, n_used, slot_dst, x_sorted, w1, b1, w2, b2, n_rows_out):
    depth, n_exp, d, ff2 = w1.shape
    ff = w2.shape[2]
    smem_tile = lambda f: pl.BlockSpec((None, 1, MOE_TM), f, memory_space=pltpu.SMEM)
    grid_spec = pltpu.PrefetchScalarGridSpec(
        num_scalar_prefetch=2,
        grid=(n_tiles + 1,),
        in_specs=[smem_tile(lambda i, te, nu: (jnp.minimum(i, nu[0]), 0, 0)),
                  pl.BlockSpec((MOE_TM * SUBLANES, LANES), lambda i, te, nu: (jnp.minimum(i, nu[0] - 1), 0)),
                  pl.BlockSpec((None, None, d, ff2), lambda i, te, nu: (layer, te[i], 0, 0)),
                  pl.BlockSpec((None, None, 1, ff2), lambda i, te, nu: (layer, te[i], 0, 0)),
                  pl.BlockSpec((None, None, ff, d), lambda i, te, nu: (layer, te[i], 0, 0)),
                  pl.BlockSpec((None, None, 1, d), lambda i, te, nu: (layer, te[i], 0, 0))],
        out_specs=pl.BlockSpec(memory_space=pl.ANY),
        scratch_shapes=[pltpu.VMEM((MOE_TM * SUBLANES, LANES), F32)] * N_YBUF
        + [pltpu.VMEM((d, ff2), BF16),
           pltpu.VMEM((ff, d), BF16),
           pltpu.SemaphoreType.DMA((N_YBUF,))],
    )
    dst3 = slot_dst.reshape(n_tiles + 1, 1, MOE_TM)
    return pl.pallas_call(
        _moe_kernel,
        grid_spec=grid_spec,
        out_shape=jax.ShapeDtypeStruct((n_rows_out, SUBLANES, LANES), F32),
        compiler_params=_cparams(1, VMEM_LIMIT),
        name="moe_experts",
    )(tile_expert, n_used, dst3, x_sorted.reshape(x_sorted.shape[0] * SUBLANES, LANES),
      w1, b1.reshape(depth, n_exp, 1, ff2), w2, b2.reshape(depth, n_exp, 1, d))


def _moe_plan(cfg, route, counts):
    n = cfg["n"]
    m = n * TOP_K
    n_tiles = m // MOE_TM + N_EXPERTS
    m_pad = n_tiles * MOE_TM
    top_i = route[:, 0:TOP_K].astype(jnp.int32)
    rank = route[:, 2 * TOP_K:3 * TOP_K].astype(jnp.int32)
    cnt = counts[0, :N_EXPERTS].astype(jnp.int32)
    tiles_e = (cnt + MOE_TM - 1) // MOE_TM
    tile_end = jnp.cumsum(tiles_e)
    pstart = (tile_end - tiles_e) * MOE_TM
    expert = jnp.arange(N_EXPERTS, dtype=jnp.int32)
    pos = jnp.sum(jnp.where(top_i[..., None] == expert, pstart, 0), axis=-1) + rank
    t = jnp.arange(n_tiles + 1, dtype=jnp.int32)
    te = jnp.sum((tile_end[None, :] <= t[:, None]).astype(jnp.int32), axis=1)
    last_used = jnp.sum((tile_end <= tile_end[-1] - 1).astype(jnp.int32))
    te = jnp.minimum(te, last_used).astype(jnp.int32)
    n_used = tile_end[-1]
    spare = n_used + jnp.arange(n_tiles - m // MOE_TM, dtype=jnp.int32)
    zero_tiles = jnp.concatenate([jnp.where(tiles_e > 0, tile_end - 1, -1),
                                  jnp.where(spare < n_tiles, spare, -1)]).astype(jnp.int32)
    return n_tiles, te, n_used.reshape(1).astype(jnp.int32), pos, zero_tiles, m + 2 * MOE_TM


def _combine_kernel(cfg, x_ref, g2p_ref, g2s_ref, route_ref, *rest):
    y_refs, o_ref = rest[:TOP_K], rest[TOP_K]
    i = pl.program_id(0)
    is_p = i < cfg["n_pt"]
    route = route_ref[...]
    acc = route[:, TOP_K:TOP_K + 1] * _rows_from_slabs(y_refs[0])
    for k in range(1, TOP_K):
        acc = acc + route[:, TOP_K + k:TOP_K + k + 1] * _rows_from_slabs(y_refs[k])
    g2 = jnp.where(is_p, g2p_ref[...], g2s_ref[...])
    o_ref[...] = x_ref[...] + g2 * acc.reshape(GP, SUBLANES, cfg["d"])


def _combine(cfg, x1, mp, ms, route, ybuf):
    d, n_t = cfg["d"], cfg["n_t"]
    y_specs = [pl.BlockSpec((TM * SUBLANES, LANES), lambda i, k=k: (k * n_t + i, 0)) for k in range(TOP_K)]
    ybuf = ybuf.reshape(ybuf.shape[0] * SUBLANES, LANES)
    return pl.pallas_call(
        functools.partial(_combine_kernel, cfg),
        grid=(n_t,),
        in_specs=[pl.BlockSpec((GP, SUBLANES, d), lambda i: (i, 0, 0))] + _mod_specs(cfg, 5)
        + [pl.BlockSpec((TM, LANES), lambda i: (i, 0))] + y_specs,
        out_specs=pl.BlockSpec((GP, SUBLANES, d), lambda i: (i, 0, 0)),
        out_shape=jax.ShapeDtypeStruct(x1.shape, F32),
        compiler_params=_cparams(1, VMEM_LIMIT),
        name="moe_combine",
    )(x1, mp, ms, route, *([ybuf] * TOP_K))


def kernel(x_prompt, x_sample, cache_kv_w128_d1, cache_kv_w512_d4, cache_kv_w2048_d16, state_pool,
           c_prompt, c_sample, rel_bias, norm_mix, norm_ffn, w_ada, b_ada, w_in, q_norm, k_norm,
           pool_w, pool_scale, w_out, w_router, b_router, w_expert_in, b_expert_in,
           w_expert_out, b_expert_out):
    b, s_len, d = x_prompt.shape
    bd, t_new, _ = x_sample.shape
    depth = w_in.shape[0]
    np_, ns = b * s_len, bd * t_new
    assert t_new == SUBLANES and s_len % TM == 0 and ns % TM == 0
    assert all((s_len // dil) % BAND == 0 for _, dil in ATT_GROUPS)
    cfg = dict(b=b, s=s_len, d=d, bd=bd, t=t_new, np=np_, ns=ns, n=np_ + ns,
               n_pt=np_ // TM, n_t=(np_ + ns) // TM, tps=s_len // TM)
    kv_keep = []
    for win, _ in ATT_GROUPS:
        keep = min(win, s_len)
        assert keep % TM == 0 or TM % keep == 0
        kv_keep.append((keep, max(keep // TM, 1)))
    cfg["kv_keep"] = tuple(kv_keep)
    caches_t = [jnp.transpose(c, (0, 1, 3, 4, 5, 2)).reshape(depth, bd, 2 * GROUP_DIM, c.shape[2])
                for c in (cache_kv_w128_d1, cache_kv_w512_d4, cache_kv_w2048_d16)]

    mods = _ada_mods(jnp.concatenate([c_prompt, c_sample], axis=0), w_ada, b_ada)
    bias_p = _prompt_bias_tables(rel_bias)
    bias_s = _sample_bias_tables(rel_bias, t_new)
    w_in_b = w_in.astype(BF16)
    w_out_b = w_out.astype(BF16)
    eye = jnp.eye(len(POOL_WINDOWS), dtype=F32)
    w_pool_bd = (eye[None, :, None, :, None] * pool_w[:, :, :, None, :]).reshape(depth, POOL_DIM, POOL_DIM)
    w_pool_bd = w_pool_bd.astype(BF16)
    w_r_pad = jnp.pad(w_router, ((0, 0), (0, 0), (0, LANES - N_EXPERTS)))
    b_r_pad = jnp.pad(b_router, ((0, 0), (0, LANES - N_EXPERTS)), constant_values=NEG_INF)

    x3 = jnp.concatenate([x_prompt.reshape(np_, d), x_sample.reshape(ns, d)], axis=0)
    x3 = x3.reshape((np_ + ns) // SUBLANES, SUBLANES, d)

    kv_p = [[] for _ in ATT_GROUPS]
    kv_s = [[] for _ in ATT_GROUPS]
    pool_p_state, pool_s_state = [], []
    for l in range(depth):
        mp = mods[l, :b].reshape(b, 1, 6 * d)
        ms = mods[l, b:].reshape(bd, 1, 6 * d)
        qg = jnp.tile(q_norm[l], HEADS).reshape(1, GROUP_DIM)
        kg = jnp.tile(k_norm[l], HEADS).reshape(1, GROUP_DIM)
        (u_p, u_s, qkv0, qkv1, qkv2, qs, kvs, *kvp) = _inproj(
            cfg, x3, mp, ms, norm_mix[l].reshape(1, d), w_in_b[l], qg, kg)

        att = []
        for g, qkv in enumerate((qkv0, qkv1, qkv2)):
            o_p, lse_p = _attn_prompt(cfg, g, qkv, bias_p[g])
            o_s, lse_s = _attn_sample(cfg, g, l, qs, kvs, caches_t[g], bias_s[g])
            att.append((o_p, lse_p, o_s, lse_s))
            keep = cfg["kv_keep"][g][0]
            kv_p[g].append(kvp[g].reshape(b, keep, 2, HEADS, HEAD_DIM))
            kv_s[g].append(kvs[g].reshape(bd, t_new, 2, HEADS, HEAD_DIM))

        scale = pool_scale[l].reshape(1, POOL_DIM)
        pool_p = _pool_prompt(cfg, u_p, w_pool_bd[l], scale)
        pool_s = _pool_sample(cfg, l, u_s, state_pool, w_pool_bd[l], scale)
        pool_p_state.append(u_p.reshape(b, s_len, POOL_DIM)[:, s_len - POOL_HIST:])
        pool_s_state.append(jnp.concatenate(
            [state_pool[l], u_s.reshape(bd, t_new, POOL_DIM)], axis=1)[:, -POOL_HIST:])

        x1, h2, route, counts = _outproj(cfg, x3, mp, ms, pool_p, pool_s, att,
                                         norm_ffn[l].reshape(1, d), w_out_b[l], w_r_pad[l],
                                         b_r_pad[l].reshape(1, LANES))
        n_tiles, te, n_used, pos, zero_tiles, n_rows_out = _moe_plan(cfg, route, counts)
        x_sorted, slot_dst = _dispatch(cfg, pos, zero_tiles, h2, n_tiles * MOE_TM)
        ybuf = _moe(l, n_tiles, te, n_used, slot_dst, x_sorted,
                    w_expert_in, b_expert_in, w_expert_out, b_expert_out, n_rows_out)
        x3 = _combine(cfg, x1, mp, ms, route, ybuf)

    x_all = x3.reshape(np_ + ns, d)
    y_prompt = x_all[:np_].reshape(b, s_len, d)
    y_sample = x_all[np_:].reshape(bd, t_new, d)
    return (y_prompt, y_sample,
            jnp.stack(kv_p[0]), jnp.stack(kv_p[1]), jnp.stack(kv_p[2]), jnp.stack(pool_p_state),
            jnp.stack(kv_s[0]), jnp.stack(kv_s[1]), jnp.stack(kv_s[2]), jnp.stack(pool_s_state))
```

```python
import functools
import math

import numpy as np
import jax
import jax.numpy as jnp
from jax import lax
from jax.experimental import pallas as pl
from jax.experimental.pallas import tpu as pltpu

F32 = jnp.float32
BF16 = jnp.bfloat16

HEAD_DIM = 64
HEADS = 4
GROUP_DIM = HEADS * HEAD_DIM
ATT_GROUPS = ((128, 1), (512, 4), (2048, 16))
N_GROUPS = len(ATT_GROUPS)
BAND = 128
POOL_WINDOWS = (2, 4, 8, 16)
POOL_HIST = 15
POOL_DIM = 256
POOL_PAD = 16
REL_BUCKETS = 32
REL_MAX_DIST = 2048
N_EXPERTS = 32
TOP_K = 4
SWIGLU_LIMIT = 7.0
SWIGLU_ALPHA = 1.702
EPS = 1e-6
NEG_INF = -1e30
PAST_LEN = 2048

SUBLANES = 8
LANES = 128
TM = 512
GP = TM // SUBLANES
MOE_TM = 512
VMEM_LIMIT = 58 * 1024 * 1024


def _cparams(n_axes, vmem=None):
    return pltpu.CompilerParams(dimension_semantics=("arbitrary",) * n_axes,
                                vmem_limit_bytes=vmem)


def _lane_head(width=GROUP_DIM):
    return lax.broadcasted_iota(jnp.int32, (1, width), 1) // HEAD_DIM


def _ada_kernel(c_ref, w_ref, b_ref, o_ref):
    c = c_ref[...]
    a = (c * jax.nn.sigmoid(c)).astype(BF16)
    o_ref[...] = jnp.dot(a, w_ref[...].astype(BF16), preferred_element_type=F32) + b_ref[...]


def _ada_mods(c_all, w_ada, b_ada):
    depth, d, d6 = w_ada.shape
    bc = c_all.shape[0]
    tn = d6 // 4
    return pl.pallas_call(
        _ada_kernel,
        grid=(depth, d6 // tn),
        in_specs=[pl.BlockSpec((bc, d), lambda l, j: (0, 0)),
                  pl.BlockSpec((None, d, tn), lambda l, j: (l, 0, j)),
                  pl.BlockSpec((None, 1, tn), lambda l, j: (l, 0, j))],
        out_specs=pl.BlockSpec((None, bc, tn), lambda l, j: (l, 0, j)),
        out_shape=jax.ShapeDtypeStruct((depth, bc, d6), F32),
        compiler_params=_cparams(2, VMEM_LIMIT),
        name="ada_mods",
    )(c_all, w_ada, b_ada.reshape(depth, 1, d6))


def _mod_specs(cfg, col):
    d = cfg["d"]
    n_pt, tps, b = cfg["n_pt"], cfg["tps"], cfg["b"]
    return [pl.BlockSpec((1, 1, d), lambda i: (jnp.minimum(i // tps, b - 1), 0, col)),
            pl.BlockSpec((GP, 1, d), lambda i: (jnp.maximum(i - n_pt, 0), 0, col))]


def _rmsnorm_mod(x, g, sc, sh):
    inv = lax.rsqrt(jnp.mean(x * x, axis=-1, keepdims=True) + EPS)
    return x * inv * g * (1.0 + sc) + sh


def _inproj_kernel(cfg, x_ref, shp_ref, shs_ref, scp_ref, scs_ref, g_ref, w_ref, qg_ref, kg_ref,
                   up_ref, us_ref, qkv0_ref, qkv1_ref, qkv2_ref, qs_ref, kvs_ref,
                   kvp0_ref, kvp1_ref, kvp2_ref, res_ref):
    qkv_refs = (qkv0_ref, qkv1_ref, qkv2_ref)
    kvp_refs = (kvp0_ref, kvp1_ref, kvp2_ref)
    i = pl.program_id(0)
    n_pt, tps, d = cfg["n_pt"], cfg["tps"], cfg["d"]
    is_p = i < n_pt
    sh = jnp.where(is_p, shp_ref[...], shs_ref[...])
    sc = jnp.where(is_p, scp_ref[...], scs_ref[...])
    h = _rmsnorm_mod(x_ref[...], g_ref[...], sc, sh)
    z = jnp.dot(h.reshape(TM, d).astype(BF16), w_ref[...], preferred_element_type=F32)

    r = lax.broadcasted_iota(jnp.int32, (GROUP_DIM, GROUP_DIM), 0) // HEAD_DIM
    c = lax.broadcasted_iota(jnp.int32, (GROUP_DIM, GROUP_DIM), 1) // HEAD_DIM
    head_ones = (r == c).astype(BF16)

    def head_norm(t, gain):
        ss = jnp.dot((t * t).astype(BF16), head_ones, preferred_element_type=F32) * (1.0 / HEAD_DIM)
        return t * lax.rsqrt(ss + EPS) * gain

    u = z[:, :POOL_DIM]
    att = d - POOL_DIM
    q, k, v = [], [], []
    for g in range(N_GROUPS):
        lo = POOL_DIM + g * GROUP_DIM
        q.append(head_norm(z[:, lo:lo + GROUP_DIM], qg_ref[...]) * (HEAD_DIM ** -0.5))
        k.append(head_norm(z[:, lo + att:lo + att + GROUP_DIM], kg_ref[...]))
        v.append(z[:, lo + 2 * att:lo + 2 * att + GROUP_DIM])

    @pl.when(is_p)
    def _():
        up_ref[...] = u
        for g, (_, dil) in enumerate(ATT_GROUPS):
            for which, val in enumerate((q[g], k[g], v[g])):
                if dil == 1:
                    qkv_refs[g][which, 0] = val.astype(BF16)
                else:
                    for half in range(GROUP_DIM // LANES):
                        res_ref[half] = val[:, half * LANES:(half + 1) * LANES]
                    for r in range(dil):
                        rows = [res_ref[half, pl.ds(r, TM // dil, stride=dil), :]
                                for half in range(GROUP_DIM // LANES)]
                        qkv_refs[g][which, r] = jnp.concatenate(rows, axis=1).astype(BF16)

    @pl.when(jnp.logical_not(is_p))
    def _():
        us_ref[...] = u
        for g in range(N_GROUPS):
            qs_ref[g] = q[g]
            kvs_ref[g, :, :GROUP_DIM] = k[g]
            kvs_ref[g, :, GROUP_DIM:] = v[g]

    j = jnp.minimum(i, n_pt - 1) % tps
    for g, (keep, nk) in enumerate(cfg["kv_keep"]):
        rows = min(keep, TM)

        @pl.when(jnp.logical_and(is_p, j >= tps - nk))
        def _(g=g, rows=rows):
            kvp_refs[g][:, :GROUP_DIM] = k[g][TM - rows:]
            kvp_refs[g][:, GROUP_DIM:] = v[g][TM - rows:]


def _inproj(cfg, x3, mp, ms, gain, w_in_b, qg, kg):
    d, n_pt, n_t, tps, b = cfg["d"], cfg["n_pt"], cfg["n_t"], cfg["tps"], cfg["b"]
    np_, ns = cfg["np"], cfg["ns"]
    pidx = lambda i: jnp.minimum(i, n_pt - 1)
    sidx = lambda i: jnp.maximum(i - n_pt, 0)

    kv_specs, kv_shapes = [], []
    for keep, nk in cfg["kv_keep"]:
        rows = min(keep, TM)

        def kv_idx(i, nk=nk):
            ip = pidx(i)
            return ((ip // tps) * nk + jnp.maximum(ip % tps - (tps - nk), 0), 0)

        kv_specs.append(pl.BlockSpec((rows, 2 * GROUP_DIM), kv_idx))
        kv_shapes.append(jax.ShapeDtypeStruct((b * keep, 2 * GROUP_DIM), F32))

    qkv_specs, qkv_shapes = [], []
    for _, dil in ATT_GROUPS:
        qkv_specs.append(pl.BlockSpec((3, None, dil, TM // dil, GROUP_DIM),
                                      lambda i: (0, pidx(i) // tps, 0, pidx(i) % tps, 0)))
        qkv_shapes.append(jax.ShapeDtypeStruct((3, b, dil, cfg["s"] // dil, GROUP_DIM), BF16))

    return pl.pallas_call(
        functools.partial(_inproj_kernel, cfg),
        grid=(n_t,),
        in_specs=[pl.BlockSpec((GP, SUBLANES, d), lambda i: (i, 0, 0))]
        + _mod_specs(cfg, 0) + _mod_specs(cfg, 1)
        + [pl.BlockSpec((1, d), lambda i: (0, 0)),
           pl.BlockSpec(w_in_b.shape, lambda i: (0, 0)),
           pl.BlockSpec((1, GROUP_DIM), lambda i: (0, 0)),
           pl.BlockSpec((1, GROUP_DIM), lambda i: (0, 0))],
        out_specs=[pl.BlockSpec((TM, POOL_DIM), lambda i: (pidx(i), 0)),
                   pl.BlockSpec((TM, POOL_DIM), lambda i: (sidx(i), 0))] + qkv_specs
        + [pl.BlockSpec((N_GROUPS, TM, GROUP_DIM), lambda i: (0, sidx(i), 0)),
           pl.BlockSpec((N_GROUPS, TM, 2 * GROUP_DIM), lambda i: (0, sidx(i), 0))] + kv_specs,
        out_shape=[jax.ShapeDtypeStruct((np_, POOL_DIM), F32),
                   jax.ShapeDtypeStruct((ns, POOL_DIM), F32)] + qkv_shapes
        + [jax.ShapeDtypeStruct((N_GROUPS, ns, GROUP_DIM), F32),
           jax.ShapeDtypeStruct((N_GROUPS, ns, 2 * GROUP_DIM), F32)] + kv_shapes,
        scratch_shapes=[pltpu.VMEM((GROUP_DIM // LANES, TM, LANES), F32)],
        compiler_params=_cparams(1, VMEM_LIMIT),
        name="inproj",
    )(x3, mp, ms, mp, ms, gain, w_in_b, qg, kg)


def _rel_bucket_np(dist):
    dist = np.asarray(dist, np.int32)
    max_exact = REL_BUCKETS // 2
    d_f = np.maximum(dist, 1).astype(np.float32)
    ratio = np.log(d_f / np.float32(max_exact)) / np.float32(math.log(REL_MAX_DIST / max_exact))
    large = max_exact + (ratio * np.float32(REL_BUCKETS - max_exact)).astype(np.int32)
    large = np.minimum(large, REL_BUCKETS - 1)
    return np.where(dist < max_exact, dist, large)


def _bias_lookup(rel_bias, g, bucket, ok):
    table = rel_bias[:, g * HEADS:(g + 1) * HEADS].astype(F32)
    vals = jnp.zeros((HEADS,) + bucket.shape, F32)
    for bkt in np.unique(bucket):
        vals = jnp.where((bucket == bkt)[None], table[bkt].reshape((HEADS,) + (1,) * bucket.ndim), vals)
    return jnp.where(ok[None], vals, NEG_INF)


def _prompt_bias_tables(rel_bias):
    qi = np.arange(BAND)[:, None]
    kj = np.arange(2 * BAND)[None, :]
    delta = qi + BAND - kj
    ok = (delta >= 0) & (delta <= BAND)
    ok_first = ok & (kj >= BAND)
    tables = []
    for g, (_, dil) in enumerate(ATT_GROUPS):
        bucket = _rel_bucket_np(np.maximum(delta, 0) * dil)
        both = jnp.stack([_bias_lookup(rel_bias, g, bucket, ok), _bias_lookup(rel_bias, g, bucket, ok_first)])
        tables.append(both.reshape(2, HEADS * BAND, 2 * BAND))
    return tables


def _sample_bias_tables(rel_bias, t_new):
    tables = []
    for g, (win, dil) in enumerate(ATT_GROUPS):
        lb = min(win, PAST_LEN)
        t = np.arange(t_new)[:, None]
        col = np.arange(lb + LANES)[None, :]
        delta = lb + t - col
        ok = (col < lb + t_new) & (delta >= 0) & (delta % dil == 0) & (delta // dil <= win // dil)
        bucket = _rel_bucket_np(np.maximum(delta, 0))
        tables.append(_bias_lookup(rel_bias, g, bucket, ok).reshape(HEADS * t_new, lb + LANES))
    return tables


def _stack_heads(q):
    lh = _lane_head()
    return jnp.concatenate([jnp.where(lh == h, q, jnp.zeros_like(q)) for h in range(HEADS)], axis=0)


def _softmax(s):
    m = jnp.max(s, axis=1, keepdims=True)
    p = jnp.exp(s - m)
    l = jnp.sum(p, axis=1, keepdims=True)
    return p.astype(BF16), l, m + jnp.log(l)


def _unstack_heads(o4, lse4, rows):
    lh = _lane_head()
    o = o4[0:rows]
    lse = jnp.broadcast_to(lse4[0:rows], (rows, GROUP_DIM))
    for h in range(1, HEADS):
        o = jnp.where(lh == h, o4[h * rows:(h + 1) * rows], o)
        lse = jnp.where(lh == h, lse4[h * rows:(h + 1) * rows], lse)
    return o, lse


_NT = (((1,), (1,)), ((), ()))


def _attn_prompt_kernel(nsub, q_ref, kc_ref, kp_ref, vc_ref, vp_ref, bias_ref, o_ref, lse_ref, kbuf, vbuf):
    i = pl.program_id(2)
    kbuf[0:BAND] = kp_ref[...]
    kbuf[BAND:] = kc_ref[...]
    vbuf[0:BAND] = vp_ref[...]
    vbuf[BAND:] = vc_ref[...]
    for s in range(nsub):
        q4 = _stack_heads(q_ref[s * BAND:(s + 1) * BAND])
        kc = kbuf[s * BAND:(s + 2) * BAND]
        vc = vbuf[s * BAND:(s + 2) * BAND]
        logits = lax.dot_general(q4, kc, _NT, preferred_element_type=F32)
        if s == 0:
            bias = bias_ref[jnp.where(i == 0, 1, 0)]
        else:
            bias = bias_ref[0]
        p, l, lse4 = _softmax(logits + bias)
        o4 = jnp.dot(p, vc, preferred_element_type=F32) / l
        o, lse = _unstack_heads(o4, lse4, BAND)
        o_ref[s * BAND:(s + 1) * BAND] = o
        lse_ref[s * BAND:(s + 1) * BAND] = lse


def _attn_prompt(cfg, g, qkv, bias):
    b, s_len = cfg["b"], cfg["s"]
    _, dil = ATT_GROUPS[g]
    l_len = s_len // dil
    tq = min(TM, l_len)
    nsub = tq // BAND
    nq = l_len // tq

    def cur(which):
        return pl.BlockSpec((None, None, None, tq, GROUP_DIM), lambda bi, r, i: (which, bi, r, i, 0))

    def prev(which):
        return pl.BlockSpec((None, None, None, BAND, GROUP_DIM),
                            lambda bi, r, i: (which, bi, r, jnp.maximum(i * nsub - 1, 0), 0))

    out = pl.BlockSpec((None, None, tq, GROUP_DIM), lambda bi, r, i: (bi, r, i, 0))
    return pl.pallas_call(
        functools.partial(_attn_prompt_kernel, nsub),
        grid=(b, dil, nq),
        in_specs=[cur(0), cur(1), prev(1), cur(2), prev(2),
                  pl.BlockSpec(bias.shape, lambda bi, r, i: (0, 0, 0))],
        out_specs=[out, out],
        out_shape=[jax.ShapeDtypeStruct((b, dil, l_len, GROUP_DIM), F32)] * 2,
        scratch_shapes=[pltpu.VMEM((tq + BAND, GROUP_DIM), BF16)] * 2,
        compiler_params=_cparams(3, VMEM_LIMIT),
        name=f"attn_prompt_g{g}",
    )(qkv, qkv, qkv, qkv, qkv, bias)


def _attn_sample_kernel(bs, lb, t_new, q_ref, kvn_ref, cache_ref, bias_ref, o_ref, lse_ref):
    pad = jnp.zeros((LANES - t_new, GROUP_DIM), BF16)
    for j in range(bs):
        kt = cache_ref[j, 0:GROUP_DIM, :].astype(BF16)
        vt = cache_ref[j, GROUP_DIM:2 * GROUP_DIM, :].astype(BF16)
        kvn = kvn_ref[j].astype(BF16)
        kn = jnp.concatenate([kvn[:, :GROUP_DIM], pad], axis=0)
        vn = jnp.concatenate([kvn[:, GROUP_DIM:], pad], axis=0)
        q4 = _stack_heads(q_ref[j].astype(BF16))
        logits = jnp.concatenate(
            [jnp.dot(q4, kt, preferred_element_type=F32),
             lax.dot_general(q4, kn, _NT, preferred_element_type=F32)], axis=1)
        p, l, lse4 = _softmax(logits + bias_ref[...])
        o4 = (lax.dot_general(p[:, :lb], vt, _NT, preferred_element_type=F32)
              + jnp.dot(p[:, lb:], vn, preferred_element_type=F32)) / l
        o, lse = _unstack_heads(o4, lse4, t_new)
        o_ref[j] = o
        lse_ref[j] = lse


def _attn_sample(cfg, g, layer, qs, kvs, cache_t, bias):
    bd, t_new = cfg["bd"], cfg["t"]
    lb = cache_t.shape[-1]
    bs = max(1, min(8, (8 * 1024 * 1024) // (2 * GROUP_DIM * lb * 4)))
    out = pl.BlockSpec((bs, t_new, GROUP_DIM), lambda i: (i, 0, 0))
    o, lse = pl.pallas_call(
        functools.partial(_attn_sample_kernel, bs, lb, t_new),
        grid=(bd // bs,),
        in_specs=[pl.BlockSpec((None, bs, t_new, GROUP_DIM), lambda i: (g, i, 0, 0)),
                  pl.BlockSpec((None, bs, t_new, 2 * GROUP_DIM), lambda i: (g, i, 0, 0)),
                  pl.BlockSpec((None, bs, 2 * GROUP_DIM, lb), lambda i: (layer, i, 0, 0)),
                  pl.BlockSpec(bias.shape, lambda i: (0, 0))],
        out_specs=[out, out],
        out_shape=[jax.ShapeDtypeStruct((bd, t_new, GROUP_DIM), F32)] * 2,
        compiler_params=_cparams(1, VMEM_LIMIT),
        name=f"attn_sample_g{g}",
    )(qs.reshape(N_GROUPS, bd, t_new, GROUP_DIM), kvs.reshape(N_GROUPS, bd, t_new, 2 * GROUP_DIM),
      cache_t, bias)
    return o.reshape(bd * t_new, GROUP_DIM), lse.reshape(bd * t_new, GROUP_DIM)


def _pool_mix(ext_ref, rows, pos, w_ref, scale_ref, o_ref):
    lo = POOL_PAD
    u = ext_ref[:, lo:lo + rows, :]
    acc = u
    sums = {}
    for j in range(1, max(POOL_WINDOWS)):
        acc = acc + ext_ref[:, lo - j:lo - j + rows, :]
        if j + 1 in POOL_WINDOWS:
            sums[j + 1] = acc
    lane_grp = lax.broadcasted_iota(jnp.int32, (1, 1, POOL_DIM), 2) // (POOL_DIM // len(POOL_WINDOWS))
    z = None
    for gi, w in enumerate(POOL_WINDOWS):
        cnt = jnp.minimum(w, pos + 1).astype(F32)
        zw = sums[w] / cnt - u
        z = zw if z is None else jnp.where(lane_grp == gi, zw, z)
    nb = u.shape[0]
    y = jnp.dot(z.reshape(nb * rows, POOL_DIM).astype(BF16), w_ref[...], preferred_element_type=F32)
    o_ref[...] = (y * scale_ref[...]).astype(BF16)


def _pool_prompt_kernel(u_ref, halo_ref, w_ref, scale_ref, o_ref, ext_ref):
    i = pl.program_id(1)
    halo = halo_ref[...]
    ext_ref[:, 0:POOL_PAD, :] = jnp.where(i == 0, jnp.zeros_like(halo), halo)
    ext_ref[:, POOL_PAD:, :] = u_ref[...]
    pos = i * TM + lax.broadcasted_iota(jnp.int32, (1, TM, 1), 1)
    _pool_mix(ext_ref, TM, pos, w_ref, scale_ref, o_ref)


def _pool_sample_kernel(t_new, u_ref, hist_ref, w_ref, scale_ref, o_ref, ext_ref):
    nb = u_ref.shape[0]
    ext_ref[:, 0:POOL_PAD - POOL_HIST, :] = jnp.zeros((nb, POOL_PAD - POOL_HIST, POOL_DIM), F32)
    ext_ref[:, POOL_PAD - POOL_HIST:POOL_PAD, :] = hist_ref[...]
    ext_ref[:, POOL_PAD:, :] = u_ref[...]
    pos = PAST_LEN + lax.broadcasted_iota(jnp.int32, (1, t_new, 1), 1)
    _pool_mix(ext_ref, t_new, pos, w_ref, scale_ref, o_ref)


def _pool_prompt(cfg, u_p, w_bd, scale):
    b, s_len = cfg["b"], cfg["s"]
    u3 = u_p.reshape(b, s_len, POOL_DIM)
    per = TM // POOL_PAD
    return pl.pallas_call(
        _pool_prompt_kernel,
        grid=(b, s_len // TM),
        in_specs=[pl.BlockSpec((1, TM, POOL_DIM), lambda bi, i: (bi, i, 0)),
                  pl.BlockSpec((1, POOL_PAD, POOL_DIM), lambda bi, i: (bi, jnp.maximum(i * per - 1, 0), 0)),
                  pl.BlockSpec((POOL_DIM, POOL_DIM), lambda bi, i: (0, 0)),
                  pl.BlockSpec((1, POOL_DIM), lambda bi, i: (0, 0))],
        out_specs=pl.BlockSpec((TM, POOL_DIM), lambda bi, i: (bi * (s_len // TM) + i, 0)),
        out_shape=jax.ShapeDtypeStruct((b * s_len, POOL_DIM), BF16),
        scratch_shapes=[pltpu.VMEM((1, TM + POOL_PAD, POOL_DIM), F32)],
        compiler_params=_cparams(2, VMEM_LIMIT),
        name="pool_prompt",
    )(u3, u3, w_bd, scale)


def _pool_sample(cfg, layer, u_s, state_pool, w_bd, scale):
    bd, t_new = cfg["bd"], cfg["t"]
    u3 = u_s.reshape(bd, t_new, POOL_DIM)
    nb = GP
    return pl.pallas_call(
        functools.partial(_pool_sample_kernel, t_new),
        grid=(bd // nb,),
        in_specs=[pl.BlockSpec((nb, t_new, POOL_DIM), lambda i: (i, 0, 0)),
                  pl.BlockSpec((None, nb, POOL_HIST, POOL_DIM), lambda i: (layer, i, 0, 0)),
                  pl.BlockSpec((POOL_DIM, POOL_DIM), lambda i: (0, 0)),
                  pl.BlockSpec((1, POOL_DIM), lambda i: (0, 0))],
        out_specs=pl.BlockSpec((nb * t_new, POOL_DIM), lambda i: (i, 0)),
        out_shape=jax.ShapeDtypeStruct((bd * t_new, POOL_DIM), BF16),
        scratch_shapes=[pltpu.VMEM((nb, t_new + POOL_PAD, POOL_DIM), F32)],
        compiler_params=_cparams(1, VMEM_LIMIT),
        name="pool_sample",
    )(u3, state_pool, w_bd, scale)


def _outproj_kernel(cfg, x_ref, g1p_ref, g1s_ref, shp_ref, shs_ref, scp_ref, scs_ref, pp_ref, ps_ref, *rest):
    att_refs = rest[:4 * N_GROUPS]
    (gain_ref, wo_ref, wr_ref, br_ref, x1_ref, h2_ref, route_ref, cnt_ref,
     run_ref, nat_ref) = rest[4 * N_GROUPS:]
    i = pl.program_id(0)
    n_pt, d = cfg["n_pt"], cfg["d"]
    is_p = i < n_pt

    halves = GROUP_DIM // LANES

    def put(slot, val, rows=None):
        for half in range(halves):
            piece = val[:, half * LANES:(half + 1) * LANES]
            if rows is None:
                nat_ref[slot, half] = piece
            else:
                nat_ref[slot, half, rows, :] = piece

    @pl.when(is_p)
    def _():
        for g, (_, dil) in enumerate(ATT_GROUPS):
            for which in range(2):
                src = att_refs[4 * g + which]
                if dil == 1:
                    put(2 * g + which, src[0])
                else:
                    for r in range(dil):
                        put(2 * g + which, src[r], pl.ds(r, TM // dil, stride=dil))

    @pl.when(jnp.logical_not(is_p))
    def _():
        for g in range(N_GROUPS):
            for which in range(2):
                put(2 * g + which, att_refs[4 * g + 2 + which][...])

    def get(slot):
        return jnp.concatenate([nat_ref[slot, half] for half in range(halves)], axis=1)

    o = [get(2 * g) for g in range(N_GROUPS)]
    lse = [get(2 * g + 1) for g in range(N_GROUPS)]
    mx = functools.reduce(jnp.maximum, lse)
    e = [jnp.exp(l - mx) for l in lse]
    attn = sum(eg * og for eg, og in zip(e, o)) / sum(e)
    pool = jnp.where(is_p, pp_ref[...], ps_ref[...])
    cat = jnp.concatenate([pool, attn.astype(BF16)], axis=1)
    y = jnp.dot(cat, wo_ref[...], preferred_element_type=F32)

    g1 = jnp.where(is_p, g1p_ref[...], g1s_ref[...])
    x1 = x_ref[...] + g1 * y.reshape(GP, SUBLANES, d)
    x1_ref[...] = x1
    sh = jnp.where(is_p, shp_ref[...], shs_ref[...])
    sc = jnp.where(is_p, scp_ref[...], scs_ref[...])
    h2 = _rmsnorm_mod(x1, gain_ref[...], sc, sh).reshape(TM, d)
    _rows_to_slabs(h2_ref, h2)

    h_hi = h2.astype(BF16)
    h_lo = (h2 - h_hi.astype(F32)).astype(BF16)
    wr = wr_ref[...]
    w_hi = wr.astype(BF16)
    w_lo = (wr - w_hi.astype(F32)).astype(BF16)
    logits = (jnp.dot(h_hi, w_hi, preferred_element_type=F32)
              + jnp.dot(h_lo, w_hi, preferred_element_type=F32)
              + jnp.dot(h_hi, w_lo, preferred_element_type=F32)) + br_ref[...]

    lane = lax.broadcasted_iota(jnp.int32, (TM, LANES), 1).astype(F32)
    vals = logits
    top_v, top_i = [], []
    onehot = jnp.zeros((TM, LANES), F32)
    for _ in range(TOP_K):
        m = jnp.max(vals, axis=1, keepdims=True)
        idx = jnp.min(jnp.where(vals == m, lane, float(LANES)), axis=1, keepdims=True)
        hit = lane == idx
        vals = jnp.where(hit, -jnp.inf, vals)
        onehot = jnp.where(hit, 1.0, onehot)
        top_v.append(m)
        top_i.append(idx)
    ev = [jnp.exp(v - top_v[0]) for v in top_v]
    den = sum(ev)

    @pl.when(i == 0)
    def _():
        run_ref[...] = jnp.zeros_like(run_ref)

    row = lax.broadcasted_iota(jnp.int32, (TM, TM), 0)
    col = lax.broadcasted_iota(jnp.int32, (TM, TM), 1)
    before = (col < row).astype(BF16)
    rank_all = jnp.dot(before, onehot.astype(BF16), preferred_element_type=F32) + run_ref[...]
    route = jnp.zeros((TM, LANES), F32)
    for k in range(TOP_K):
        rank_k = jnp.sum(jnp.where(lane == top_i[k], rank_all, 0.0), axis=1, keepdims=True)
        route = jnp.where(lane == float(k), top_i[k], route)
        route = jnp.where(lane == float(TOP_K + k), ev[k] / den, route)
        route = jnp.where(lane == float(2 * TOP_K + k), rank_k, route)
    route_ref[...] = route
    run_ref[...] = run_ref[...] + jnp.sum(onehot, axis=0, keepdims=True)
    cnt_ref[...] = run_ref[...]


def _outproj(cfg, x3, mp, ms, pool_p, pool_s, att, gain, w_out_b, w_r, b_r):
    d, n_pt, n_t = cfg["d"], cfg["n_pt"], cfg["n_t"]
    n = cfg["n"]
    tps = cfg["tps"]
    pidx = lambda i: (jnp.minimum(i, n_pt - 1), 0)
    sidx = lambda i: (jnp.maximum(i - n_pt, 0), 0)
    pspec = pl.BlockSpec((TM, GROUP_DIM), pidx)
    sspec = pl.BlockSpec((TM, GROUP_DIM), sidx)
    att_specs, att_args = [], []
    for (_, dil), (o_p, lse_p, o_s, lse_s) in zip(ATT_GROUPS, att):
        rspec = pl.BlockSpec((None, dil, TM // dil, GROUP_DIM),
                             lambda i: (pidx(i)[0] // tps, 0, pidx(i)[0] % tps, 0))
        att_specs += [rspec, rspec, sspec, sspec]
        att_args += [o_p, lse_p, o_s, lse_s]
    full = lambda a: pl.BlockSpec(a.shape, lambda i: (0,) * a.ndim)
    return pl.pallas_call(
        functools.partial(_outproj_kernel, cfg),
        grid=(n_t,),
        in_specs=[pl.BlockSpec((GP, SUBLANES, d), lambda i: (i, 0, 0))]
        + _mod_specs(cfg, 2) + _mod_specs(cfg, 3) + _mod_specs(cfg, 4)
        + [pspec, sspec] + att_specs + [full(gain), full(w_out_b), full(w_r), full(b_r)],
        out_specs=[pl.BlockSpec((GP, SUBLANES, d), lambda i: (i, 0, 0)),
                   pl.BlockSpec((TM * SUBLANES, LANES), lambda i: (i, 0)),
                   pl.BlockSpec((TM, LANES), lambda i: (i, 0)),
                   pl.BlockSpec((1, LANES), lambda i: (0, 0))],
        out_shape=[jax.ShapeDtypeStruct(x3.shape, F32),
                   jax.ShapeDtypeStruct((n * SUBLANES, LANES), F32),
                   jax.ShapeDtypeStruct((n, LANES), F32),
                   jax.ShapeDtypeStruct((1, LANES), F32)],
        scratch_shapes=[pltpu.VMEM((1, LANES), F32),
                        pltpu.VMEM((2 * N_GROUPS, GROUP_DIM // LANES, TM, LANES), F32)],
        compiler_params=_cparams(1, VMEM_LIMIT),
        name="outproj_router",
    )(x3, mp, ms, mp, ms, mp, ms, pool_p, pool_s, *att_args, gain, w_out_b, w_r, b_r)


def _rows_from_slabs(ref):
    rows = ref.shape[0] // SUBLANES
    return jnp.concatenate([ref[pl.ds(j, rows, stride=SUBLANES), :] for j in range(SUBLANES)], axis=1)


def _rows_to_slabs(ref, val):
    for j in range(SUBLANES):
        ref[pl.ds(j, val.shape[0], stride=SUBLANES), :] = val[:, j * LANES:(j + 1) * LANES]


def _dispatch_kernel(n_tok, zt_ref, pos_ref, h_ref, x_hbm, dst_ref, zbuf, zsem, sem):
    i = pl.program_id(0)
    rows = h_ref.shape[0] // SUBLANES
    n_asg = n_tok * TOP_K

    @pl.when(i == 0)
    def _():
        zbuf[...] = jnp.zeros_like(zbuf)

        def fill(z):
            return pltpu.make_async_copy(zbuf, x_hbm.at[pl.ds(zt_ref[z] * MOE_TM, MOE_TM)], zsem)

        for z in range(zt_ref.shape[0]):
            pl.when(zt_ref[z] >= 0)(lambda z=z: fill(z).start())

        def pad_rows(t):
            base = n_asg + ((t + 1) % 2) * MOE_TM
            for r in range(MOE_TM):
                dst_ref[t * MOE_TM + r] = base + r

        pad_rows(0)

        def pad_listed(z, carry):
            pl.when(zt_ref[z] >= 0)(lambda: pad_rows(zt_ref[z] + 1))
            return carry

        lax.fori_loop(0, zt_ref.shape[0], pad_listed, 0)
        for z in range(zt_ref.shape[0]):
            pl.when(zt_ref[z] >= 0)(lambda z=z: fill(z).wait())

    for r in range(rows):
        for k in range(TOP_K):
            slot = pos_ref[0, r * TOP_K + k]
            dst_ref[MOE_TM + slot] = k * n_tok + i * rows + r
            pltpu.make_async_copy(h_ref.at[pl.ds(r * SUBLANES, SUBLANES)], x_hbm.at[slot],
                                  sem.at[k]).start(priority=k % 2)
    for k in range(TOP_K):
        pltpu.make_async_copy(h_ref, h_ref, sem.at[k]).wait()


def _dispatch(cfg, pos, zero_tiles, h2, m_pad):
    n_t = cfg["n_t"]
    grid_spec = pltpu.PrefetchScalarGridSpec(
        num_scalar_prefetch=1,
        grid=(n_t,),
        in_specs=[pl.BlockSpec((None, 1, TM * TOP_K), lambda i, zt: (i, 0, 0), memory_space=pltpu.SMEM),
                  pl.BlockSpec((TM * SUBLANES, LANES), lambda i, zt: (i, 0))],
        out_specs=[pl.BlockSpec(memory_space=pl.ANY), pl.BlockSpec(memory_space=pltpu.SMEM)],
        scratch_shapes=[pltpu.VMEM((MOE_TM, SUBLANES, LANES), F32),
                        pltpu.SemaphoreType.DMA(()),
                        pltpu.SemaphoreType.DMA((TOP_K,))],
    )
    return pl.pallas_call(
        functools.partial(_dispatch_kernel, cfg["n"]),
        grid_spec=grid_spec,
        out_shape=[jax.ShapeDtypeStruct((m_pad, SUBLANES, LANES), F32),
                   jax.ShapeDtypeStruct((m_pad + MOE_TM,), jnp.int32)],
        compiler_params=_cparams(1, VMEM_LIMIT),
        name="moe_dispatch",
    )(zero_tiles, pos.reshape(n_t, 1, TM * TOP_K), h2)


N_YBUF = 3


def _moe_kernel(te_ref, nu_ref, prv_ref, x_ref, w1_ref, b1_ref, w2_ref, b2_ref, y_hbm,
                y0, y1, y2, w1b, w2b, ssem):
    i = pl.program_id(0)
    n_used = nu_ref[0]
    ff = w2_ref.shape[0]
    ys = (y0, y1, y2)

    def start_scatter(idx_ref, q):
        for r in range(MOE_TM):
            pltpu.make_async_copy(ys[q].at[pl.ds(r * SUBLANES, SUBLANES)], y_hbm.at[idx_ref[0, r]],
                                  ssem.at[q]).start(priority=r % 2)

    def wait_scatter(q):
        pltpu.make_async_copy(ys[q], ys[q], ssem.at[q]).wait()

    @pl.when(i == 0)
    def _():
        n_asg = y_hbm.shape[0] - 2 * MOE_TM
        for q in range(N_YBUF):
            ys[q][...] = jnp.zeros_like(ys[q])
        for q in range(2):
            for r in range(MOE_TM):
                pltpu.make_async_copy(ys[q].at[pl.ds(r * SUBLANES, SUBLANES)],
                                      y_hbm.at[n_asg + q * MOE_TM + r], ssem.at[q]).start()
            wait_scatter(q)

    @pl.when(jnp.logical_and(i < n_used,
                             jnp.logical_or(i == 0, te_ref[i] != te_ref[jnp.maximum(i - 1, 0)])))
    def _():
        w1b[...] = w1_ref[...].astype(BF16)
        w2b[...] = w2_ref[...].astype(BF16)

    def step(q):
        q_prev, q_old = (q + 2) % N_YBUF, (q + 1) % N_YBUF
        start_scatter(prv_ref, q_prev)
        x = _rows_from_slabs(x_ref).astype(BF16)
        h1 = jnp.dot(x, w1b[...], preferred_element_type=F32) + b1_ref[...]
        gate = jnp.minimum(h1[:, :ff], SWIGLU_LIMIT)
        up = jnp.clip(h1[:, ff:], -SWIGLU_LIMIT, SWIGLU_LIMIT)
        act = gate * jax.nn.sigmoid(SWIGLU_ALPHA * gate) * (up + 1.0)
        _rows_to_slabs(ys[q], jnp.dot(act.astype(BF16), w2b[...], preferred_element_type=F32) + b2_ref[...])

        @pl.when(i >= 1)
        def _():
            wait_scatter(q_old)

    def drain(q):
        q_prev, q_old = (q + 2) % N_YBUF, (q + 1) % N_YBUF
        start_scatter(prv_ref, q_prev)
        wait_scatter(q_old)
        wait_scatter(q_prev)

    for q in range(N_YBUF):
        pl.when(jnp.logical_and(i % N_YBUF == q, i < n_used))(functools.partial(step, q))
        pl.when(jnp.logical_and(i % N_YBUF == q, i == n_used))(functools.partial(drain, q))


def _moe(layer, n_tiles, tile_expert, n_used, slot_dst, x_sorted, w1, b1, w2, b2, n_rows_out):
    depth, n_exp, d, ff2 = w1.shape
    ff = w2.shape[2]
    smem_tile = lambda f: pl.BlockSpec((None, 1, MOE_TM), f, memory_space=pltpu.SMEM)
    grid_spec = pltpu.PrefetchScalarGridSpec(
        num_scalar_prefetch=2,
        grid=(n_tiles + 1,),
        in_specs=[smem_tile(lambda i, te, nu: (jnp.minimum(i, nu[0]), 0, 0)),
                  pl.BlockSpec((MOE_TM * SUBLANES, LANES), lambda i, te, nu: (jnp.minimum(i, nu[0] - 1), 0)),
                  pl.BlockSpec((None, None, d, ff2), lambda i, te, nu: (layer, te[i], 0, 0)),
                  pl.BlockSpec((None, None, 1, ff2), lambda i, te, nu: (layer, te[i], 0, 0)),
                  pl.BlockSpec((None, None, ff, d), lambda i, te, nu: (layer, te[i], 0, 0)),
                  pl.BlockSpec((None, None, 1, d), lambda i, te, nu: (layer, te[i], 0, 0))],
        out_specs=pl.BlockSpec(memory_space=pl.ANY),
        scratch_shapes=[pltpu.VMEM((MOE_TM * SUBLANES, LANES), F32)] * N_YBUF
        + [pltpu.VMEM((d, ff2), BF16),
           pltpu.VMEM((ff, d), BF16),
           pltpu.SemaphoreType.DMA((N_YBUF,))],
    )
    dst3 = slot_dst.reshape(n_tiles + 1, 1, MOE_TM)
    return pl.pallas_call(
        _moe_kernel,
        grid_spec=grid_spec,
        out_shape=jax.ShapeDtypeStruct((n_rows_out, SUBLANES, LANES), F32),
        compiler_params=_cparams(1, VMEM_LIMIT),
        name="moe_experts",
    )(tile_expert, n_used, dst3, x_sorted.reshape(x_sorted.shape[0] * SUBLANES, LANES),
      w1, b1.reshape(depth, n_exp, 1, ff2), w2, b2.reshape(depth, n_exp, 1, d))


def _moe_plan(cfg, route, counts):
    n = cfg["n"]
    m = n * TOP_K
    n_tiles = m // MOE_TM + N_EXPERTS
    m_pad = n_tiles * MOE_TM
    top_i = route[:, 0:TOP_K].astype(jnp.int32)
    rank = route[:, 2 * TOP_K:3 * TOP_K].astype(jnp.int32)
    cnt = counts[0, :N_EXPERTS].astype(jnp.int32)
    tiles_e = (cnt + MOE_TM - 1) // MOE_TM
    tile_end = jnp.cumsum(tiles_e)
    pstart = (tile_end - tiles_e) * MOE_TM
    expert = jnp.arange(N_EXPERTS, dtype=jnp.int32)
    pos = jnp.sum(jnp.where(top_i[..., None] == expert, pstart, 0), axis=-1) + rank
    t = jnp.arange(n_tiles + 1, dtype=jnp.int32)
    te = jnp.sum((tile_end[None, :] <= t[:, None]).astype(jnp.int32), axis=1)
    last_used = jnp.sum((tile_end <= tile_end[-1] - 1).astype(jnp.int32))
    te = jnp.minimum(te, last_used).astype(jnp.int32)
    n_used = tile_end[-1]
    spare = n_used + jnp.arange(n_tiles - m // MOE_TM, dtype=jnp.int32)
    zero_tiles = jnp.concatenate([jnp.where(tiles_e > 0, tile_end - 1, -1),
                                  jnp.where(spare < n_tiles, spare, -1)]).astype(jnp.int32)
    return n_tiles, te, n_used.reshape(1).astype(jnp.int32), pos, zero_tiles, m + 2 * MOE_TM


def _combine_kernel(cfg, x_ref, g2p_ref, g2s_ref, route_ref, *rest):
    y_refs, o_ref = rest[:TOP_K], rest[TOP_K]
    i = pl.program_id(0)
    is_p = i < cfg["n_pt"]
    route = route_ref[...]
    acc = route[:, TOP_K:TOP_K + 1] * _rows_from_slabs(y_refs[0])
    for k in range(1, TOP_K):
        acc = acc + route[:, TOP_K + k:TOP_K + k + 1] * _rows_from_slabs(y_refs[k])
    g2 = jnp.where(is_p, g2p_ref[...], g2s_ref[...])
    o_ref[...] = x_ref[...] + g2 * acc.reshape(GP, SUBLANES, cfg["d"])


def _combine(cfg, x1, mp, ms, route, ybuf):
    d, n_t = cfg["d"], cfg["n_t"]
    y_specs = [pl.BlockSpec((TM * SUBLANES, LANES), lambda i, k=k: (k * n_t + i, 0)) for k in range(TOP_K)]
    ybuf = ybuf.reshape(ybuf.shape[0] * SUBLANES, LANES)
    return pl.pallas_call(
        functools.partial(_combine_kernel, cfg),
        grid=(n_t,),
        in_specs=[pl.BlockSpec((GP, SUBLANES, d), lambda i: (i, 0, 0))] + _mod_specs(cfg, 5)
        + [pl.BlockSpec((TM, LANES), lambda i: (i, 0))] + y_specs,
        out_specs=pl.BlockSpec((GP, SUBLANES, d), lambda i: (i, 0, 0)),
        out_shape=jax.ShapeDtypeStruct(x1.shape, F32),
        compiler_params=_cparams(1, VMEM_LIMIT),
        name="moe_combine",
    )(x1, mp, ms, route, *([ybuf] * TOP_K))


def kernel(x_prompt, x_sample, cache_kv_w128_d1, cache_kv_w512_d4, cache_kv_w2048_d16, state_pool,
           c_prompt, c_sample, rel_bias, norm_mix, norm_ffn, w_ada, b_ada, w_in, q_norm, k_norm,
           pool_w, pool_scale, w_out, w_router, b_router, w_expert_in, b_expert_in,
           w_expert_out, b_expert_out):
    b, s_len, d = x_prompt.shape
    bd, t_new, _ = x_sample.shape
    depth = w_in.shape[0]
    np_, ns = b * s_len, bd * t_new
    assert t_new == SUBLANES and s_len % TM == 0 and ns % TM == 0
    assert all((s_len // dil) % BAND == 0 for _, dil in ATT_GROUPS)
    cfg = dict(b=b, s=s_len, d=d, bd=bd, t=t_new, np=np_, ns=ns, n=np_ + ns,
               n_pt=np_ // TM, n_t=(np_ + ns) // TM, tps=s_len // TM)
    kv_keep = []
    for win, _ in ATT_GROUPS:
        keep = min(win, s_len)
        assert keep % TM == 0 or TM % keep == 0
        kv_keep.append((keep, max(keep // TM, 1)))
    cfg["kv_keep"] = tuple(kv_keep)
    caches_t = [jnp.transpose(c, (0, 1, 3, 4, 5, 2)).reshape(depth, bd, 2 * GROUP_DIM, c.shape[2])
                for c in (cache_kv_w128_d1, cache_kv_w512_d4, cache_kv_w2048_d16)]

    mods = _ada_mods(jnp.concatenate([c_prompt, c_sample], axis=0), w_ada, b_ada)
    bias_p = _prompt_bias_tables(rel_bias)
    bias_s = _sample_bias_tables(rel_bias, t_new)
    w_in_b = w_in.astype(BF16)
    w_out_b = w_out.astype(BF16)
    eye = jnp.eye(len(POOL_WINDOWS), dtype=F32)
    w_pool_bd = (eye[None, :, None, :, None] * pool_w[:, :, :, None, :]).reshape(depth, POOL_DIM, POOL_DIM)
    w_pool_bd = w_pool_bd.astype(BF16)
    w_r_pad = jnp.pad(w_router, ((0, 0), (0, 0), (0, LANES - N_EXPERTS)))
    b_r_pad = jnp.pad(b_router, ((0, 0), (0, LANES - N_EXPERTS)), constant_values=NEG_INF)

    x3 = jnp.concatenate([x_prompt.reshape(np_, d), x_sample.reshape(ns, d)], axis=0)
    x3 = x3.reshape((np_ + ns) // SUBLANES, SUBLANES, d)

    kv_p = [[] for _ in ATT_GROUPS]
    kv_s = [[] for _ in ATT_GROUPS]
    pool_p_state, pool_s_state = [], []
    for l in range(depth):
        mp = mods[l, :b].reshape(b, 1, 6 * d)
        ms = mods[l, b:].reshape(bd, 1, 6 * d)
        qg = jnp.tile(q_norm[l], HEADS).reshape(1, GROUP_DIM)
        kg = jnp.tile(k_norm[l], HEADS).reshape(1, GROUP_DIM)
        (u_p, u_s, qkv0, qkv1, qkv2, qs, kvs, *kvp) = _inproj(
            cfg, x3, mp, ms, norm_mix[l].reshape(1, d), w_in_b[l], qg, kg)

        att = []
        for g, qkv in enumerate((qkv0, qkv1, qkv2)):
            o_p, lse_p = _attn_prompt(cfg, g, qkv, bias_p[g])
            o_s, lse_s = _attn_sample(cfg, g, l, qs, kvs, caches_t[g], bias_s[g])
            att.append((o_p, lse_p, o_s, lse_s))
            keep = cfg["kv_keep"][g][0]
            kv_p[g].append(kvp[g].reshape(b, keep, 2, HEADS, HEAD_DIM))
            kv_s[g].append(kvs[g].reshape(bd, t_new, 2, HEADS, HEAD_DIM))

        scale = pool_scale[l].reshape(1, POOL_DIM)
        pool_p = _pool_prompt(cfg, u_p, w_pool_bd[l], scale)
        pool_s = _pool_sample(cfg, l, u_s, state_pool, w_pool_bd[l], scale)
        pool_p_state.append(u_p.reshape(b, s_len, POOL_DIM)[:, s_len - POOL_HIST:])
        pool_s_state.append(jnp.concatenate(
            [state_pool[l], u_s.reshape(bd, t_new, POOL_DIM)], axis=1)[:, -POOL_HIST:])

        x1, h2, route, counts = _outproj(cfg, x3, mp, ms, pool_p, pool_s, att,
                                         norm_ffn[l].reshape(1, d), w_out_b[l], w_r_pad[l],
                                         b_r_pad[l].reshape(1, LANES))
        n_tiles, te, n_used, pos, zero_tiles, n_rows_out = _moe_plan(cfg, route, counts)
        x_sorted, slot_dst = _dispatch(cfg, pos, zero_tiles, h2, n_tiles * MOE_TM)
        ybuf = _moe(l, n_tiles, te, n_used, slot_dst, x_sorted,
                    w_expert_in, b_expert_in, w_expert_out, b_expert_out, n_rows_out)
        x3 = _combine(cfg, x1, mp, ms, route, ybuf)

    x_all = x3.reshape(np_ + ns, d)
    y_prompt = x_all[:np_].reshape(b, s_len, d)
    y_sample = x_all[np_:].reshape(bd, t_new, d)
    return (y_prompt, y_sample,
            jnp.stack(kv_p[0]), jnp.stack(kv_p[1]), jnp.stack(kv_p[2]), jnp.stack(pool_p_state),
            jnp.stack(kv_s[0]), jnp.stack(kv_s[1]), jnp.stack(kv_s[2]), jnp.stack(pool_s_state))
```
